```python
import jax, jax.numpy as jnp
from jax import lax
import numpy as np

D_MODEL = 1024
BATCH = 16
SEQ = 256
DEPTH = 4
DEC_BATCH = 4
DEC_SEQ = 1024
PAST_LEN = 512

GRID_W = 64
EPS = 1e-6
MLA_HEADS = 8
MLA_NOPE = 64
MLA_ROPE = 32
MLA_V = 64
Q_RANK = 384
KV_RANK = 256
ROPE_BASE = 10000.0
ATTN_BLOCK = 128
DENSE_KEY_LIMIT = 2048
GLA_HEADS = 8
GLA_DK = 64
GLA_DV = 64
GLA_W = GLA_HEADS * GLA_DK
GLA_GATE_RANK = 16
GLA_TAU = 16.0
GLA_CHUNK = 64
N_EXPERTS = 16
EXPERT_FF = 1024
CAPACITY_FACTOR = 2
IN_SIZES = (Q_RANK, KV_RANK, MLA_ROPE, GLA_W, GLA_W, GLA_W, 2 * GLA_GATE_RANK, GLA_W, D_MODEL, D_MODEL)
IN_COLS = sum(IN_SIZES)

kernel_name = 'hybrid_mla_gla_ec_diffusion_step'


def rms_norm(x, g):
    xf = x.astype(jnp.float32)
    y = xf * lax.rsqrt(jnp.mean(xf * xf, axis=-1, keepdims=True) + EPS)
    return (y * g.astype(jnp.float32)).astype(x.dtype)


def adaln_params(cvec, w_mod, b_mod):
    mod = jax.nn.silu(cvec) @ w_mod + b_mod
    return [m[:, None, :] for m in jnp.split(mod, 6, axis=-1)]


def split_proj(p):
    idx = [int(i) for i in np.cumsum(IN_SIZES)[:-1]]
    return jnp.split(p, idx, axis=-1)


def axial_rope(x, row_pos, col_pos):
    n = x.shape[1]
    half = MLA_ROPE // 2
    npair = half // 2
    freqs = ROPE_BASE ** (-jnp.arange(npair, dtype=jnp.float32) / npair)

    def rot(seg, pos):
        ang = pos.astype(jnp.float32)[:, None] * freqs
        ang = ang.reshape((n,) + (1,) * (x.ndim - 3) + (npair,))
        cos = jnp.cos(ang).astype(x.dtype)
        sin = jnp.sin(ang).astype(x.dtype)
        a, b = seg[..., :npair], seg[..., npair:]
        return jnp.concatenate([a * cos - b * sin, a * sin + b * cos], axis=-1)

    return jnp.concatenate([rot(x[..., :half], row_pos), rot(x[..., half:], col_pos)], axis=-1)


def mla_expand(ckv, lp):
    b, n, _ = ckv.shape
    k_nope = (ckv @ lp['w_uk']).reshape(b, n, MLA_HEADS, MLA_NOPE)
    v = (ckv @ lp['w_uv']).reshape(b, n, MLA_HEADS, MLA_V)
    return k_nope, v


def attend(q_nope, q_rope, k_nope, k_rope, v):
    scale = (MLA_NOPE + MLA_ROPE) ** -0.5

    def block(qn, qr):
        s = jnp.einsum('bqhd,bkhd->bhqk', qn, k_nope) + jnp.einsum('bqhr,bkr->bhqk', qr, k_rope)
        p = jax.nn.softmax(s.astype(jnp.float32) * scale, axis=-1).astype(v.dtype)
        return jnp.einsum('bhqk,bkhd->bqhd', p, v)

    b, nq = q_nope.shape[:2]
    if k_nope.shape[1] < DENSE_KEY_LIMIT:
        return block(q_nope, q_rope)
    nb = nq // ATTN_BLOCK
    qn = q_nope.reshape(b, nb, ATTN_BLOCK, MLA_HEADS, MLA_NOPE).swapaxes(0, 1)
    qr = q_rope.reshape(b, nb, ATTN_BLOCK, MLA_HEADS, MLA_ROPE).swapaxes(0, 1)
    o = lax.map(lambda a: block(a[0], a[1]), (qn, qr))
    return o.swapaxes(0, 1).reshape(b, nq, MLA_HEADS, MLA_V)


def gla_scan(q, k, v, g, s0):
    b, h, n, dk = q.shape
    dv = v.shape[-1]
    nc = n // GLA_CHUNK

    def chunks(t):
        return t.reshape(b, h, nc, GLA_CHUNK, t.shape[-1]).transpose(2, 0, 1, 3, 4)

    mask = jnp.tril(jnp.ones((GLA_CHUNK, GLA_CHUNK), dtype=bool))

    def step(S, inp):
        qc, kc, vc, gc = inp
        cum = jnp.cumsum(gc, axis=2)
        o_inter = jnp.einsum('bhtk,bhkv->bhtv', qc * jnp.exp(cum), S)
        diff = cum[:, :, :, None, :] - cum[:, :, None, :, :]
        decay = jnp.exp(jnp.where(mask[:, :, None], diff, -jnp.inf))
        a = jnp.einsum('bhtk,bhsk,bhtsk->bhts', qc, kc, decay)
        o = o_inter + jnp.einsum('bhts,bhsv->bhtv', a, vc)
        last = cum[:, :, -1:, :]
        S_new = jnp.exp(last[:, :, 0, :])[..., None] * S + jnp.einsum('bhsk,bhsv->bhkv', kc * jnp.exp(last - cum), vc)
        return S_new, o

    S, o = lax.scan(step, s0, (chunks(q), chunks(k), chunks(v), chunks(g)))
    o = o.transpose(1, 2, 0, 3, 4).reshape(b, h, n, dv)
    return o, S


def gla_branch(gq, gk, gv, glr, gog, lp, s0_f, s0_b):
    b, n, _ = gq.shape

    def to_heads(t):
        return t.reshape(b, n, GLA_HEADS, -1).transpose(0, 2, 1, 3).astype(jnp.float32)

    q = to_heads(gq) * (GLA_DK ** -0.5)
    k = to_heads(gk)
    v = to_heads(gv)
    lr = glr.reshape(b, n, 2, GLA_GATE_RANK)
    logit = jnp.einsum('bnzr,zrk->zbnk', lr, lp['w_gla_gate']) + lp['b_gla_gate'][:, None, None, :]
    log_alpha = jax.nn.log_sigmoid(logit.astype(jnp.float32)) / GLA_TAU
    g_f = to_heads(log_alpha[0])
    g_b = to_heads(log_alpha[1])
    o_f, S_f = gla_scan(q, k, v, g_f, s0_f.astype(jnp.float32))
    flip = lambda t: jnp.flip(t, axis=2)
    o_b, S_b = gla_scan(flip(q), flip(k), flip(v), flip(g_b), s0_b.astype(jnp.float32))
    o = o_f + flip(o_b)
    o = o * lax.rsqrt(jnp.mean(o * o, axis=-1, keepdims=True) + EPS)
    o = o.transpose(0, 2, 1, 3).reshape(b, n, GLA_HEADS * GLA_DV) * lp['g_gla'].astype(jnp.float32)
    o = o.astype(gq.dtype) * jax.nn.silu(gog)
    return o @ lp['w_o_gla'], jnp.stack([S_f, S_b], axis=1)


def mixer(h, lp, pos, ctx_ckv, ctx_kr, s0_f, s0_b):
    b, n, _ = h.shape
    pq, pkv, kr, gq, gk, gv, glr, gog, ga, gb = split_proj(h @ lp['w_in'])
    cq = rms_norm(pq, lp['g_q'])
    q = (cq @ lp['w_uq']).reshape(b, n, MLA_HEADS, MLA_NOPE + MLA_ROPE)
    q_nope, q_rope = q[..., :MLA_NOPE], q[..., MLA_NOPE:]
    ckv = rms_norm(pkv, lp['g_kv'])
    k_nope, v = mla_expand(ckv, lp)
    if pos is None:
        k_nope_all, v_all, kr_all = k_nope, v, kr
    else:
        q_rope = axial_rope(q_rope, pos[0], pos[1])
        kr_lat = axial_rope(kr, pos[0], pos[1])
        k_nope_ctx, v_ctx = mla_expand(ctx_ckv.astype(h.dtype), lp)
        k_nope_all = jnp.concatenate([k_nope_ctx, k_nope], axis=1)
        v_all = jnp.concatenate([v_ctx, v], axis=1)
        kr_all = jnp.concatenate([ctx_kr.astype(h.dtype), kr_lat], axis=1)
    o_mla = attend(q_nope, q_rope, k_nope_all, kr_all, v_all).reshape(b, n, MLA_HEADS * MLA_V) @ lp['w_o_mla']
    o_gla, S = gla_branch(gq, gk, gv, glr, gog, lp, s0_f, s0_b)
    merged = jax.nn.sigmoid(ga) * o_mla + jax.nn.sigmoid(gb) * o_gla
    return merged @ lp['w_out'], ckv, kr, S


def expert_choice_ffn(h, lp):
    shp = h.shape
    x = h.reshape(-1, shp[-1])
    t = x.shape[0]
    cap = max(1, CAPACITY_FACTOR * t // N_EXPERTS)
    aff = jax.nn.softmax((x @ lp['w_router']).astype(jnp.float32), axis=-1)
    w_sel, idx = lax.top_k(aff.T, cap)
    xs = x[idx]
    hid = jax.nn.silu(jnp.einsum('ecd,edf->ecf', xs, lp['w_e_gate'])) * jnp.einsum('ecd,edf->ecf', xs, lp['w_e_up'])
    ye = jnp.einsum('ecf,efd->ecd', hid, lp['w_e_down']) * w_sel[..., None].astype(x.dtype)
    out = jnp.zeros_like(x).at[idx.reshape(-1)].add(ye.reshape(-1, shp[-1]))
    return out.reshape(shp)


def block(x, lp, cvec, pos, ctx_ckv, ctx_kr, s0_f, s0_b):
    sh1, sc1, gt1, sh2, sc2, gt2 = adaln_params(cvec, lp['w_mod'], lp['b_mod'])
    h = rms_norm(x, lp['g_norm1']) * (1 + sc1) + sh1
    mix, ckv, kr, S = mixer(h, lp, pos, ctx_ckv, ctx_kr, s0_f, s0_b)
    x = x + gt1 * mix
    h = rms_norm(x, lp['g_norm2']) * (1 + sc2) + sh2
    x = x + gt2 * expert_choice_ffn(h, lp)
    return x, ckv, kr, S


def setup_inputs(seed: int = 0) -> dict:
    key = jax.random.key(seed)
    ks = iter(jax.random.split(key, 40))

    def nrm(shape, scale):
        return jax.random.normal(next(ks), shape, jnp.float32) * scale

    D = D_MODEL
    L = DEPTH
    return {
        'x_prompt': nrm((BATCH, SEQ, D), 1.0),
        'x_sample': nrm((DEC_BATCH, DEC_SEQ, D), 1.0),
        'cache_ckv': nrm((DEC_BATCH, L, PAST_LEN, KV_RANK), 1.0),
        'cache_krope': nrm((DEC_BATCH, L, PAST_LEN, MLA_ROPE), 1.0),
        'state_gla': nrm((DEC_BATCH, L, 2, GLA_HEADS, GLA_DK, GLA_DV), 1.0),
        'c': nrm((DEC_BATCH, D), 1.0),
        'c_ctx': nrm((D,), 1.0),
        'w_mod': nrm((L, D, 6 * D), 0.5 * D ** -0.5),
        'b_mod': nrm((L, 6 * D), 0.02),
        'g_norm1': 1.0 + nrm((L, D), 0.02),
        'g_norm2': 1.0 + nrm((L, D), 0.02),
        'w_in': nrm((L, D, IN_COLS), D ** -0.5),
        'g_q': 1.0 + nrm((L, Q_RANK), 0.02),
        'g_kv': 1.0 + nrm((L, KV_RANK), 0.02),
        'w_uq': nrm((L, Q_RANK, MLA_HEADS * (MLA_NOPE + MLA_ROPE)), Q_RANK ** -0.5),
        'w_uk': nrm((L, KV_RANK, MLA_HEADS * MLA_NOPE), KV_RANK ** -0.5),
        'w_uv': nrm((L, KV_RANK, MLA_HEADS * MLA_V), KV_RANK ** -0.5),
        'w_o_mla': nrm((L, MLA_HEADS * MLA_V, D), (MLA_HEADS * MLA_V) ** -0.5),
        'w_gla_gate': nrm((L, 2, GLA_GATE_RANK, GLA_W), GLA_GATE_RANK ** -0.5),
        'b_gla_gate': nrm((L, 2, GLA_W), 0.1),
        'g_gla': 1.0 + nrm((L, GLA_HEADS * GLA_DV), 0.02),
        'w_o_gla': nrm((L, GLA_HEADS * GLA_DV, D), (GLA_HEADS * GLA_DV) ** -0.5),
        'w_out': nrm((L, D, D), D ** -0.5),
        'w_router': nrm((L, D, N_EXPERTS), D ** -0.5),
        'w_e_gate': nrm((L, N_EXPERTS, D, EXPERT_FF), D ** -0.5),
        'w_e_up': nrm((L, N_EXPERTS, D, EXPERT_FF), D ** -0.5),
        'w_e_down': nrm((L, N_EXPERTS, EXPERT_FF, D), EXPERT_FF ** -0.5),
        'g_final': 1.0 + nrm((D,), 0.02),
    }


def reference(x_prompt, x_sample, cache_ckv, cache_krope, state_gla, c, c_ctx, w_mod, b_mod, g_norm1, g_norm2,
              w_in, g_q, g_kv, w_uq, w_uk, w_uv, w_o_mla, w_gla_gate, b_gla_gate, g_gla, w_o_gla, w_out,
              w_router, w_e_gate, w_e_up, w_e_down, g_final):
    n_lat = x_sample.shape[1]
    ROWS = n_lat // GRID_W
    row_pos = jnp.repeat(jnp.arange(ROWS), GRID_W)
    col_pos = jnp.tile(jnp.arange(GRID_W), ROWS)
    bp = x_prompt.shape[0]
    zero_state = jnp.zeros((bp, GLA_HEADS, GLA_DK, GLA_DV), jnp.float32)
    xp = x_prompt
    xs = x_sample
    ckv_list, kr_list, gla_list = [], [], []
    for l in range(DEPTH):
        lp = {
            'w_mod': w_mod[l], 'b_mod': b_mod[l], 'g_norm1': g_norm1[l], 'g_norm2': g_norm2[l],
            'w_in': w_in[l], 'g_q': g_q[l], 'g_kv': g_kv[l], 'w_uq': w_uq[l], 'w_uk': w_uk[l], 'w_uv': w_uv[l],
            'w_o_mla': w_o_mla[l], 'w_gla_gate': w_gla_gate[l], 'b_gla_gate': b_gla_gate[l], 'g_gla': g_gla[l],
            'w_o_gla': w_o_gla[l], 'w_out': w_out[l], 'w_router': w_router[l], 'w_e_gate': w_e_gate[l],
            'w_e_up': w_e_up[l], 'w_e_down': w_e_down[l],
        }
        xp, ckv, kr, S = block(xp, lp, c_ctx[None, :], None, None, None, zero_state, zero_state)
        ckv_list.append(ckv)
        kr_list.append(kr)
        gla_list.append(S.astype(x_prompt.dtype))
        xs, _, _, _ = block(xs, lp, c, (row_pos, col_pos), cache_ckv[:, l], cache_krope[:, l],
                            state_gla[:, l, 0], state_gla[:, l, 1])
    y_prompt = rms_norm(xp, g_final)
    y_sample = rms_norm(xs, g_final)
    new_ckv = jnp.stack(ckv_list, axis=1)
    new_krope = jnp.stack(kr_list, axis=1)
    new_gla = jnp.stack(gla_list, axis=1)
    return (y_prompt, y_sample, new_ckv, new_krope, new_gla)
```

```python
import functools

import jax
import jax.numpy as jnp
import numpy as np
from jax import lax
from jax.experimental import pallas as pl
from jax.experimental.pallas import tpu as pltpu

F32 = jnp.float32
BF16 = jnp.bfloat16
I32 = jnp.int32

GRID_W = 64
EPS = 1e-6
MLA_HEADS = 8
MLA_NOPE = 64
MLA_ROPE = 32
MLA_V = 64
Q_RANK = 384
KV_RANK = 256
ROPE_BASE = 10000.0
GLA_HEADS = 8
GLA_DK = 64
GLA_DV = 64
GLA_W = GLA_HEADS * GLA_DK
GLA_GATE_RANK = 16
GLA_TAU = 16.0
GLA_CHUNK = 64
N_EXPERTS = 16
CAPACITY_FACTOR = 2

LANES = 128
SUBLANES = 8
HEAD_PAD = 128
VMEM_LIMIT = 56 * 1024 * 1024

ROW_TILE = 256
GLA_QUAD = 4 * GLA_DK
GLA_DIAG = 8

_SEG = {}
_off = 0
for _name, _width in (("pq", Q_RANK), ("pkv", KV_RANK), ("kr", HEAD_PAD), ("krs", HEAD_PAD),
                      ("gq", GLA_W), ("gk", GLA_W), ("gv", GLA_W), ("glr", LANES),
                      ("gog", GLA_W), ("ga", None), ("gb", None)):
    _SEG[_name] = (_off, _width)
    if _width is not None:
        _off += _width
_PACK_FIXED = _off


def _cparams(sem, vmem=VMEM_LIMIT):
    return pltpu.CompilerParams(dimension_semantics=sem, vmem_limit_bytes=vmem)


def _dot(a, b):
    return jnp.dot(a, b, preferred_element_type=F32)


def _dot_nt(a, b):
    return lax.dot_general(a, b, (((1,), (1,)), ((), ())), preferred_element_type=F32)


def _dot_tn(a, b):
    return lax.dot_general(a, b, (((0,), (0,)), ((), ())), preferred_element_type=F32)


def _rms(x, g):
    return x * lax.rsqrt(jnp.mean(x * x, axis=-1, keepdims=True) + EPS) * g


def _sigmoid(x):
    return 1.0 / (1.0 + jnp.exp(-x))


def _split3(x):
    a = x.astype(BF16)
    r = x - a.astype(F32)
    b = r.astype(BF16)
    c = (r - b.astype(F32)).astype(BF16)
    return a, b, c


def _mod_body(c_ref, w_ref, b_ref, o_ref):
    c = c_ref[...]
    s = (c * _sigmoid(c)).astype(BF16)
    o_ref[0] = _dot(s, w_ref[0].astype(BF16)) + b_ref[0]


def _adaln_all(cvec, w_mod, b_mod):
    L, D, D6 = w_mod.shape
    R = cvec.shape[0]
    tn = 1536
    return pl.pallas_call(
        _mod_body,
        grid=(L, D6 // tn),
        in_specs=[pl.BlockSpec((R, D), lambda l, j: (0, 0)),
                  pl.BlockSpec((1, D, tn), lambda l, j: (l, 0, j)),
                  pl.BlockSpec((1, 1, tn), lambda l, j: (l, 0, j))],
        out_specs=pl.BlockSpec((1, R, tn), lambda l, j: (l, 0, j)),
        out_shape=jax.ShapeDtypeStruct((L, R, D6), F32),
        compiler_params=_cparams(("arbitrary", "arbitrary")),
        name="adaln_mod",
    )(cvec, w_mod, b_mod.reshape(L, 1, D6))


def _inproj_body(x_ref, mod_ref, g1_ref, w_ref, gq_ref, gkv_ref, wuq_ref, wuk_ref, wuv_ref,
                 wg_ref, bg_ref, ct_ref, st_ref,
                 q_ref, k_ref, v_ref, ckv_ref, kr_ref, gqo_ref, gko_ref, gvo_ref, g_ref,
                 sg_ref, sa_ref, sb_ref, *, D):
    x = x_ref[...]
    mod = mod_ref[0]
    h = (_rms(x, g1_ref[...]) * (1.0 + mod[:, D:2 * D]) + mod[:, 0:D]).astype(BF16)

    def seg(name, width=None):
        a, w = _SEG[name]
        w = width if w is None else w
        return _dot(h, w_ref[:, a:a + w])

    cos = ct_ref[...]
    sin = st_ref[...]
    nq = MLA_HEADS * HEAD_PAD

    cq = _rms(seg("pq"), gq_ref[...]).astype(BF16)
    qq = _dot(cq, wuq_ref[...])
    for hd in range(MLA_HEADS):
        a = hd * HEAD_PAD
        q_ref[:, a:a + HEAD_PAD] = (qq[:, a:a + HEAD_PAD] * cos
                                    + qq[:, nq + a:nq + a + HEAD_PAD] * sin).astype(BF16)

    ckv = _rms(seg("pkv"), gkv_ref[...])
    ckv_ref[...] = ckv
    ckv_b = ckv.astype(BF16)
    kr = seg("kr")
    kr_ref[...] = kr
    kr_rot = kr * cos + seg("krs") * sin
    kn = _dot(ckv_b, wuk_ref[...])
    for hd in range(MLA_HEADS):
        a = hd * HEAD_PAD
        k_ref[:, a:a + HEAD_PAD] = (kn[:, a:a + HEAD_PAD] + kr_rot).astype(BF16)
    v_ref[...] = _dot(ckv_b, wuv_ref[...]).astype(BF16)

    gqo_ref[...] = seg("gq") * (GLA_DK ** -0.5)
    gko_ref[...] = seg("gk")
    gvo_ref[...] = seg("gv")
    logit = _dot(seg("glr").astype(BF16), wg_ref[...]) + bg_ref[...]
    g_ref[...] = (jnp.minimum(logit, 0.0) - jnp.log1p(jnp.exp(-jnp.abs(logit)))) * (1.0 / GLA_TAU)
    gog = seg("gog")
    sg_ref[...] = gog * _sigmoid(gog)
    a0 = _SEG["ga"][0]
    sa_ref[...] = _sigmoid(_dot(h, w_ref[:, a0:a0 + D]))
    sb_ref[...] = _sigmoid(_dot(h, w_ref[:, a0 + D:a0 + 2 * D]))


def _inproj(x, mod_l, g1, wp, gq, gkv, wuq, wuk, wuv, wg, bg, cos_t, sin_t, cond_of_tile, tab_of_tile):
    T2, D = x.shape
    nt = T2 // ROW_TILE
    P = wp.shape[1]
    R = mod_l.shape[0]
    row = lambda w: pl.BlockSpec((ROW_TILE, w), lambda i: (i, 0))
    full = lambda a: pl.BlockSpec(a.shape, lambda i: (0,) * a.ndim)
    outs = [("q", MLA_HEADS * HEAD_PAD, BF16), ("k", MLA_HEADS * HEAD_PAD, BF16),
            ("v", MLA_HEADS * MLA_V, BF16), ("ckv", KV_RANK, F32), ("kr", HEAD_PAD, F32),
            ("gq", GLA_W, F32), ("gk", GLA_W, F32), ("gv", GLA_W, F32), ("g", 2 * GLA_W, F32),
            ("sg", GLA_W, F32), ("sa", D, F32), ("sb", D, F32)]
    res = pl.pallas_call(
        functools.partial(_inproj_body, D=D),
        grid=(nt,),
        in_specs=[row(D),
                  pl.BlockSpec((1, 1, 6 * D), lambda i: (cond_of_tile(i), 0, 0)),
                  full(g1), full(wp), full(gq), full(gkv), full(wuq), full(wuk), full(wuv),
                  full(wg), full(bg),
                  pl.BlockSpec((ROW_TILE, HEAD_PAD), lambda i: (tab_of_tile(i), 0)),
                  pl.BlockSpec((ROW_TILE, HEAD_PAD), lambda i: (tab_of_tile(i), 0))],
        out_specs=[row(w) for _, w, _ in outs],
        out_shape=[jax.ShapeDtypeStruct((T2, w), dt) for _, w, dt in outs],
        compiler_params=_cparams(("arbitrary",)),
        name="inproj",
    )(x, mod_l.reshape(R, 1, 6 * D), g1, wp, gq, gkv, wuq, wuk, wuv, wg, bg, cos_t, sin_t)
    return dict(zip([n for n, _, _ in outs], res))


def _cache_kv_body(ckv_ref, krp_ref, wuk_ref, wuv_ref, k_ref, v_ref):
    c = ckv_ref[...].astype(BF16)
    kn = _dot(c, wuk_ref[...])
    krp = krp_ref[...]
    for hd in range(MLA_HEADS):
        a = hd * HEAD_PAD
        k_ref[:, a:a + HEAD_PAD] = (kn[:, a:a + HEAD_PAD] + krp).astype(BF16)
    v_ref[...] = _dot(c, wuv_ref[...]).astype(BF16)


def _cache_kv(cache_ckv, cache_kr_pad, wuk, wuv):
    DB, L, P, R = cache_ckv.shape
    nk = MLA_HEADS * HEAD_PAD
    nv = MLA_HEADS * MLA_V
    return pl.pallas_call(
        _cache_kv_body,
        grid=(L, DB),
        in_specs=[pl.BlockSpec((None, None, P, R), lambda l, b: (b, l, 0, 0)),
                  pl.BlockSpec((None, None, P, HEAD_PAD), lambda l, b: (b, l, 0, 0)),
                  pl.BlockSpec((None, R, nk), lambda l, b: (l, 0, 0)),
                  pl.BlockSpec((None, R, nv), lambda l, b: (l, 0, 0))],
        out_specs=[pl.BlockSpec((None, None, P, nk), lambda l, b: (l, b, 0, 0)),
                   pl.BlockSpec((None, None, P, nv), lambda l, b: (l, b, 0, 0))],
        out_shape=[jax.ShapeDtypeStruct((L, DB, P, nk), BF16),
                   jax.ShapeDtypeStruct((L, DB, P, nv), BF16)],
        compiler_params=_cparams(("arbitrary", "arbitrary")),
        name="cache_kv",
    )(cache_ckv, cache_kr_pad, wuk, wuv)


def _attn_body(*refs, nseg):
    q_ref = refs[0]
    k_refs = refs[1:1 + nseg]
    v_refs = refs[1 + nseg:1 + 2 * nseg]
    o_ref = refs[1 + 2 * nseg]
    scale = (MLA_NOPE + MLA_ROPE) ** -0.5
    lane = lax.broadcasted_iota(I32, (1, 2 * MLA_V), 1)
    outs = []
    for hh in range(2):
        qh = q_ref[:, hh * HEAD_PAD:(hh + 1) * HEAD_PAD]
        s = [_dot_nt(qh, kr[:, hh * HEAD_PAD:(hh + 1) * HEAD_PAD]) * scale for kr in k_refs]
        m = s[0].max(axis=-1, keepdims=True)
        for sj in s[1:]:
            m = jnp.maximum(m, sj.max(axis=-1, keepdims=True))
        p = [jnp.exp(sj - m) for sj in s]
        den = p[0].sum(axis=-1, keepdims=True)
        for pj in p[1:]:
            den = den + pj.sum(axis=-1, keepdims=True)
        inv = 1.0 / den
        o = None
        for pj, vr in zip(p, v_refs):
            t = _dot((pj * inv).astype(BF16), vr[...])
            o = t if o is None else o + t
        outs.append(o)
    o_ref[...] = jnp.where(lane < MLA_V, outs[0], outs[1]).astype(BF16)


def _attention_ctx(q, k, v, nseq, n):
    hp = MLA_HEADS // 2
    return pl.pallas_call(
        functools.partial(_attn_body, nseg=1),
        grid=(nseq, hp),
        in_specs=[pl.BlockSpec((n, 2 * HEAD_PAD), lambda b, h: (b, h)),
                  pl.BlockSpec((n, 2 * HEAD_PAD), lambda b, h: (b, h)),
                  pl.BlockSpec((n, 2 * MLA_V), lambda b, h: (b, h))],
        out_specs=pl.BlockSpec((n, 2 * MLA_V), lambda b, h: (b, h)),
        out_shape=jax.ShapeDtypeStruct((nseq * n, MLA_HEADS * MLA_V), BF16),
        compiler_params=_cparams(("arbitrary", "arbitrary")),
        name="attn_ctx",
    )(q, k, v)


def _attention_lat(q, k, v, kc, vc, row0, nseq, n, tq):
    hp = MLA_HEADS // 2
    P = kc.shape[1]
    qt = n // tq
    q0 = row0 // tq
    s0 = row0 // n
    return pl.pallas_call(
        functools.partial(_attn_body, nseg=2),
        grid=(nseq, hp, qt),
        in_specs=[pl.BlockSpec((tq, 2 * HEAD_PAD), lambda b, h, t: (q0 + b * qt + t, h)),
                  pl.BlockSpec((None, P, 2 * HEAD_PAD), lambda b, h, t: (b, 0, h)),
                  pl.BlockSpec((n, 2 * HEAD_PAD), lambda b, h, t: (s0 + b, h)),
                  pl.BlockSpec((None, P, 2 * MLA_V), lambda b, h, t: (b, 0, h)),
                  pl.BlockSpec((n, 2 * MLA_V), lambda b, h, t: (s0 + b, h))],
        out_specs=pl.BlockSpec((tq, 2 * MLA_V), lambda b, h, t: (b * qt + t, h)),
        out_shape=jax.ShapeDtypeStruct((nseq * n, MLA_HEADS * MLA_V), BF16),
        compiler_params=_cparams(("arbitrary", "arbitrary", "arbitrary")),
        name="attn_lat",
    )(q, kc, k, vc, v)


def _gla_chunk(q, k, v, g, st_refs, fwd, tri, ones_bd):
    C = GLA_CHUNK
    W = GLA_W
    nquad = W // GLA_QUAD
    rows = lax.broadcasted_iota(I32, (C, 1), 0)

    g1, g2, g3 = _split3(g)
    cum = _dot(tri, g1) + _dot(tri, g2) + _dot(tri, g3)
    edge = C - 1 if fwd else 0
    last = cum[edge:edge + 1]
    q_in = (q * jnp.exp(cum)).astype(BF16)
    k_st = (k * jnp.exp(last - cum)).astype(BF16)
    v_b = v.astype(BF16)

    lane_q = lax.broadcasted_iota(I32, (1, GLA_QUAD), 1)
    head_masks = [jnp.right_shift(lane_q, 6) == h for h in range(GLA_QUAD // GLA_DK)]
    col_s = jnp.bitwise_and(lane_q, C - 1)

    def stack_heads(xq):
        return jnp.concatenate([jnp.where(mh, xq, jnp.zeros_like(xq)) for mh in head_masks], axis=0)

    a_acc = [jnp.zeros((C, GLA_QUAD), F32) for _ in range(nquad)]
    half = C // 2
    while half >= GLA_DIAG:
        blk = 2 * half
        pieces = []
        for p in range(C // blk):
            rr = p * blk + (half - 1 if fwd else half)
            pieces.append(jnp.broadcast_to(cum[rr:rr + 1], (blk, W)))
        ref = jnp.concatenate(pieces, axis=0) if len(pieces) > 1 else pieces[0]
        upper = jnp.bitwise_and(rows, blk - 1) >= half
        qmask = upper if fwd else jnp.logical_not(upper)
        qe = jnp.where(qmask, q * jnp.exp(jnp.minimum(cum - ref, 0.0)), 0.0).astype(BF16)
        ke = jnp.where(qmask, 0.0, k * jnp.exp(jnp.minimum(ref - cum, 0.0))).astype(BF16)
        sh = blk.bit_length() - 1
        same = jnp.right_shift(rows, sh) == jnp.right_shift(col_s, sh)
        for qd in range(nquad):
            sl = slice(qd * GLA_QUAD, (qd + 1) * GLA_QUAD)
            a = _dot_nt(qe[:, sl], stack_heads(ke[:, sl]))
            a_acc[qd] = a_acc[qd] + jnp.where(same, a, 0.0)
        half //= 2

    o_parts = []
    for qd in range(nquad):
        sl = slice(qd * GLA_QUAD, (qd + 1) * GLA_QUAD)
        st = st_refs[qd][...]
        o = _dot_nt(q_in[:, sl], st.astype(BF16))
        o = o + _dot(a_acc[qd].astype(BF16), stack_heads(v_b[:, sl]))
        o_parts.append(o)
        upd = _dot_tn(v_b[:, sl], k_st[:, sl])
        r2 = jnp.right_shift(lax.broadcasted_iota(I32, (GLA_QUAD, GLA_QUAD), 0), 6)
        c2 = jnp.right_shift(lax.broadcasted_iota(I32, (GLA_QUAD, GLA_QUAD), 1), 6)
        st_refs[qd][...] = st * jnp.exp(last[:, sl]) + jnp.where(r2 == c2, upd, 0.0)
    o = jnp.concatenate(o_parts, axis=1)

    nb = C // GLA_DIAG
    q3 = q.reshape(nb, GLA_DIAG, W)
    k3 = k.reshape(nb, GLA_DIAG, W)
    v3 = v.reshape(nb, GLA_DIAG, W)
    c3 = cum.reshape(nb, GLA_DIAG, W)
    tl = lax.broadcasted_iota(I32, (1, GLA_DIAG, 1), 1)
    od = jnp.zeros((nb, GLA_DIAG, W), F32)
    for s in range(GLA_DIAG):
        e = jnp.exp(jnp.minimum(c3 - c3[:, s:s + 1, :], 0.0))
        valid = (tl >= s) if fwd else (tl <= s)
        z = jnp.where(valid, q3 * e * k3[:, s:s + 1, :], 0.0)
        w = _dot(z.reshape(C, W).astype(BF16), ones_bd)
        od = od + w.reshape(nb, GLA_DIAG, W) * v3[:, s:s + 1, :]
    return o + od.reshape(C, W)


def _gla_body(*refs, n, has_init, has_final):
    it = iter(refs)
    q_ref, k_ref, v_ref, g_ref, sg_ref, gg_ref, trif_ref, trib_ref, ones_ref = (next(it) for _ in range(9))
    s0_ref = next(it) if has_init else None
    o_ref = next(it)
    sf_ref = next(it) if has_final else None
    of_ref, ob_ref = next(it), next(it)
    st = [[next(it) for _ in range(GLA_W // GLA_QUAD)] for _ in range(2)]

    C = GLA_CHUNK
    nc = n // C
    nquad = GLA_W // GLA_QUAD
    for d in range(2):
        for qd in range(nquad):
            st[d][qd][...] = s0_ref[d, qd] if has_init else jnp.zeros((GLA_QUAD, GLA_QUAD), F32)

    trif = trif_ref[...]
    trib = trib_ref[...]
    ones_bd = ones_ref[...]

    def step(i, carry):
        for d, (tri, acc) in enumerate(((trif, of_ref), (trib, ob_ref))):
            c = i if d == 0 else nc - 1 - i
            r0 = pl.multiple_of(c * C, C)
            rs = pl.ds(r0, C)
            gd = g_ref[rs, d * GLA_W:(d + 1) * GLA_W]
            acc[rs, :] = _gla_chunk(q_ref[rs, :], k_ref[rs, :], v_ref[rs, :], gd, st[d], d == 0, tri, ones_bd)
        return carry

    lax.fori_loop(0, nc, step, 0)

    gg = gg_ref[...]
    fr = min(n, 256)

    def fin(i, carry):
        rs = pl.ds(pl.multiple_of(i * fr, fr), fr)
        o = of_ref[rs, :] + ob_ref[rs, :]
        a, b, c = _split3(o * o)
        ms = (_dot(a, ones_bd) + _dot(b, ones_bd) + _dot(c, ones_bd)) * (1.0 / GLA_DV)
        o_ref[rs, :] = (o * lax.rsqrt(ms + EPS) * gg * sg_ref[rs, :]).astype(BF16)
        return carry

    lax.fori_loop(0, n // fr, fin, 0)

    if has_final:
        for d in range(2):
            for qd in range(nquad):
                s = st[d][qd][...]
                for h in range(GLA_QUAD // GLA_DK):
                    sf_ref[d, qd * (GLA_QUAD // GLA_DK) + h] = s[h * GLA_DK:(h + 1) * GLA_DK,
                                                                 h * GLA_DK:(h + 1) * GLA_DK]


def _gla(gq, gk, gv, g, sg, g_gla, consts, row0, nseq, n, s0=None, want_final=False):
    trif, trib, ones_bd = consts
    b0 = row0 // n
    W = GLA_W
    nquad = W // GLA_QUAD
    seq = lambda w: pl.BlockSpec((n, w), lambda b: (b0 + b, 0))
    full = lambda a: pl.BlockSpec(a.shape, lambda b: (0,) * a.ndim)
    in_specs = [seq(W), seq(W), seq(W), seq(2 * W), seq(W), full(g_gla), full(trif), full(trib), full(ones_bd)]
    args = [gq, gk, gv, g, sg, g_gla, trif, trib, ones_bd]
    if s0 is not None:
        in_specs.append(pl.BlockSpec((None, 2, nquad, GLA_QUAD, GLA_QUAD), lambda b: (b, 0, 0, 0, 0)))
        args.append(s0)
    out_specs = [pl.BlockSpec((n, W), lambda b: (b, 0))]
    out_shape = [jax.ShapeDtypeStruct((nseq * n, W), BF16)]
    if want_final:
        out_specs.append(pl.BlockSpec((None, 2, GLA_HEADS, GLA_DV, GLA_DK), lambda b: (b, 0, 0, 0, 0)))
        out_shape.append(jax.ShapeDtypeStruct((nseq, 2, GLA_HEADS, GLA_DV, GLA_DK), F32))
    scratch = [pltpu.VMEM((n, W), F32), pltpu.VMEM((n, W), F32)]
    scratch += [pltpu.VMEM((GLA_QUAD, GLA_QUAD), F32) for _ in range(2 * nquad)]
    res = pl.pallas_call(
        functools.partial(_gla_body, n=n, has_init=s0 is not None, has_final=want_final),
        grid=(nseq,),
        in_specs=in_specs,
        out_specs=out_specs,
        out_shape=out_shape,
        scratch_shapes=scratch,
        compiler_params=_cparams(("arbitrary",)),
        name="gla_lat" if s0 is not None else "gla_ctx",
    )(*args)
    return res


def _merge_body(x_ref, mod_ref, oc_ref, ol_ref, og_ref, sa_ref, sb_ref, wom_ref, wog_ref, wout_ref,
                g2_ref, wr_ref, x1_ref, h2_ref, aff_ref, *, D, n_ctx_tiles):
    i = pl.program_id(0)
    mod = mod_ref[0]
    omla = jnp.where(i < n_ctx_tiles, oc_ref[...], ol_ref[...])
    om = _dot(omla, wom_ref[...])
    og = _dot(og_ref[...], wog_ref[...])
    merged = (sa_ref[...] * om + sb_ref[...] * og).astype(BF16)
    mix = _dot(merged, wout_ref[...])
    x1 = x_ref[...] + mod[:, 2 * D:3 * D] * mix
    x1_ref[...] = x1
    h2 = _rms(x1, g2_ref[...]) * (1.0 + mod[:, 4 * D:5 * D]) + mod[:, 3 * D:4 * D]
    h2_ref[...] = h2.astype(BF16)
    a, b, c = _split3(h2)
    wa, wb, wc = _split3(wr_ref[...])
    logits = (_dot(a, wa) + _dot(a, wb) + _dot(b, wa)) + (_dot(a, wc) + _dot(b, wb) + _dot(c, wa))
    lane = lax.broadcasted_iota(I32, logits.shape, 1)
    logits = jnp.where(lane < N_EXPERTS, logits, -jnp.inf)
    p = jnp.exp(logits - logits.max(axis=-1, keepdims=True))
    aff_ref[...] = p / p.sum(axis=-1, keepdims=True)


def _merge(x, mod_l, o_ctx, o_lat, o_gla, sa, sb, wom, wog, wout, g2, wr, cond_of_tile, n_ctx_tiles):
    T2, D = x.shape
    nt = T2 // ROW_TILE
    R = mod_l.shape[0]
    n_lat_tiles = nt - n_ctx_tiles
    row = lambda w: pl.BlockSpec((ROW_TILE, w), lambda i: (i, 0))
    full = lambda a: pl.BlockSpec(a.shape, lambda i: (0,) * a.ndim)
    W = o_ctx.shape[1]
    return pl.pallas_call(
        functools.partial(_merge_body, D=D, n_ctx_tiles=n_ctx_tiles),
        grid=(nt,),
        in_specs=[row(D),
                  pl.BlockSpec((1, 1, 6 * D), lambda i: (cond_of_tile(i), 0, 0)),
                  pl.BlockSpec((ROW_TILE, W), lambda i: (jnp.minimum(i, n_ctx_tiles - 1), 0)),
                  pl.BlockSpec((ROW_TILE, W), lambda i: (jnp.clip(i - n_ctx_tiles, 0, n_lat_tiles - 1), 0)),
                  row(GLA_W), row(D), row(D), full(wom), full(wog), full(wout), full(g2), full(wr)],
        out_specs=[row(D), row(D), row(LANES)],
        out_shape=[jax.ShapeDtypeStruct((T2, D), F32), jax.ShapeDtypeStruct((T2, D), BF16),
                   jax.ShapeDtypeStruct((T2, LANES), F32)],
        compiler_params=_cparams(("arbitrary",)),
        name="merge_router",
    )(x, mod_l.reshape(R, 1, 6 * D), o_ctx, o_lat, o_gla, sa, sb, wom, wog, wout, g2, wr)


def _topk_body(a_ref, slot_ref, cb_ref, *, T, cap):
    a = a_ref[0]
    keys = pltpu.bitcast(a, I32)
    E = a.shape[0]
    thr = jnp.zeros((E, 1), I32)
    for bit in range(30, -1, -1):
        cand = thr | (1 << bit)
        cnt = jnp.sum((keys >= cand).astype(F32), axis=1, keepdims=True)
        thr = jnp.where(cnt >= cap, cand, thr)
    gt = keys > thr
    eq = keys == thr
    need = cap - jnp.sum(gt.astype(F32), axis=1, keepdims=True)

    r = lax.broadcasted_iota(I32, (LANES, LANES), 0)
    c = lax.broadcasted_iota(I32, (LANES, LANES), 1)
    triu = (r < c).astype(BF16)
    lane = lax.broadcasted_iota(I32, (E, LANES), 1)

    nb = T // LANES
    per_tile = ROW_TILE // LANES
    carry_eq = jnp.zeros((E, 1), F32)
    carry_sel = jnp.zeros((E, 1), F32)
    cb = jnp.zeros((E, LANES), I32)
    for j in range(nb):
        sl = slice(j * LANES, (j + 1) * LANES)
        eq_j = eq[:, sl].astype(BF16)
        pre = _dot(eq_j, triu) + carry_eq
        carry_eq = carry_eq + jnp.sum(eq_j.astype(F32), axis=1, keepdims=True)
        sel = jnp.logical_or(gt[:, sl], jnp.logical_and(eq[:, sl], pre < need))
        sel_b = sel.astype(BF16)
        if j % per_tile == 0:
            cb = jnp.where(lane == j // per_tile, carry_sel.astype(I32), cb)
        slot = (_dot(sel_b, triu) + carry_sel).astype(I32)
        carry_sel = carry_sel + jnp.sum(sel_b.astype(F32), axis=1, keepdims=True)
        slot_ref[0, :, sl] = jnp.where(sel, slot, -1)
    cb_ref[0] = jnp.where(lane == nb // per_tile, carry_sel.astype(I32), cb)


def _topk(aff_t, cap):
    G, E, T = aff_t.shape
    return pl.pallas_call(
        functools.partial(_topk_body, T=T, cap=cap),
        grid=(G,),
        in_specs=[pl.BlockSpec((1, E, T), lambda g: (g, 0, 0))],
        out_specs=[pl.BlockSpec((1, E, T), lambda g: (g, 0, 0)),
                   pl.BlockSpec((1, E, LANES), lambda g: (g, 0, 0))],
        out_shape=[jax.ShapeDtypeStruct((G, E, T), I32), jax.ShapeDtypeStruct((G, E, LANES), I32)],
        compiler_params=_cparams(("arbitrary",)),
        name="expert_topk",
    )(aff_t)


def _expert_body(cb_ref, slot_ref, h_ref, wg_ref, wu_ref, wd_ref, y_ref, xs_ref, acc_ref, *,
                 cap, win, nt, nfh, E):
    g = pl.program_id(0)
    e = pl.program_id(1)
    fh = pl.program_id(2)
    base = (g * E + e) * LANES

    @pl.when(fh == 0)
    def _gather():
        xs_ref[...] = jnp.zeros_like(xs_ref)

        def tile(i, carry):
            lo = cb_ref[base + i]
            hi = cb_ref[base + i + 1]
            srow = slot_ref[i]
            for w in range(cap // win):
                @pl.when(jnp.logical_and(lo < (w + 1) * win, hi > w * win))
                def _():
                    j = lax.broadcasted_iota(I32, (win, 1), 0) + w * win
                    oh = (srow == j).astype(BF16)
                    hs = h_ref[pl.ds(pl.multiple_of(i * ROW_TILE, ROW_TILE), ROW_TILE), :]
                    xs_ref[w * win:(w + 1) * win, :] += _dot(oh, hs)
            return carry

        lax.fori_loop(0, nt, tile, 0)

    xb = xs_ref[...].astype(BF16)
    gate = _dot(xb, wg_ref[...].astype(BF16))
    up = _dot(xb, wu_ref[...].astype(BF16))
    hid = (gate * _sigmoid(gate) * up).astype(BF16)
    part = _dot(hid, wd_ref[...].astype(BF16))

    @pl.when(fh == 0)
    def _():
        acc_ref[...] = part

    @pl.when(fh > 0)
    def _():
        acc_ref[...] += part

    @pl.when(fh == nfh - 1)
    def _():
        y_ref[...] = acc_ref[...].astype(BF16)


def _experts(cb_flat, slot5, h2, w_gate, w_up, w_down, layer, cap, win):
    G, E, nt = slot5.shape[:3]
    T = nt * ROW_TILE
    D = h2.shape[1]
    FF = w_gate.shape[-1]
    nfh = 2
    fb = FF // nfh
    grid_spec = pltpu.PrefetchScalarGridSpec(
        num_scalar_prefetch=1,
        grid=(G, E, nfh),
        in_specs=[pl.BlockSpec((None, None, nt, 1, ROW_TILE), lambda g, e, f, cb: (g, e, 0, 0, 0)),
                  pl.BlockSpec((T, D), lambda g, e, f, cb: (g, 0)),
                  pl.BlockSpec((None, None, D, fb), lambda g, e, f, cb: (layer, e, 0, f)),
                  pl.BlockSpec((None, None, D, fb), lambda g, e, f, cb: (layer, e, 0, f)),
                  pl.BlockSpec((None, None, fb, D), lambda g, e, f, cb: (layer, e, f, 0))],
        out_specs=pl.BlockSpec((None, None, cap, D), lambda g, e, f, cb: (g, e, 0, 0)),
        scratch_shapes=[pltpu.VMEM((cap, D), F32), pltpu.VMEM((cap, D), F32)],
    )
    return pl.pallas_call(
        functools.partial(_expert_body, cap=cap, win=win, nt=nt, nfh=nfh, E=E),
        grid_spec=grid_spec,
        out_shape=jax.ShapeDtypeStruct((G, E, cap, D), BF16),
        compiler_params=_cparams(("arbitrary", "arbitrary", "arbitrary")),
        name="expert_ffn",
    )(cb_flat, slot5, h2, w_gate, w_up, w_down)


def _combine_body(cb_ref, x_ref, mod_ref, aff_ref, slot_ref, y_ref, gf_ref, o_ref, acc_ref, *,
                  D, cap, win, E, tiles_per_group, final):
    i = pl.program_id(0)
    g = i // tiles_per_group
    ti = i % tiles_per_group
    acc_ref[...] = jnp.zeros_like(acc_ref)
    aff = aff_ref[...]
    slots = slot_ref[...]
    for e in range(E):
        base = (g * E + e) * LANES
        lo = cb_ref[base + ti]
        hi = cb_ref[base + ti + 1]
        col = slots[:, e:e + 1]
        wcol = aff[:, e:e + 1]
        for w in range(cap // win):
            @pl.when(jnp.logical_and(lo < (w + 1) * win, hi > w * win))
            def _():
                j = lax.broadcasted_iota(I32, (1, win), 1) + w * win
                oh = (col == j).astype(BF16)
                acc_ref[...] += wcol * _dot(oh, y_ref[e, w * win:(w + 1) * win, :])
    mod = mod_ref[0]
    x2 = x_ref[...] + mod[:, 5 * D:6 * D] * acc_ref[...]
    if final:
        x2 = _rms(x2, gf_ref[...])
    o_ref[...] = x2


def _combine(cb_flat, x1, mod_l, aff, slot_t, y, g_final, cond_of_tile, cap, win, final):
    T2, D = x1.shape
    G, E = y.shape[:2]
    nt = T2 // ROW_TILE
    tpg = nt // G
    R = mod_l.shape[0]
    grid_spec = pltpu.PrefetchScalarGridSpec(
        num_scalar_prefetch=1,
        grid=(nt,),
        in_specs=[pl.BlockSpec((ROW_TILE, D), lambda i, cb: (i, 0)),
                  pl.BlockSpec((1, 1, 6 * D), lambda i, cb: (cond_of_tile(i), 0, 0)),
                  pl.BlockSpec((ROW_TILE, LANES), lambda i, cb: (i, 0)),
                  pl.BlockSpec((ROW_TILE, LANES), lambda i, cb: (i, 0)),
                  pl.BlockSpec((None, E, cap, D), lambda i, cb: (i // tpg, 0, 0, 0)),
                  pl.BlockSpec((1, D), lambda i, cb: (0, 0))],
        out_specs=pl.BlockSpec((ROW_TILE, D), lambda i, cb: (i, 0)),
        scratch_shapes=[pltpu.VMEM((ROW_TILE, D), F32)],
    )
    return pl.pallas_call(
        functools.partial(_combine_body, D=D, cap=cap, win=win, E=E, tiles_per_group=tpg, final=final),
        grid_spec=grid_spec,
        out_shape=jax.ShapeDtypeStruct((T2, D), F32),
        compiler_params=_cparams(("arbitrary",)),
        name="moe_combine",
    )(cb_flat, x1, mod_l.reshape(R, 1, 6 * D), aff, slot_t, y, g_final)


def _pack_weights(w_in, w_uq, w_uk, w_uv, w_gla_gate, b_gla_gate):
    L, D, _ = w_in.shape
    sizes = (Q_RANK, KV_RANK, MLA_ROPE, GLA_W, GLA_W, GLA_W, 2 * GLA_GATE_RANK, GLA_W, D, D)
    idx = np.cumsum(sizes)[:-1]
    pq, pkv, kr, gq, gk, gv, glr, gog, ga, gb = jnp.split(w_in, [int(i) for i in idx], axis=-1)
    npair = MLA_ROPE // 4
    swap = np.concatenate([np.arange(npair, 2 * npair), np.arange(0, npair),
                           np.arange(3 * npair, 4 * npair), np.arange(2 * npair, 3 * npair)])

    def slot_rope(w):
        return jnp.pad(w, ((0, 0), (0, 0), (MLA_NOPE, HEAD_PAD - MLA_NOPE - MLA_ROPE)))

    glr_p = jnp.pad(glr, ((0, 0), (0, 0), (0, LANES - 2 * GLA_GATE_RANK)))
    wp = jnp.concatenate([pq, pkv, slot_rope(kr), slot_rope(kr[..., swap]), gq, gk, gv, glr_p, gog, ga, gb],
                         axis=-1).astype(BF16)

    uq = w_uq.reshape(L, Q_RANK, MLA_HEADS, MLA_NOPE + MLA_ROPE)
    pad_h = HEAD_PAD - MLA_NOPE - MLA_ROPE
    uq_n = jnp.pad(uq, ((0, 0), (0, 0), (0, 0), (0, pad_h))).reshape(L, Q_RANK, MLA_HEADS * HEAD_PAD)
    uq_s = jnp.concatenate([jnp.zeros_like(uq[..., :MLA_NOPE]), uq[..., MLA_NOPE:][..., swap]], axis=-1)
    uq_s = jnp.pad(uq_s, ((0, 0), (0, 0), (0, 0), (0, pad_h))).reshape(L, Q_RANK, MLA_HEADS * HEAD_PAD)
    wuq = jnp.concatenate([uq_n, uq_s], axis=-1).astype(BF16)

    uk = w_uk.reshape(L, KV_RANK, MLA_HEADS, MLA_NOPE)
    wuk = jnp.pad(uk, ((0, 0), (0, 0), (0, 0), (0, HEAD_PAD - MLA_NOPE))).reshape(
        L, KV_RANK, MLA_HEADS * HEAD_PAD).astype(BF16)
    wuv = w_uv.astype(BF16)

    wg = jnp.zeros((L, LANES, 2 * GLA_W), F32)
    wg = wg.at[:, 0:GLA_GATE_RANK, 0:GLA_W].set(w_gla_gate[:, 0])
    wg = wg.at[:, GLA_GATE_RANK:2 * GLA_GATE_RANK, GLA_W:].set(w_gla_gate[:, 1])
    bg = b_gla_gate.reshape(L, 1, 2 * GLA_W)
    return wp, wuq, wuk, wuv, wg.astype(BF16), bg


def _rope_tables(n_lat):
    npair = MLA_ROPE // 4
    freqs = ROPE_BASE ** (-jnp.arange(npair, dtype=F32) / npair)
    pos = jnp.arange(n_lat)
    ang_r = (pos // GRID_W).astype(F32)[:, None] * freqs
    ang_c = (pos % GRID_W).astype(F32)[:, None] * freqs
    cr, sr, cc, sc = jnp.cos(ang_r), jnp.sin(ang_r), jnp.cos(ang_c), jnp.sin(ang_c)
    cos32 = jnp.concatenate([cr, cr, cc, cc], axis=-1)
    sin32 = jnp.concatenate([-sr, sr, -sc, sc], axis=-1)
    pad_h = HEAD_PAD - MLA_NOPE - MLA_ROPE
    ones = jnp.ones((n_lat, MLA_NOPE), F32)
    cos_l = jnp.concatenate([ones, cos32, jnp.zeros((n_lat, pad_h), F32)], axis=-1)
    sin_l = jnp.pad(sin32, ((0, 0), (MLA_NOPE, pad_h)))
    cos_i = jnp.concatenate([jnp.ones((ROW_TILE, MLA_NOPE + MLA_ROPE), F32), jnp.zeros((ROW_TILE, pad_h), F32)], -1)
    sin_i = jnp.zeros((ROW_TILE, HEAD_PAD), F32)
    return jnp.concatenate([cos_i, cos_l], 0), jnp.concatenate([sin_i, sin_l], 0)


def _gla_consts():
    C = GLA_CHUNK
    r = np.arange(C)
    trif = (r[None, :] <= r[:, None]).astype(np.float32)
    trib = (r[None, :] >= r[:, None]).astype(np.float32)
    h = np.arange(GLA_W) // GLA_DK
    ones_bd = (h[:, None] == h[None, :]).astype(np.float32)
    return jnp.asarray(trif, BF16), jnp.asarray(trib, BF16), jnp.asarray(ones_bd, BF16)


def kernel(x_prompt, x_sample, cache_ckv, cache_krope, state_gla, c, c_ctx, w_mod, b_mod, g_norm1, g_norm2, w_in, g_q, g_kv, w_uq, w_uk, w_uv, w_o_mla, w_gla_gate, b_gla_gate, g_gla, w_o_gla, w_out, w_router, w_e_gate, w_e_up, w_e_down, g_final):
    B, N, D = x_prompt.shape
    DB, DN, _ = x_sample.shape
    L = w_in.shape[0]
    Tc, Tl = B * N, DB * DN
    assert Tc == Tl and Tc % DN == 0 and N % ROW_TILE == 0 and DN % ROW_TILE == 0
    assert N % GLA_CHUNK == 0 and DN % GLA_CHUNK == 0 and DN % GRID_W == 0
    T = Tc
    G = 2
    n_ctx_tiles = Tc // ROW_TILE
    lat_tiles_per_seq = DN // ROW_TILE
    cap = max(1, CAPACITY_FACTOR * T // N_EXPERTS)
    win = LANES if cap % LANES == 0 else cap
    assert cap % win == 0 and win % SUBLANES == 0 and 1 + DB <= SUBLANES

    def cond_of_tile(i):
        return jnp.where(i < n_ctx_tiles, 0, 1 + (i - n_ctx_tiles) // lat_tiles_per_seq)

    def tab_of_tile(i):
        return jnp.where(i < n_ctx_tiles, 0, 1 + (i - n_ctx_tiles) % lat_tiles_per_seq)

    cvec = jnp.concatenate([c_ctx[None, :], c, jnp.zeros((SUBLANES - 1 - DB, D), F32)], axis=0)
    mod = _adaln_all(cvec, w_mod, b_mod)

    wp, wuq, wuk, wuv, wg, bg = _pack_weights(w_in, w_uq, w_uk, w_uv, w_gla_gate, b_gla_gate)
    cos_t, sin_t = _rope_tables(DN)
    gla_consts = _gla_consts()
    wom = w_o_mla.astype(BF16)
    wog = w_o_gla.astype(BF16)
    wout = w_out.astype(BF16)
    wr = jnp.pad(w_router, ((0, 0), (0, 0), (0, LANES - N_EXPERTS)))

    ckr_pad = jnp.pad(cache_krope, ((0, 0), (0, 0), (0, 0), (MLA_NOPE, HEAD_PAD - MLA_NOPE - MLA_ROPE)))
    kc_all, vc_all = _cache_kv(cache_ckv, ckr_pad, wuk, wuv)

    nquad = GLA_W // GLA_QUAD
    hq = GLA_QUAD // GLA_DK
    st_t = jnp.swapaxes(state_gla, -1, -2).reshape(DB, L, 2, nquad, hq, GLA_DV, GLA_DK)
    eye = jnp.eye(hq, dtype=F32)
    s0_all = jnp.einsum("blzqhvk,hj->blzqhvjk", st_t, eye).reshape(DB, L, 2, nquad, GLA_QUAD, GLA_QUAD)

    x = jnp.concatenate([x_prompt.reshape(Tc, D), x_sample.reshape(Tl, D)], axis=0)
    ckv_list, kr_list, gla_list = [], [], []
    for l in range(L):
        pre = _inproj(x, mod[l], g_norm1[l][None], wp[l], g_q[l][None], g_kv[l][None], wuq[l], wuk[l], wuv[l],
                      wg[l], bg[l], cos_t, sin_t, cond_of_tile, tab_of_tile)
        ckv_list.append(pre["ckv"][:Tc].reshape(B, N, KV_RANK))
        kr_list.append(pre["kr"][:Tc, MLA_NOPE:MLA_NOPE + MLA_ROPE].reshape(B, N, MLA_ROPE))

        o_ctx = _attention_ctx(pre["q"], pre["k"], pre["v"], B, N)
        o_lat = _attention_lat(pre["q"], pre["k"], pre["v"], kc_all[l], vc_all[l], Tc, DB, DN, ROW_TILE)

        gg = g_gla[l][None]
        og_ctx, s_fin = _gla(pre["gq"], pre["gk"], pre["gv"], pre["g"], pre["sg"], gg, gla_consts, 0, B, N,
                             want_final=True)
        (og_lat,) = _gla(pre["gq"], pre["gk"], pre["gv"], pre["g"], pre["sg"], gg, gla_consts, Tc, DB, DN,
                         s0=s0_all[:, l])
        gla_list.append(jnp.swapaxes(s_fin, -1, -2))
        o_gla = jnp.concatenate([og_ctx, og_lat], axis=0)

        x1, h2, aff = _merge(x, mod[l], o_ctx, o_lat, o_gla, pre["sa"], pre["sb"], wom[l], wog[l], wout[l],
                             g_norm2[l][None], wr[l], cond_of_tile, n_ctx_tiles)

        aff_t = jnp.swapaxes(aff[:, :N_EXPERTS].reshape(G, T, N_EXPERTS), 1, 2)
        slot, cb = _topk(aff_t, cap)
        cb_flat = cb.reshape(-1)
        slot5 = slot.reshape(G, N_EXPERTS, T // ROW_TILE, 1, ROW_TILE)
        slot_t = jnp.pad(jnp.swapaxes(slot, 1, 2).reshape(G * T, N_EXPERTS),
                         ((0, 0), (0, LANES - N_EXPERTS)), constant_values=-1)
        y = _experts(cb_flat, slot5, h2, w_e_gate, w_e_up, w_e_down, l, cap, win)
        x = _combine(cb_flat, x1, mod[l], aff, slot_t, y, g_final[None], cond_of_tile, cap, win, final=(l == L - 1))

    y_prompt = x[:Tc].reshape(B, N, D)
    y_sample = x[Tc:].reshape(DB, DN, D)
    new_ckv = jnp.stack(ckv_list, axis=1)
    new_krope = jnp.stack(kr_list, axis=1)
    new_gla = jnp.stack(gla_list, axis=1)
    return (y_prompt, y_sample, new_ckv, new_krope, new_gla)
```

```python
import functools

import jax
import jax.numpy as jnp
import numpy as np
from jax import lax
from jax.experimental import pallas as pl
from jax.experimental.pallas import tpu as pltpu

F32 = jnp.float32
BF16 = jnp.bfloat16
I32 = jnp.int32

GRID_W = 64
EPS = 1e-6
MLA_HEADS = 8
MLA_NOPE = 64
MLA_ROPE = 32
MLA_V = 64
Q_RANK = 384
KV_RANK = 256
ROPE_BASE = 10000.0
GLA_HEADS = 8
GLA_DK = 64
GLA_DV = 64
GLA_W = GLA_HEADS * GLA_DK
GLA_GATE_RANK = 16
GLA_TAU = 16.0
GLA_CHUNK = 64
N_EXPERTS = 16
CAPACITY_FACTOR = 2

LANES = 128
SUBLANES = 8
BF16_ROWS = 16
HEAD_PAD = 128
VMEM_LIMIT = 56 * 1024 * 1024

ROW_TILE = 256
GLA_QUAD = 4 * GLA_DK
GLA_DIAG = 8
GLA_DK_SHIFT = GLA_DK.bit_length() - 1
assert 1 << GLA_DK_SHIFT == GLA_DK
Q_PRESCALE = (MLA_NOPE + MLA_ROPE) ** -0.5 * float(np.log2(np.e))

_SEG = {}
_off = 0
for _name, _width in (("pq", Q_RANK), ("pkv", KV_RANK), ("kr", HEAD_PAD), ("krs", HEAD_PAD),
                      ("gq", GLA_W), ("gk", GLA_W), ("gv", GLA_W), ("glr", LANES),
                      ("gog", GLA_W), ("ga", None), ("gb", None)):
    _SEG[_name] = (_off, _width)
    if _width is not None:
        _off += _width
_PACK_FIXED = _off


def _cparams(sem, vmem=VMEM_LIMIT):
    return pltpu.CompilerParams(dimension_semantics=sem, vmem_limit_bytes=vmem)


def _dot(a, b):
    return jnp.dot(a, b, preferred_element_type=F32)


def _dot_nt(a, b):
    return lax.dot_general(a, b, (((1,), (1,)), ((), ())), preferred_element_type=F32)


def _dot_tn(a, b):
    return lax.dot_general(a, b, (((0,), (0,)), ((), ())), preferred_element_type=F32)


def _rms(x, g):
    return x * lax.rsqrt(jnp.mean(x * x, axis=-1, keepdims=True) + EPS) * g


def _sigmoid(x):
    return 1.0 / (1.0 + jnp.exp(-x))


def _split3(x):
    a = x.astype(BF16)
    r = x - a.astype(F32)
    b = r.astype(BF16)
    c = (r - b.astype(F32)).astype(BF16)
    return a, b, c


def _mod_body(c_ref, w_ref, b_ref, o_ref):
    c = c_ref[...]
    s = (c * _sigmoid(c)).astype(BF16)
    o_ref[0] = _dot(s, w_ref[0].astype(BF16)) + b_ref[0]


def _adaln_all(cvec, w_mod, b_mod):
    L, D, D6 = w_mod.shape
    R = cvec.shape[0]
    tn = 1536
    return pl.pallas_call(
        _mod_body,
        grid=(L, D6 // tn),
        in_specs=[pl.BlockSpec((R, D), lambda l, j: (0, 0)),
                  pl.BlockSpec((1, D, tn), lambda l, j: (l, 0, j)),
                  pl.BlockSpec((1, 1, tn), lambda l, j: (l, 0, j))],
        out_specs=pl.BlockSpec((1, R, tn), lambda l, j: (l, 0, j)),
        out_shape=jax.ShapeDtypeStruct((L, R, D6), F32),
        compiler_params=_cparams(("arbitrary", "arbitrary")),
        name="adaln_mod",
    )(cvec, w_mod, b_mod.reshape(L, 1, D6))


def _inproj_body(x_ref, mod_ref, g1_ref, w_ref, gq_ref, gkv_ref, wuq_ref, wuk_ref, wuv_ref,
                 wg_ref, bg_ref, ct_ref, st_ref,
                 q_ref, k_ref, v_ref, ckv_ref, kr_ref, gqo_ref, gko_ref, gvo_ref, g_ref,
                 sg_ref, sa_ref, sb_ref, *, D):
    x = x_ref[...]
    mod = mod_ref[0]
    h = (_rms(x, g1_ref[...]) * (1.0 + mod[:, D:2 * D]) + mod[:, 0:D]).astype(BF16)

    def seg(name, width=None):
        a, w = _SEG[name]
        w = width if w is None else w
        return _dot(h, w_ref[:, a:a + w])

    cos = ct_ref[...]
    sin = st_ref[...]
    nq = MLA_HEADS * HEAD_PAD

    cq = _rms(seg("pq"), gq_ref[...]).astype(BF16)
    qq = _dot(cq, wuq_ref[...])
    for hd in range(MLA_HEADS):
        a = hd * HEAD_PAD
        q_ref[:, a:a + HEAD_PAD] = ((qq[:, a:a + HEAD_PAD] * cos
                                     + qq[:, nq + a:nq + a + HEAD_PAD] * sin) * Q_PRESCALE).astype(BF16)

    ckv = _rms(seg("pkv"), gkv_ref[...])
    ckv_ref[...] = ckv
    ckv_b = ckv.astype(BF16)
    kr = seg("kr")
    kr_ref[...] = kr
    kr_rot = kr * cos + seg("krs") * sin
    kn = _dot(ckv_b, wuk_ref[...])
    for hd in range(MLA_HEADS):
        a = hd * HEAD_PAD
        k_ref[:, a:a + HEAD_PAD] = (kn[:, a:a + HEAD_PAD] + kr_rot).astype(BF16)
    v_ref[...] = _dot(ckv_b, wuv_ref[...]).astype(BF16)

    gqo_ref[...] = seg("gq") * (GLA_DK ** -0.5)
    gko_ref[...] = seg("gk")
    gvo_ref[...] = seg("gv")
    logit = _dot(seg("glr").astype(BF16), wg_ref[...]) + bg_ref[...]
    g_ref[...] = (jnp.minimum(logit, 0.0) - jnp.log1p(jnp.exp(-jnp.abs(logit)))) * (1.0 / GLA_TAU)
    gog = seg("gog")
    sg_ref[...] = gog * _sigmoid(gog)
    a0 = _SEG["ga"][0]
    sa_ref[...] = _sigmoid(_dot(h, w_ref[:, a0:a0 + D]))
    sb_ref[...] = _sigmoid(_dot(h, w_ref[:, a0 + D:a0 + 2 * D]))


def _inproj(x, mod_l, g1, wp, gq, gkv, wuq, wuk, wuv, wg, bg, cos_t, sin_t, cond_of_tile, tab_of_tile):
    T2, D = x.shape
    nt = T2 // ROW_TILE
    P = wp.shape[1]
    R = mod_l.shape[0]
    row = lambda w: pl.BlockSpec((ROW_TILE, w), lambda i: (i, 0))
    full = lambda a: pl.BlockSpec(a.shape, lambda i: (0,) * a.ndim)
    outs = [("q", MLA_HEADS * HEAD_PAD, BF16), ("k", MLA_HEADS * HEAD_PAD, BF16),
            ("v", MLA_HEADS * MLA_V, BF16), ("ckv", KV_RANK, F32), ("kr", HEAD_PAD, F32),
            ("gq", GLA_W, F32), ("gk", GLA_W, F32), ("gv", GLA_W, F32), ("g", 2 * GLA_W, F32),
            ("sg", GLA_W, F32), ("sa", D, F32), ("sb", D, F32)]
    res = pl.pallas_call(
        functools.partial(_inproj_body, D=D),
        grid=(nt,),
        in_specs=[row(D),
                  pl.BlockSpec((1, 1, 6 * D), lambda i: (cond_of_tile(i), 0, 0)),
                  full(g1), full(wp), full(gq), full(gkv), full(wuq), full(wuk), full(wuv),
                  full(wg), full(bg),
                  pl.BlockSpec((ROW_TILE, HEAD_PAD), lambda i: (tab_of_tile(i), 0)),
                  pl.BlockSpec((ROW_TILE, HEAD_PAD), lambda i: (tab_of_tile(i), 0))],
        out_specs=[row(w) for _, w, _ in outs],
        out_shape=[jax.ShapeDtypeStruct((T2, w), dt) for _, w, dt in outs],
        compiler_params=_cparams(("arbitrary",)),
        name="inproj",
    )(x, mod_l.reshape(R, 1, 6 * D), g1, wp, gq, gkv, wuq, wuk, wuv, wg, bg, cos_t, sin_t)
    return dict(zip([n for n, _, _ in outs], res))


def _cache_kv_body(ckv_ref, krp_ref, wuk_ref, wuv_ref, k_ref, v_ref):
    c = ckv_ref[...].astype(BF16)
    kn = _dot(c, wuk_ref[...])
    krp = krp_ref[...]
    for hd in range(MLA_HEADS):
        a = hd * HEAD_PAD
        k_ref[:, a:a + HEAD_PAD] = (kn[:, a:a + HEAD_PAD] + krp).astype(BF16)
    v_ref[...] = _dot(c, wuv_ref[...]).astype(BF16)


def _cache_kv(cache_ckv, cache_kr_pad, wuk, wuv):
    DB, L, P, R = cache_ckv.shape
    nk = MLA_HEADS * HEAD_PAD
    nv = MLA_HEADS * MLA_V
    return pl.pallas_call(
        _cache_kv_body,
        grid=(L, DB),
        in_specs=[pl.BlockSpec((None, None, P, R), lambda l, b: (b, l, 0, 0)),
                  pl.BlockSpec((None, None, P, HEAD_PAD), lambda l, b: (b, l, 0, 0)),
                  pl.BlockSpec((None, R, nk), lambda l, b: (l, 0, 0)),
                  pl.BlockSpec((None, R, nv), lambda l, b: (l, 0, 0))],
        out_specs=[pl.BlockSpec((None, None, P, nk), lambda l, b: (l, b, 0, 0)),
                   pl.BlockSpec((None, None, P, nv), lambda l, b: (l, b, 0, 0))],
        out_shape=[jax.ShapeDtypeStruct((L, DB, P, nk), BF16),
                   jax.ShapeDtypeStruct((L, DB, P, nv), BF16)],
        compiler_params=_cparams(("arbitrary", "arbitrary")),
        name="cache_kv",
    )(cache_ckv, cache_kr_pad, wuk, wuv)


def _attn_body(*refs, nseg):
    q_ref = refs[0]
    k_refs = refs[1:1 + nseg]
    v_refs = refs[1 + nseg:1 + 2 * nseg]
    o_ref = refs[1 + 2 * nseg]
    lane = lax.broadcasted_iota(I32, (1, 2 * MLA_V), 1)
    outs = []
    for hh in range(2):
        qh = q_ref[:, hh * HEAD_PAD:(hh + 1) * HEAD_PAD]
        s = [_dot_nt(qh, kr[:, hh * HEAD_PAD:(hh + 1) * HEAD_PAD]) for kr in k_refs]
        m = s[0].max(axis=-1, keepdims=True)
        for sj in s[1:]:
            m = jnp.maximum(m, sj.max(axis=-1, keepdims=True))
        p = [jnp.exp2(sj - m) for sj in s]
        den = p[0].sum(axis=-1, keepdims=True)
        for pj in p[1:]:
            den = den + pj.sum(axis=-1, keepdims=True)
        o = None
        for pj, vr in zip(p, v_refs):
            t = _dot(pj.astype(BF16), vr[...])
            o = t if o is None else o + t
        outs.append(o * (1.0 / den))
    o_ref[...] = jnp.where(lane < MLA_V, outs[0], outs[1]).astype(BF16)


def _attention_ctx(q, k, v, nseq, n):
    hp = MLA_HEADS // 2
    return pl.pallas_call(
        functools.partial(_attn_body, nseg=1),
        grid=(nseq, hp),
        in_specs=[pl.BlockSpec((n, 2 * HEAD_PAD), lambda b, h: (b, h)),
                  pl.BlockSpec((n, 2 * HEAD_PAD), lambda b, h: (b, h)),
                  pl.BlockSpec((n, 2 * MLA_V), lambda b, h: (b, h))],
        out_specs=pl.BlockSpec((n, 2 * MLA_V), lambda b, h: (b, h)),
        out_shape=jax.ShapeDtypeStruct((nseq * n, MLA_HEADS * MLA_V), BF16),
        compiler_params=_cparams(("arbitrary", "arbitrary")),
        name="attn_ctx",
    )(q, k, v)


def _attention_lat(q, k, v, kc, vc, row0, nseq, n, tq):
    hp = MLA_HEADS // 2
    P = kc.shape[1]
    qt = n // tq
    q0 = row0 // tq
    s0 = row0 // n
    return pl.pallas_call(
        functools.partial(_attn_body, nseg=2),
        grid=(nseq, hp, qt),
        in_specs=[pl.BlockSpec((tq, 2 * HEAD_PAD), lambda b, h, t: (q0 + b * qt + t, h)),
                  pl.BlockSpec((None, P, 2 * HEAD_PAD), lambda b, h, t: (b, 0, h)),
                  pl.BlockSpec((n, 2 * HEAD_PAD), lambda b, h, t: (s0 + b, h)),
                  pl.BlockSpec((None, P, 2 * MLA_V), lambda b, h, t: (b, 0, h)),
                  pl.BlockSpec((n, 2 * MLA_V), lambda b, h, t: (s0 + b, h))],
        out_specs=pl.BlockSpec((tq, 2 * MLA_V), lambda b, h, t: (b * qt + t, h)),
        out_shape=jax.ShapeDtypeStruct((nseq * n, MLA_HEADS * MLA_V), BF16),
        compiler_params=_cparams(("arbitrary", "arbitrary", "arbitrary")),
        name="attn_lat",
    )(q, kc, k, vc, v)


def _gla_chunk(q, k, v, g, st_refs, fwd, tri, ones_bd):
    C = GLA_CHUNK
    W = GLA_W
    nquad = W // GLA_QUAD
    rows = lax.broadcasted_iota(I32, (C, 1), 0)

    g1, g2, g3 = _split3(g)
    cum = _dot(tri, jnp.concatenate([g1, g2, g3], axis=1))
    cum = cum[:, 0:W] + cum[:, W:2 * W] + cum[:, 2 * W:3 * W]
    edge = C - 1 if fwd else 0
    last = cum[edge:edge + 1]
    q_in = (q * jnp.exp(cum)).astype(BF16)
    k_st = (k * jnp.exp(last - cum)).astype(BF16)
    v_b = v.astype(BF16)

    lane_q = lax.broadcasted_iota(I32, (1, GLA_QUAD), 1)
    head_masks = [jnp.right_shift(lane_q, GLA_DK_SHIFT) == h for h in range(GLA_QUAD // GLA_DK)]
    col_s = jnp.bitwise_and(lane_q, C - 1)

    def stack_heads(xq):
        return jnp.concatenate([jnp.where(mh, xq, jnp.zeros_like(xq)) for mh in head_masks], axis=0)

    a_acc = [jnp.zeros((C, GLA_QUAD), F32) for _ in range(nquad)]
    half = C // 2
    while half >= GLA_DIAG:
        blk = 2 * half
        pieces = []
        for p in range(C // blk):
            rr = p * blk + (half - 1 if fwd else half)
            pieces.append(jnp.broadcast_to(cum[rr:rr + 1], (blk, W)))
        ref = jnp.concatenate(pieces, axis=0) if len(pieces) > 1 else pieces[0]
        upper = jnp.bitwise_and(rows, blk - 1) >= half
        qmask = upper if fwd else jnp.logical_not(upper)
        qe = jnp.where(qmask, q * jnp.exp(jnp.minimum(cum - ref, 0.0)), 0.0).astype(BF16)
        ke = jnp.where(qmask, 0.0, k * jnp.exp(jnp.minimum(ref - cum, 0.0))).astype(BF16)
        sh = blk.bit_length() - 1
        same = jnp.right_shift(rows, sh) == jnp.right_shift(col_s, sh)
        for qd in range(nquad):
            sl = slice(qd * GLA_QUAD, (qd + 1) * GLA_QUAD)
            a = _dot_nt(qe[:, sl], stack_heads(ke[:, sl]))
            a_acc[qd] = a_acc[qd] + jnp.where(same, a, 0.0)
        half //= 2

    o_parts = []
    for qd in range(nquad):
        sl = slice(qd * GLA_QUAD, (qd + 1) * GLA_QUAD)
        st = st_refs[qd][...]
        o = _dot_nt(q_in[:, sl], st.astype(BF16))
        o = o + _dot(a_acc[qd].astype(BF16), stack_heads(v_b[:, sl]))
        o_parts.append(o)
        upd = _dot_tn(v_b[:, sl], k_st[:, sl])
        r2 = jnp.right_shift(lax.broadcasted_iota(I32, (GLA_QUAD, GLA_QUAD), 0), GLA_DK_SHIFT)
        c2 = jnp.right_shift(lax.broadcasted_iota(I32, (GLA_QUAD, GLA_QUAD), 1), GLA_DK_SHIFT)
        st_refs[qd][...] = st * jnp.exp(last[:, sl]) + jnp.where(r2 == c2, upd, 0.0)
    o = jnp.concatenate(o_parts, axis=1)

    nb = C // GLA_DIAG
    q3 = q.reshape(nb, GLA_DIAG, W)
    k3 = k.reshape(nb, GLA_DIAG, W)
    v3 = v.reshape(nb, GLA_DIAG, W)
    c3 = cum.reshape(nb, GLA_DIAG, W)
    tl = lax.broadcasted_iota(I32, (1, GLA_DIAG, 1), 1)
    zs = []
    for s in range(GLA_DIAG):
        e = jnp.exp(c3 - c3[:, s:s + 1, :])
        valid = (tl >= s) if fwd else (tl <= s)
        zs.append(jnp.where(valid, q3 * e * k3[:, s:s + 1, :], 0.0).reshape(C, W).astype(BF16))
    w_all = _dot(jnp.concatenate(zs, axis=0), ones_bd)
    od = jnp.zeros((nb, GLA_DIAG, W), F32)
    for s in range(GLA_DIAG):
        od = od + w_all[s * C:(s + 1) * C].reshape(nb, GLA_DIAG, W) * v3[:, s:s + 1, :]
    return o + od.reshape(C, W)


def _gla_body(*refs, n, has_init, has_final):
    it = iter(refs)
    q_ref, k_ref, v_ref, g_ref, sg_ref, gg_ref, trif_ref, trib_ref, ones_ref = (next(it) for _ in range(9))
    s0_ref = next(it) if has_init else None
    o_ref = next(it)
    sf_ref = next(it) if has_final else None
    of_ref, ob_ref = next(it), next(it)
    st = [[next(it) for _ in range(GLA_W // GLA_QUAD)] for _ in range(2)]

    C = GLA_CHUNK
    nc = n // C
    nquad = GLA_W // GLA_QUAD
    for d in range(2):
        for qd in range(nquad):
            st[d][qd][...] = s0_ref[d, qd] if has_init else jnp.zeros((GLA_QUAD, GLA_QUAD), F32)

    trif = trif_ref[...]
    trib = trib_ref[...]
    ones_bd = ones_ref[...]

    def step(i, carry):
        for d, (tri, acc) in enumerate(((trif, of_ref), (trib, ob_ref))):
            c = i if d == 0 else nc - 1 - i
            r0 = pl.multiple_of(c * C, C)
            rs = pl.ds(r0, C)
            gd = g_ref[rs, d * GLA_W:(d + 1) * GLA_W]
            acc[rs, :] = _gla_chunk(q_ref[rs, :], k_ref[rs, :], v_ref[rs, :], gd, st[d], d == 0, tri, ones_bd)
        return carry

    lax.fori_loop(0, nc, step, 0)

    gg = gg_ref[...]
    fr = min(n, 256)

    def fin(i, carry):
        rs = pl.ds(pl.multiple_of(i * fr, fr), fr)
        o = of_ref[rs, :] + ob_ref[rs, :]
        ms = _dot(jnp.concatenate(_split3(o * o), axis=0), ones_bd)
        ms = (ms[0:fr] + ms[fr:2 * fr] + ms[2 * fr:3 * fr]) * (1.0 / GLA_DV)
        o_ref[rs, :] = (o * lax.rsqrt(ms + EPS) * gg * sg_ref[rs, :]).astype(BF16)
        return carry

    lax.fori_loop(0, n // fr, fin, 0)

    if has_final:
        for d in range(2):
            for qd in range(nquad):
                s = st[d][qd][...]
                for h in range(GLA_QUAD // GLA_DK):
                    sf_ref[d, qd * (GLA_QUAD // GLA_DK) + h] = s[h * GLA_DK:(h + 1) * GLA_DK,
                                                                 h * GLA_DK:(h + 1) * GLA_DK]


def _gla(gq, gk, gv, g, sg, g_gla, consts, row0, nseq, n, s0=None, want_final=False):
    trif, trib, ones_bd = consts
    b0 = row0 // n
    W = GLA_W
    nquad = W // GLA_QUAD
    seq = lambda w: pl.BlockSpec((n, w), lambda b: (b0 + b, 0))
    full = lambda a: pl.BlockSpec(a.shape, lambda b: (0,) * a.ndim)
    in_specs = [seq(W), seq(W), seq(W), seq(2 * W), seq(W), full(g_gla), full(trif), full(trib), full(ones_bd)]
    args = [gq, gk, gv, g, sg, g_gla, trif, trib, ones_bd]
    if s0 is not None:
        in_specs.append(pl.BlockSpec((None, 2, nquad, GLA_QUAD, GLA_QUAD), lambda b: (b, 0, 0, 0, 0)))
        args.append(s0)
    out_specs = [pl.BlockSpec((n, W), lambda b: (b, 0))]
    out_shape = [jax.ShapeDtypeStruct((nseq * n, W), BF16)]
    if want_final:
        out_specs.append(pl.BlockSpec((None, 2, GLA_HEADS, GLA_DV, GLA_DK), lambda b: (b, 0, 0, 0, 0)))
        out_shape.append(jax.ShapeDtypeStruct((nseq, 2, GLA_HEADS, GLA_DV, GLA_DK), F32))
    scratch = [pltpu.VMEM((n, W), F32), pltpu.VMEM((n, W), F32)]
    scratch += [pltpu.VMEM((GLA_QUAD, GLA_QUAD), F32) for _ in range(2 * nquad)]
    res = pl.pallas_call(
        functools.partial(_gla_body, n=n, has_init=s0 is not None, has_final=want_final),
        grid=(nseq,),
        in_specs=in_specs,
        out_specs=out_specs,
        out_shape=out_shape,
        scratch_shapes=scratch,
        compiler_params=_cparams(("arbitrary",)),
        name="gla_lat" if s0 is not None else "gla_ctx",
    )(*args)
    return res


def _merge_body(x_ref, mod_ref, oc_ref, ol_ref, og_ref, sa_ref, sb_ref, wom_ref, wog_ref, wout_ref,
                g2_ref, wr_ref, x1_ref, h2_ref, aff_ref, *, D, n_ctx_tiles):
    i = pl.program_id(0)
    mod = mod_ref[0]
    omla = jnp.where(i < n_ctx_tiles, oc_ref[...], ol_ref[...])
    om = _dot(omla, wom_ref[...])
    og = _dot(og_ref[...], wog_ref[...])
    merged = (sa_ref[...] * om + sb_ref[...] * og).astype(BF16)
    mix = _dot(merged, wout_ref[...])
    x1 = x_ref[...] + mod[:, 2 * D:3 * D] * mix
    x1_ref[...] = x1
    h2 = _rms(x1, g2_ref[...]) * (1.0 + mod[:, 4 * D:5 * D]) + mod[:, 3 * D:4 * D]
    h2_ref[...] = h2.astype(BF16)
    a, b, c = _split3(h2)
    wa, wb, wc = _split3(wr_ref[...])
    logits = (_dot(a, wa) + _dot(a, wb) + _dot(b, wa)) + (_dot(a, wc) + _dot(b, wb) + _dot(c, wa))
    lane = lax.broadcasted_iota(I32, logits.shape, 1)
    logits = jnp.where(lane < N_EXPERTS, logits, -jnp.inf)
    p = jnp.exp(logits - logits.max(axis=-1, keepdims=True))
    aff_ref[...] = p / p.sum(axis=-1, keepdims=True)


def _merge(x, mod_l, o_ctx, o_lat, o_gla, sa, sb, wom, wog, wout, g2, wr, cond_of_tile, n_ctx_tiles):
    T2, D = x.shape
    nt = T2 // ROW_TILE
    R = mod_l.shape[0]
    n_lat_tiles = nt - n_ctx_tiles
    row = lambda w: pl.BlockSpec((ROW_TILE, w), lambda i: (i, 0))
    full = lambda a: pl.BlockSpec(a.shape, lambda i: (0,) * a.ndim)
    W = o_ctx.shape[1]
    return pl.pallas_call(
        functools.partial(_merge_body, D=D, n_ctx_tiles=n_ctx_tiles),
        grid=(nt,),
        in_specs=[row(D),
                  pl.BlockSpec((1, 1, 6 * D), lambda i: (cond_of_tile(i), 0, 0)),
                  pl.BlockSpec((ROW_TILE, W), lambda i: (jnp.minimum(i, n_ctx_tiles - 1), 0)),
                  pl.BlockSpec((ROW_TILE, W), lambda i: (jnp.clip(i - n_ctx_tiles, 0, n_lat_tiles - 1), 0)),
                  row(GLA_W), row(D), row(D), full(wom), full(wog), full(wout), full(g2), full(wr)],
        out_specs=[row(D), row(D), row(LANES)],
        out_shape=[jax.ShapeDtypeStruct((T2, D), F32), jax.ShapeDtypeStruct((T2, D), BF16),
                   jax.ShapeDtypeStruct((T2, LANES), F32)],
        compiler_params=_cparams(("arbitrary",)),
        name="merge_router",
    )(x, mod_l.reshape(R, 1, 6 * D), o_ctx, o_lat, o_gla, sa, sb, wom, wog, wout, g2, wr)


def _topk_body(a_ref, slot_ref, cb_ref, *, T, cap):
    a = a_ref[0]
    keys = pltpu.bitcast(a, I32)
    E = a.shape[0]
    thr = jnp.zeros((E, 1), I32)
    for bit in range(30, -1, -1):
        cand = thr | (1 << bit)
        cnt = jnp.sum((keys >= cand).astype(F32), axis=1, keepdims=True)
        thr = jnp.where(cnt >= cap, cand, thr)
    gt = keys > thr
    eq = keys == thr
    need = cap - jnp.sum(gt.astype(F32), axis=1, keepdims=True)

    r = lax.broadcasted_iota(I32, (LANES, LANES), 0)
    c = lax.broadcasted_iota(I32, (LANES, LANES), 1)
    triu = (r < c).astype(BF16)
    lane = lax.broadcasted_iota(I32, (E, LANES), 1)

    nb = T // LANES
    per_tile = ROW_TILE // LANES
    carry_eq = jnp.zeros((E, 1), F32)
    carry_sel = jnp.zeros((E, 1), F32)
    cb = jnp.zeros((E, LANES), I32)
    for j in range(nb):
        sl = slice(j * LANES, (j + 1) * LANES)
        eq_j = eq[:, sl].astype(BF16)
        pre = _dot(eq_j, triu) + carry_eq
        carry_eq = carry_eq + jnp.sum(eq_j.astype(F32), axis=1, keepdims=True)
        sel = jnp.logical_or(gt[:, sl], jnp.logical_and(eq[:, sl], pre < need))
        sel_b = sel.astype(BF16)
        if j % per_tile == 0:
            cb = jnp.where(lane == j // per_tile, carry_sel.astype(I32), cb)
        slot = (_dot(sel_b, triu) + carry_sel).astype(I32)
        carry_sel = carry_sel + jnp.sum(sel_b.astype(F32), axis=1, keepdims=True)
        slot_ref[0, :, sl] = jnp.where(sel, slot, -1)
    cb_ref[0] = jnp.where(lane == nb // per_tile, carry_sel.astype(I32), cb)


def _topk(aff_t, cap):
    G, E, T = aff_t.shape
    return pl.pallas_call(
        functools.partial(_topk_body, T=T, cap=cap),
        grid=(G,),
        in_specs=[pl.BlockSpec((1, E, T), lambda g: (g, 0, 0))],
        out_specs=[pl.BlockSpec((1, E, T), lambda g: (g, 0, 0)),
                   pl.BlockSpec((1, E, LANES), lambda g: (g, 0, 0))],
        out_shape=[jax.ShapeDtypeStruct((G, E, T), I32), jax.ShapeDtypeStruct((G, E, LANES), I32)],
        compiler_params=_cparams(("arbitrary",)),
        name="expert_topk",
    )(aff_t)


def _expert_body(cb_ref, slot_ref, aff_ref, h_ref, wg_ref, wu_ref, wd_ref, y_ref, xs_ref, acc_ref, ws_ref, *,
                 cap, win, nt, nfh, E):
    g = pl.program_id(0)
    e = pl.program_id(1)
    fh = pl.program_id(2)
    base = (g * E + e) * LANES

    @pl.when(fh == 0)
    def _gather():
        xs_ref[...] = jnp.zeros_like(xs_ref)
        ws_ref[...] = jnp.zeros_like(ws_ref)

        def tile(i, carry):
            lo = cb_ref[base + i]
            hi = cb_ref[base + i + 1]
            srow = slot_ref[i]
            arow = aff_ref[i]
            for w in range(cap // win):
                @pl.when(jnp.logical_and(lo < (w + 1) * win, hi > w * win))
                def _():
                    j = lax.broadcasted_iota(I32, (win, 1), 0) + w * win
                    hit = srow == j
                    hs = h_ref[pl.ds(pl.multiple_of(i * ROW_TILE, ROW_TILE), ROW_TILE), :]
                    xs_ref[w * win:(w + 1) * win, :] += _dot(hit.astype(BF16), hs)
                    ws_ref[w * win:(w + 1) * win, :] += jnp.sum(jnp.where(hit, arow, 0.0), axis=1, keepdims=True)
            return carry

        lax.fori_loop(0, nt, tile, 0)

    xb = xs_ref[...].astype(BF16)
    gate = _dot(xb, wg_ref[...].astype(BF16))
    up = _dot(xb, wu_ref[...].astype(BF16))
    hid = (gate * _sigmoid(gate) * up).astype(BF16)
    part = _dot(hid, wd_ref[...].astype(BF16))

    @pl.when(fh == 0)
    def _():
        acc_ref[...] = part

    @pl.when(fh > 0)
    def _():
        acc_ref[...] += part

    @pl.when(fh == nfh - 1)
    def _():
        y_ref[...] = (acc_ref[...] * ws_ref[...]).astype(BF16)


def _experts(cb_flat, slot5, aff5, h2, w_gate, w_up, w_down, layer, cap, win):
    G, E, nt = slot5.shape[:3]
    T = nt * ROW_TILE
    D = h2.shape[1]
    FF = w_gate.shape[-1]
    nfh = 2
    fb = FF // nfh
    grid_spec = pltpu.PrefetchScalarGridSpec(
        num_scalar_prefetch=1,
        grid=(G, E, nfh),
        in_specs=[pl.BlockSpec((None, None, nt, 1, ROW_TILE), lambda g, e, f, cb: (g, e, 0, 0, 0)),
                  pl.BlockSpec((None, None, nt, 1, ROW_TILE), lambda g, e, f, cb: (g, e, 0, 0, 0)),
                  pl.BlockSpec((T, D), lambda g, e, f, cb: (g, 0)),
                  pl.BlockSpec((None, None, D, fb), lambda g, e, f, cb: (layer, e, 0, f)),
                  pl.BlockSpec((None, None, D, fb), lambda g, e, f, cb: (layer, e, 0, f)),
                  pl.BlockSpec((None, None, fb, D), lambda g, e, f, cb: (layer, e, f, 0))],
        out_specs=pl.BlockSpec((None, None, cap, D), lambda g, e, f, cb: (g, e, 0, 0)),
        scratch_shapes=[pltpu.VMEM((cap, D), F32), pltpu.VMEM((cap, D), F32), pltpu.VMEM((cap, 1), F32)],
    )
    return pl.pallas_call(
        functools.partial(_expert_body, cap=cap, win=win, nt=nt, nfh=nfh, E=E),
        grid_spec=grid_spec,
        out_shape=jax.ShapeDtypeStruct((G, E, cap, D), BF16),
        compiler_params=_cparams(("arbitrary", "arbitrary", "arbitrary")),
        name="expert_ffn",
    )(cb_flat, slot5, aff5, h2, w_gate, w_up, w_down)


def _combine_body(cb_ref, x_ref, mod_ref, slot_ref, y_ref, spread_ref, gf_ref, o_ref, acc_ref, *,
                  D, cap, win, cw, E, tiles_per_group, final):
    i = pl.program_id(0)
    g = i // tiles_per_group
    ti = i % tiles_per_group
    slots = slot_ref[...]
    los = [cb_ref[(g * E + e) * LANES + ti] for e in range(E)]
    his = [cb_ref[(g * E + e) * LANES + ti + 1] for e in range(E)]
    starts = [jnp.minimum(jnp.bitwise_and(lo, -BF16_ROWS), cap - cw) for lo in los]
    fits = his[0] <= starts[0] + cw
    for e in range(1, E):
        fits = jnp.logical_and(fits, his[e] <= starts[e] + cw)

    @pl.when(fits)
    def _fast():
        sp1 = slots + 1
        digits = jnp.concatenate([jnp.right_shift(sp1, 4), jnp.bitwise_and(sp1, 15)], axis=1)
        spread = _dot(digits.astype(F32).astype(BF16), spread_ref[...])
        lane = lax.broadcasted_iota(I32, (1, cw), 1)
        tgt = jnp.concatenate([lane + (starts[e] + 1) for e in range(E)], axis=1).astype(F32)
        onehot = (spread == tgt).astype(BF16)
        rows = jnp.concatenate([y_ref[e, pl.ds(pl.multiple_of(starts[e], BF16_ROWS), cw), :] for e in range(E)],
                               axis=0)
        acc_ref[...] = _dot(onehot, rows)

    @pl.when(jnp.logical_not(fits))
    def _general():
        acc_ref[...] = jnp.zeros_like(acc_ref)
        for e in range(E):
            col = slots[:, e:e + 1]
            for w in range(cap // win):
                @pl.when(jnp.logical_and(los[e] < (w + 1) * win, his[e] > w * win))
                def _():
                    j = lax.broadcasted_iota(I32, (1, win), 1) + w * win
                    oh = (col == j).astype(BF16)
                    acc_ref[...] += _dot(oh, y_ref[e, w * win:(w + 1) * win, :])

    mod = mod_ref[0]
    x2 = x_ref[...] + mod[:, 5 * D:6 * D] * acc_ref[...]
    if final:
        x2 = _rms(x2, gf_ref[...])
    o_ref[...] = x2


def _combine(cb_flat, x1, mod_l, slot_t, y, g_final, cond_of_tile, cap, win, final):
    T2, D = x1.shape
    G, E = y.shape[:2]
    nt = T2 // ROW_TILE
    tpg = nt // G
    R = mod_l.shape[0]
    cw = min(LANES, cap)
    spread = np.zeros((2 * LANES, E * cw), np.float32)
    for e in range(E):
        spread[e, e * cw:(e + 1) * cw] = 16.0
        spread[LANES + e, e * cw:(e + 1) * cw] = 1.0
    spread = jnp.asarray(spread, BF16)
    grid_spec = pltpu.PrefetchScalarGridSpec(
        num_scalar_prefetch=1,
        grid=(nt,),
        in_specs=[pl.BlockSpec((ROW_TILE, D), lambda i, cb: (i, 0)),
                  pl.BlockSpec((1, 1, 6 * D), lambda i, cb: (cond_of_tile(i), 0, 0)),
                  pl.BlockSpec((ROW_TILE, LANES), lambda i, cb: (i, 0)),
                  pl.BlockSpec((None, E, cap, D), lambda i, cb: (i // tpg, 0, 0, 0)),
                  pl.BlockSpec(spread.shape, lambda i, cb: (0, 0)),
                  pl.BlockSpec((1, D), lambda i, cb: (0, 0))],
        out_specs=pl.BlockSpec((ROW_TILE, D), lambda i, cb: (i, 0)),
        scratch_shapes=[pltpu.VMEM((ROW_TILE, D), F32)],
    )
    return pl.pallas_call(
        functools.partial(_combine_body, D=D, cap=cap, win=win, cw=cw, E=E, tiles_per_group=tpg, final=final),
        grid_spec=grid_spec,
        out_shape=jax.ShapeDtypeStruct((T2, D), F32),
        compiler_params=_cparams(("arbitrary",)),
        name="moe_combine",
    )(cb_flat, x1, mod_l.reshape(R, 1, 6 * D), slot_t, y, spread, g_final)


def _pack_weights(w_in, w_uq, w_uk, w_uv, w_gla_gate, b_gla_gate):
    L, D, _ = w_in.shape
    sizes = (Q_RANK, KV_RANK, MLA_ROPE, GLA_W, GLA_W, GLA_W, 2 * GLA_GATE_RANK, GLA_W, D, D)
    idx = np.cumsum(sizes)[:-1]
    pq, pkv, kr, gq, gk, gv, glr, gog, ga, gb = jnp.split(w_in, [int(i) for i in idx], axis=-1)
    npair = MLA_ROPE // 4
    swap = np.concatenate([np.arange(npair, 2 * npair), np.arange(0, npair),
                           np.arange(3 * npair, 4 * npair), np.arange(2 * npair, 3 * npair)])

    def slot_rope(w):
        return jnp.pad(w, ((0, 0), (0, 0), (MLA_NOPE, HEAD_PAD - MLA_NOPE - MLA_ROPE)))

    glr_p = jnp.pad(glr, ((0, 0), (0, 0), (0, LANES - 2 * GLA_GATE_RANK)))
    wp = jnp.concatenate([pq, pkv, slot_rope(kr), slot_rope(kr[..., swap]), gq, gk, gv, glr_p, gog, ga, gb],
                         axis=-1).astype(BF16)

    uq = w_uq.reshape(L, Q_RANK, MLA_HEADS, MLA_NOPE + MLA_ROPE)
    pad_h = HEAD_PAD - MLA_NOPE - MLA_ROPE
    uq_n = jnp.pad(uq, ((0, 0), (0, 0), (0, 0), (0, pad_h))).reshape(L, Q_RANK, MLA_HEADS * HEAD_PAD)
    uq_s = jnp.concatenate([jnp.zeros_like(uq[..., :MLA_NOPE]), uq[..., MLA_NOPE:][..., swap]], axis=-1)
    uq_s = jnp.pad(uq_s, ((0, 0), (0, 0), (0, 0), (0, pad_h))).reshape(L, Q_RANK, MLA_HEADS * HEAD_PAD)
    wuq = jnp.concatenate([uq_n, uq_s], axis=-1).astype(BF16)

    uk = w_uk.reshape(L, KV_RANK, MLA_HEADS, MLA_NOPE)
    wuk = jnp.pad(uk, ((0, 0), (0, 0), (0, 0), (0, HEAD_PAD - MLA_NOPE))).reshape(
        L, KV_RANK, MLA_HEADS * HEAD_PAD).astype(BF16)
    wuv = w_uv.astype(BF16)

    wg = jnp.zeros((L, LANES, 2 * GLA_W), F32)
    wg = wg.at[:, 0:GLA_GATE_RANK, 0:GLA_W].set(w_gla_gate[:, 0])
    wg = wg.at[:, GLA_GATE_RANK:2 * GLA_GATE_RANK, GLA_W:].set(w_gla_gate[:, 1])
    bg = b_gla_gate.reshape(L, 1, 2 * GLA_W)
    return wp, wuq, wuk, wuv, wg.astype(BF16), bg


def _rope_tables(n_lat):
    npair = MLA_ROPE // 4
    freqs = ROPE_BASE ** (-jnp.arange(npair, dtype=F32) / npair)
    pos = jnp.arange(n_lat)
    ang_r = (pos // GRID_W).astype(F32)[:, None] * freqs
    ang_c = (pos % GRID_W).astype(F32)[:, None] * freqs
    cr, sr, cc, sc = jnp.cos(ang_r), jnp.sin(ang_r), jnp.cos(ang_c), jnp.sin(ang_c)
    cos32 = jnp.concatenate([cr, cr, cc, cc], axis=-1)
    sin32 = jnp.concatenate([-sr, sr, -sc, sc], axis=-1)
    pad_h = HEAD_PAD - MLA_NOPE - MLA_ROPE
    ones = jnp.ones((n_lat, MLA_NOPE), F32)
    cos_l = jnp.concatenate([ones, cos32, jnp.zeros((n_lat, pad_h), F32)], axis=-1)
    sin_l = jnp.pad(sin32, ((0, 0), (MLA_NOPE, pad_h)))
    cos_i = jnp.concatenate([jnp.ones((ROW_TILE, MLA_NOPE + MLA_ROPE), F32), jnp.zeros((ROW_TILE, pad_h), F32)], -1)
    sin_i = jnp.zeros((ROW_TILE, HEAD_PAD), F32)
    return jnp.concatenate([cos_i, cos_l], 0), jnp.concatenate([sin_i, sin_l], 0)


def _gla_consts():
    C = GLA_CHUNK
    r = np.arange(C)
    trif = (r[None, :] <= r[:, None]).astype(np.float32)
    trib = (r[None, :] >= r[:, None]).astype(np.float32)
    h = np.arange(GLA_W) // GLA_DK
    ones_bd = (h[:, None] == h[None, :]).astype(np.float32)
    return jnp.asarray(trif, BF16), jnp.asarray(trib, BF16), jnp.asarray(ones_bd, BF16)


def kernel(x_prompt, x_sample, cache_ckv, cache_krope, state_gla, c, c_ctx, w_mod, b_mod, g_norm1, g_norm2, w_in, g_q, g_kv, w_uq, w_uk, w_uv, w_o_mla, w_gla_gate, b_gla_gate, g_gla, w_o_gla, w_out, w_router, w_e_gate, w_e_up, w_e_down, g_final):
    B, N, D = x_prompt.shape
    DB, DN, _ = x_sample.shape
    L = w_in.shape[0]
    Tc, Tl = B * N, DB * DN
    assert Tc == Tl and Tc % DN == 0 and N % ROW_TILE == 0 and DN % ROW_TILE == 0
    assert N % GLA_CHUNK == 0 and DN % GLA_CHUNK == 0 and DN % GRID_W == 0
    T = Tc
    G = 2
    n_ctx_tiles = Tc // ROW_TILE
    lat_tiles_per_seq = DN // ROW_TILE
    cap = max(1, CAPACITY_FACTOR * T // N_EXPERTS)
    win = LANES if cap % LANES == 0 else cap
    assert cap % win == 0 and win % SUBLANES == 0 and 1 + DB <= SUBLANES

    def cond_of_tile(i):
        return jnp.where(i < n_ctx_tiles, 0, 1 + (i - n_ctx_tiles) // lat_tiles_per_seq)

    def tab_of_tile(i):
        return jnp.where(i < n_ctx_tiles, 0, 1 + (i - n_ctx_tiles) % lat_tiles_per_seq)

    cvec = jnp.concatenate([c_ctx[None, :], c, jnp.zeros((SUBLANES - 1 - DB, D), F32)], axis=0)
    mod = _adaln_all(cvec, w_mod, b_mod)

    wp, wuq, wuk, wuv, wg, bg = _pack_weights(w_in, w_uq, w_uk, w_uv, w_gla_gate, b_gla_gate)
    cos_t, sin_t = _rope_tables(DN)
    gla_consts = _gla_consts()
    wom = w_o_mla.astype(BF16)
    wog = w_o_gla.astype(BF16)
    wout = w_out.astype(BF16)
    wr = jnp.pad(w_router, ((0, 0), (0, 0), (0, LANES - N_EXPERTS)))

    ckr_pad = jnp.pad(cache_krope, ((0, 0), (0, 0), (0, 0), (MLA_NOPE, HEAD_PAD - MLA_NOPE - MLA_ROPE)))
    kc_all, vc_all = _cache_kv(cache_ckv, ckr_pad, wuk, wuv)

    nquad = GLA_W // GLA_QUAD
    hq = GLA_QUAD // GLA_DK
    st_t = jnp.swapaxes(state_gla, -1, -2).reshape(DB, L, 2, nquad, hq, GLA_DV, GLA_DK)
    eye = jnp.eye(hq, dtype=F32)
    s0_all = jnp.einsum("blzqhvk,hj->blzqhvjk", st_t, eye).reshape(DB, L, 2, nquad, GLA_QUAD, GLA_QUAD)

    x = jnp.concatenate([x_prompt.reshape(Tc, D), x_sample.reshape(Tl, D)], axis=0)
    ckv_list, kr_list, gla_list = [], [], []
    for l in range(L):
        pre = _inproj(x, mod[l], g_norm1[l][None], wp[l], g_q[l][None], g_kv[l][None], wuq[l], wuk[l], wuv[l],
                      wg[l], bg[l], cos_t, sin_t, cond_of_tile, tab_of_tile)
        ckv_list.append(pre["ckv"][:Tc].reshape(B, N, KV_RANK))
        kr_list.append(pre["kr"][:Tc, MLA_NOPE:MLA_NOPE + MLA_ROPE].reshape(B, N, MLA_ROPE))

        o_ctx = _attention_ctx(pre["q"], pre["k"], pre["v"], B, N)
        o_lat = _attention_lat(pre["q"], pre["k"], pre["v"], kc_all[l], vc_all[l], Tc, DB, DN, ROW_TILE)

        gg = g_gla[l][None]
        og_ctx, s_fin = _gla(pre["gq"], pre["gk"], pre["gv"], pre["g"], pre["sg"], gg, gla_consts, 0, B, N,
                             want_final=True)
        (og_lat,) = _gla(pre["gq"], pre["gk"], pre["gv"], pre["g"], pre["sg"], gg, gla_consts, Tc, DB, DN,
                         s0=s0_all[:, l])
        gla_list.append(jnp.swapaxes(s_fin, -1, -2))
        o_gla = jnp.concatenate([og_ctx, og_lat], axis=0)

        x1, h2, aff = _merge(x, mod[l], o_ctx, o_lat, o_gla, pre["sa"], pre["sb"], wom[l], wog[l], wout[l],
                             g_norm2[l][None], wr[l], cond_of_tile, n_ctx_tiles)

        aff_t = jnp.swapaxes(aff[:, :N_EXPERTS].reshape(G, T, N_EXPERTS), 1, 2)
        slot, cb = _topk(aff_t, cap)
        cb_flat = cb.reshape(-1)
        slot5 = slot.reshape(G, N_EXPERTS, T // ROW_TILE, 1, ROW_TILE)
        slot_t = jnp.pad(jnp.swapaxes(slot, 1, 2).reshape(G * T, N_EXPERTS),
                         ((0, 0), (0, LANES - N_EXPERTS)), constant_values=-1)
        aff5 = aff_t.reshape(G, N_EXPERTS, T // ROW_TILE, 1, ROW_TILE)
        y = _experts(cb_flat, slot5, aff5, h2, w_e_gate, w_e_up, w_e_down, l, cap, win)
        x = _combine(cb_flat, x1, mod[l], slot_t, y, g_final[None], cond_of_tile, cap, win, final=(l == L - 1))

    y_prompt = x[:Tc].reshape(B, N, D)
    y_sample = x[Tc:].reshape(DB, DN, D)
    new_ckv = jnp.stack(ckv_list, axis=1)
    new_krope = jnp.stack(kr_list, axis=1)
    new_gla = jnp.stack(gla_list, axis=1)
    return (y_prompt, y_sample, new_ckv, new_krope, new_gla)
```

```python
import functools

import jax
import jax.numpy as jnp
import numpy as np
from jax import lax
from jax.experimental import pallas as pl
from jax.experimental.pallas import tpu as pltpu

F32 = jnp.float32
BF16 = jnp.bfloat16
I32 = jnp.int32

GRID_W = 64
EPS = 1e-6
MLA_HEADS = 8
MLA_NOPE = 64
MLA_ROPE = 32
MLA_V = 64
Q_RANK = 384
KV_RANK = 256
ROPE_BASE = 10000.0
GLA_HEADS = 8
GLA_DK = 64
GLA_DV = 64
GLA_W = GLA_HEADS * GLA_DK
GLA_GATE_RANK = 16
GLA_TAU = 16.0
GLA_CHUNK = 64
N_EXPERTS = 16
CAPACITY_FACTOR = 2

LANES = 128
SUBLANES = 8
BF16_ROWS = 16
HEAD_PAD = 128
VMEM_LIMIT = 56 * 1024 * 1024

ROW_TILE = 256
TOKEN_TILE = 256
GLA_QUAD = 4 * GLA_DK
GLA_DIAG = 8
F32_MIN_NORMAL = 2.0 ** -126
TOPK_EXP_BITS = 7
TOPK_BISECT_STEPS = 52
GLA_DK_SHIFT = GLA_DK.bit_length() - 1
assert 1 << GLA_DK_SHIFT == GLA_DK
LOG2E = float(np.log2(np.e))
Q_PRESCALE = (MLA_NOPE + MLA_ROPE) ** -0.5 * LOG2E

_SEG = {}
_off = 0
for _name, _width in (("pq", Q_RANK), ("pkv", KV_RANK), ("kr", HEAD_PAD), ("krs", HEAD_PAD),
                      ("gq", GLA_W), ("gk", GLA_W), ("gv", GLA_W), ("glr", LANES), ("gog", GLA_W)):
    _SEG[_name] = (_off, _width)
    _off += _width


def _cparams(sem, vmem=VMEM_LIMIT):
    return pltpu.CompilerParams(dimension_semantics=sem, vmem_limit_bytes=vmem)


def _dot(a, b):
    return jnp.dot(a, b, preferred_element_type=F32)


def _dot_nt(a, b):
    return lax.dot_general(a, b, (((1,), (1,)), ((), ())), preferred_element_type=F32)


def _dot_tn(a, b):
    return lax.dot_general(a, b, (((0,), (0,)), ((), ())), preferred_element_type=F32)


def _rms(x, g):
    return x * lax.rsqrt(jnp.mean(x * x, axis=-1, keepdims=True) + EPS) * g


def _modulated_norm(x, g, scale, shift):
    return _rms(x, g) * (1.0 + scale) + shift


def _sigmoid(x):
    return 0.5 * jnp.tanh(0.5 * x) + 0.5


def _split3(x):
    a = x.astype(BF16)
    r = x - a.astype(F32)
    b = r.astype(BF16)
    c = (r - b.astype(F32)).astype(BF16)
    return a, b, c


def _mod_body(c_ref, w_ref, b_ref, o_ref):
    c = c_ref[...]
    s = (c * _sigmoid(c)).astype(BF16)
    o_ref[0] = _dot(s, w_ref[0].astype(BF16)) + b_ref[0]


def _adaln_all(cvec, w_mod, b_mod):
    L, D, D6 = w_mod.shape
    R = cvec.shape[0]
    tn = 1536
    return pl.pallas_call(
        _mod_body,
        grid=(L, D6 // tn),
        in_specs=[pl.BlockSpec((R, D), lambda l, j: (0, 0)),
                  pl.BlockSpec((1, D, tn), lambda l, j: (l, 0, j)),
                  pl.BlockSpec((1, 1, tn), lambda l, j: (l, 0, j))],
        out_specs=pl.BlockSpec((1, R, tn), lambda l, j: (l, 0, j)),
        out_shape=jax.ShapeDtypeStruct((L, R, D6), F32),
        compiler_params=_cparams(("arbitrary", "arbitrary")),
        name="adaln_mod",
    )(cvec, w_mod, b_mod.reshape(L, 1, D6))


def _inproj_body(x_ref, mod_ref, g1_ref, w_ref, gq_ref, gkv_ref, wuq_ref, wuk_ref, wuv_ref,
                 wg_ref, bg_ref, ct_ref, st_ref,
                 q_ref, k_ref, v_ref, ckv_ref, kr_ref, gqo_ref, gko_ref, gvo_ref, g_ref,
                 sg_ref, *, D):
    mod = mod_ref[0]
    h = _modulated_norm(x_ref[...], g1_ref[...], mod[:, D:2 * D], mod[:, 0:D]).astype(BF16)

    def seg(name):
        a, w = _SEG[name]
        return _dot(h, w_ref[:, a:a + w])

    cos = ct_ref[...]
    sin = st_ref[...]
    nq = MLA_HEADS * HEAD_PAD

    cq = _rms(seg("pq"), gq_ref[...]).astype(BF16)
    qq = _dot(cq, wuq_ref[...])
    for hd in range(MLA_HEADS):
        a = hd * HEAD_PAD
        q_ref[:, a:a + HEAD_PAD] = ((qq[:, a:a + HEAD_PAD] * cos
                                     + qq[:, nq + a:nq + a + HEAD_PAD] * sin) * Q_PRESCALE).astype(BF16)

    ckv = _rms(seg("pkv"), gkv_ref[...])
    ckv_ref[...] = ckv
    ckv_b = ckv.astype(BF16)
    kr = seg("kr")
    kr_ref[...] = kr
    kr_rot = kr * cos + seg("krs") * sin
    kn = _dot(ckv_b, wuk_ref[...])
    for hd in range(MLA_HEADS):
        a = hd * HEAD_PAD
        k_ref[:, a:a + HEAD_PAD] = (kn[:, a:a + HEAD_PAD] + kr_rot).astype(BF16)
    v_ref[...] = _dot(ckv_b, wuv_ref[...]).astype(BF16)

    gqo_ref[...] = seg("gq") * (GLA_DK ** -0.5)
    gko_ref[...] = seg("gk")
    gvo_ref[...] = seg("gv")
    logit = _dot(seg("glr").astype(BF16), wg_ref[...]) + bg_ref[...]
    g_ref[...] = (jnp.minimum(logit, 0.0) - jnp.log1p(jnp.exp(-jnp.abs(logit)))) * (1.0 / GLA_TAU)
    gog = seg("gog")
    sg_ref[...] = gog * _sigmoid(gog)


def _layer_spec(a, layer):
    nd = a.ndim - 1
    return pl.BlockSpec((None,) + a.shape[1:], lambda i: (layer,) + (0,) * nd, pipeline_mode=pl.Buffered(1))


def _mod_spec(R, D, layer, cond_of_tile):
    return pl.BlockSpec((1, 1, 6 * D), lambda i: (layer * R + cond_of_tile(i), 0, 0))


def _inproj(x, mod, layer, g1, wp, gq, gkv, wuq, wuk, wuv, wg, bg, cos_t, sin_t, cond_of_tile, tab_of_tile):
    T2, D = x.shape
    nt = T2 // TOKEN_TILE
    L, R = mod.shape[:2]
    row = lambda w: pl.BlockSpec((TOKEN_TILE, w), lambda i: (i, 0))
    outs = [("q", MLA_HEADS * HEAD_PAD, BF16), ("k", MLA_HEADS * HEAD_PAD, BF16),
            ("v", MLA_HEADS * MLA_V, BF16), ("ckv", KV_RANK, F32), ("kr", HEAD_PAD, F32),
            ("gq", GLA_W, F32), ("gk", GLA_W, F32), ("gv", GLA_W, F32), ("g", 2 * GLA_W, F32),
            ("sg", GLA_W, F32)]
    params = [g1, wp, gq, gkv, wuq, wuk, wuv, wg, bg]
    res = pl.pallas_call(
        functools.partial(_inproj_body, D=D),
        grid=(nt,),
        in_specs=[row(D), _mod_spec(R, D, layer, cond_of_tile)]
                 + [_layer_spec(a, layer) for a in params]
                 + [pl.BlockSpec((TOKEN_TILE, HEAD_PAD), lambda i: (tab_of_tile(i), 0)),
                    pl.BlockSpec((TOKEN_TILE, HEAD_PAD), lambda i: (tab_of_tile(i), 0))],
        out_specs=[row(w) for _, w, _ in outs],
        out_shape=[jax.ShapeDtypeStruct((T2, w), dt) for _, w, dt in outs],
        compiler_params=_cparams(("arbitrary",)),
        name="inproj",
    )(x, mod.reshape(L * R, 1, 6 * D), *params, cos_t, sin_t)
    return dict(zip([n for n, _, _ in outs], res))


def _cache_kv_body(ckv_ref, krp_ref, wuk_ref, wuv_ref, k_ref, v_ref):
    c = ckv_ref[...].astype(BF16)
    kn = _dot(c, wuk_ref[...])
    krp = krp_ref[...]
    for hd in range(MLA_HEADS):
        a = hd * HEAD_PAD
        k_ref[:, a:a + HEAD_PAD] = (kn[:, a:a + HEAD_PAD] + krp).astype(BF16)
    v_ref[...] = _dot(c, wuv_ref[...]).astype(BF16)


def _cache_kv(cache_ckv, cache_kr_pad, wuk, wuv):
    DB, L, P, R = cache_ckv.shape
    nk = MLA_HEADS * HEAD_PAD
    nv = MLA_HEADS * MLA_V
    return pl.pallas_call(
        _cache_kv_body,
        grid=(L, DB),
        in_specs=[pl.BlockSpec((None, None, P, R), lambda l, b: (b, l, 0, 0)),
                  pl.BlockSpec((None, None, P, HEAD_PAD), lambda l, b: (b, l, 0, 0)),
                  pl.BlockSpec((None, R, nk), lambda l, b: (l, 0, 0)),
                  pl.BlockSpec((None, R, nv), lambda l, b: (l, 0, 0))],
        out_specs=[pl.BlockSpec((None, None, P, nk), lambda l, b: (l, b, 0, 0)),
                   pl.BlockSpec((None, None, P, nv), lambda l, b: (l, b, 0, 0))],
        out_shape=[jax.ShapeDtypeStruct((L, DB, P, nk), BF16),
                   jax.ShapeDtypeStruct((L, DB, P, nv), BF16)],
        compiler_params=_cparams(("arbitrary", "arbitrary")),
        name="cache_kv",
    )(cache_ckv, cache_kr_pad, wuk, wuv)


def _attn_body(*refs, nseg):
    q_ref = refs[0]
    k_refs = refs[1:1 + nseg]
    v_refs = refs[1 + nseg:1 + 2 * nseg]
    o_ref = refs[1 + 2 * nseg]
    lane = lax.broadcasted_iota(I32, (1, 2 * MLA_V), 1)
    outs = []
    for hh in range(2):
        qh = q_ref[:, hh * HEAD_PAD:(hh + 1) * HEAD_PAD]
        s = [_dot_nt(qh, kr[:, hh * HEAD_PAD:(hh + 1) * HEAD_PAD]) for kr in k_refs]
        m = s[0].max(axis=-1, keepdims=True)
        for sj in s[1:]:
            m = jnp.maximum(m, sj.max(axis=-1, keepdims=True))
        p = [jnp.exp2(sj - m) for sj in s]
        den = p[0].sum(axis=-1, keepdims=True)
        for pj in p[1:]:
            den = den + pj.sum(axis=-1, keepdims=True)
        o = None
        for pj, vr in zip(p, v_refs):
            t = _dot(pj.astype(BF16), vr[...])
            o = t if o is None else o + t
        outs.append(o * (1.0 / den))
    o_ref[...] = jnp.where(lane < MLA_V, outs[0], outs[1]).astype(BF16)


def _attention_ctx(q, k, v, nseq, n):
    hp = MLA_HEADS // 2
    return pl.pallas_call(
        functools.partial(_attn_body, nseg=1),
        grid=(nseq, hp),
        in_specs=[pl.BlockSpec((n, 2 * HEAD_PAD), lambda b, h: (b, h)),
                  pl.BlockSpec((n, 2 * HEAD_PAD), lambda b, h: (b, h)),
                  pl.BlockSpec((n, 2 * MLA_V), lambda b, h: (b, h))],
        out_specs=pl.BlockSpec((n, 2 * MLA_V), lambda b, h: (b, h)),
        out_shape=jax.ShapeDtypeStruct((nseq * n, MLA_HEADS * MLA_V), BF16),
        compiler_params=_cparams(("arbitrary", "arbitrary")),
        name="attn_ctx",
    )(q, k, v)


def _attention_lat(q, k, v, kc, vc, layer, row0, nseq, n, tq):
    hp = MLA_HEADS // 2
    P = kc.shape[2]
    qt = n // tq
    q0 = row0 // tq
    s0 = row0 // n
    return pl.pallas_call(
        functools.partial(_attn_body, nseg=2),
        grid=(nseq, hp, qt),
        in_specs=[pl.BlockSpec((tq, 2 * HEAD_PAD), lambda b, h, t: (q0 + b * qt + t, h)),
                  pl.BlockSpec((None, None, P, 2 * HEAD_PAD), lambda b, h, t: (layer, b, 0, h)),
                  pl.BlockSpec((n, 2 * HEAD_PAD), lambda b, h, t: (s0 + b, h)),
                  pl.BlockSpec((None, None, P, 2 * MLA_V), lambda b, h, t: (layer, b, 0, h)),
                  pl.BlockSpec((n, 2 * MLA_V), lambda b, h, t: (s0 + b, h))],
        out_specs=pl.BlockSpec((tq, 2 * MLA_V), lambda b, h, t: (b * qt + t, h)),
        out_shape=jax.ShapeDtypeStruct((nseq * n, MLA_HEADS * MLA_V), BF16),
        compiler_params=_cparams(("arbitrary", "arbitrary", "arbitrary")),
        name="attn_lat",
    )(q, kc, k, vc, v)


def _head_sums(x, ones_quad):
    return jnp.concatenate([_dot(x[:, a:a + GLA_QUAD], ones_quad) for a in range(0, GLA_W, GLA_QUAD)], axis=1)


def _gla_chunk(q, k, v, g, st_refs, fwd, tri, ones_bd):
    C = GLA_CHUNK
    W = GLA_W
    nquad = W // GLA_QUAD
    rows = lax.broadcasted_iota(I32, (C, 1), 0)

    g1, g2, g3 = _split3(g * LOG2E)
    cum = _dot(tri, jnp.concatenate([g1, g2, g3], axis=1))
    cum = cum[:, 0:W] + cum[:, W:2 * W] + cum[:, 2 * W:3 * W]
    edge = C - 1 if fwd else 0
    last = cum[edge:edge + 1]
    q_in = (q * jnp.exp2(cum)).astype(BF16)
    k_st = (k * jnp.exp2(last - cum)).astype(BF16)
    v_b = v.astype(BF16)

    lane_q = lax.broadcasted_iota(I32, (1, GLA_QUAD), 1)
    head_masks = [jnp.right_shift(lane_q, GLA_DK_SHIFT) == h for h in range(GLA_QUAD // GLA_DK)]
    col_s = jnp.bitwise_and(lane_q, C - 1)

    def stack_heads(xq):
        return jnp.concatenate([jnp.where(mh, xq, jnp.zeros_like(xq)) for mh in head_masks], axis=0)

    a_acc = [jnp.zeros((C, GLA_QUAD), F32) for _ in range(nquad)]
    half = C // 2
    while half >= GLA_DIAG:
        blk = 2 * half
        pieces = []
        for p in range(C // blk):
            rr = p * blk + (half - 1 if fwd else half)
            pieces.append(jnp.broadcast_to(cum[rr:rr + 1], (blk, W)))
        ref = jnp.concatenate(pieces, axis=0) if len(pieces) > 1 else pieces[0]
        upper = jnp.bitwise_and(rows, blk - 1) >= half
        qmask = upper if fwd else jnp.logical_not(upper)
        qe = jnp.where(qmask, q * jnp.exp2(cum - ref), 0.0).astype(BF16)
        ke = jnp.where(qmask, 0.0, k * jnp.exp2(ref - cum)).astype(BF16)
        sh = blk.bit_length() - 1
        same = jnp.right_shift(rows, sh) == jnp.right_shift(col_s, sh)
        for qd in range(nquad):
            sl = slice(qd * GLA_QUAD, (qd + 1) * GLA_QUAD)
            a = _dot_nt(qe[:, sl], stack_heads(ke[:, sl]))
            a_acc[qd] = a_acc[qd] + jnp.where(same, a, 0.0)
        half //= 2

    r2 = jnp.right_shift(lax.broadcasted_iota(I32, (GLA_QUAD, GLA_QUAD), 0), GLA_DK_SHIFT)
    c2 = jnp.right_shift(lax.broadcasted_iota(I32, (GLA_QUAD, GLA_QUAD), 1), GLA_DK_SHIFT)
    o_parts = []
    for qd in range(nquad):
        sl = slice(qd * GLA_QUAD, (qd + 1) * GLA_QUAD)
        st = st_refs[qd][...]
        o = _dot_nt(q_in[:, sl], st.astype(BF16))
        o = o + _dot(a_acc[qd].astype(BF16), stack_heads(v_b[:, sl]))
        o_parts.append(o)
        upd = _dot_tn(v_b[:, sl], k_st[:, sl])
        st_refs[qd][...] = st * jnp.exp2(last[:, sl]) + jnp.where(r2 == c2, upd, 0.0)
    o = jnp.concatenate(o_parts, axis=1)

    nb = C // GLA_DIAG
    q3 = q.reshape(nb, GLA_DIAG, W)
    k3 = k.reshape(nb, GLA_DIAG, W)
    v3 = v.reshape(nb, GLA_DIAG, W)
    c3 = cum.reshape(nb, GLA_DIAG, W)
    tl = lax.broadcasted_iota(I32, (1, GLA_DIAG, 1), 1)
    zs = []
    for s in range(GLA_DIAG):
        e = jnp.exp2(c3 - c3[:, s:s + 1, :])
        valid = (tl >= s) if fwd else (tl <= s)
        zs.append(jnp.where(valid, q3 * e * k3[:, s:s + 1, :], 0.0).reshape(C, W).astype(BF16))
    w_all = _head_sums(jnp.concatenate(zs, axis=0), ones_bd)
    od = jnp.zeros((nb, GLA_DIAG, W), F32)
    for s in range(GLA_DIAG):
        od = od + w_all[s * C:(s + 1) * C].reshape(nb, GLA_DIAG, W) * v3[:, s:s + 1, :]
    return o + od.reshape(C, W)


def _gla_body(*refs, n, has_init, has_final):
    it = iter(refs)
    q_ref, k_ref, v_ref, g_ref, sg_ref, gg_ref, trif_ref, trib_ref, ones_ref = (next(it) for _ in range(9))
    s0_ref = next(it) if has_init else None
    o_ref = next(it)
    sf_ref = next(it) if has_final else None
    of_ref, ob_ref = next(it), next(it)
    st =[[next(it) for _ in range(GLA_W // GLA_QUAD)] for _ in range(2)]

    C = GLA_CHUNK
    nc = n // C
    nquad = GLA_W // GLA_QUAD
    hq = GLA_QUAD // GLA_DK
    for d in range(2):
        for qd in range(nquad):
            st[d][qd][...] = jnp.zeros((GLA_QUAD, GLA_QUAD), F32)
            if has_init:
                for h in range(hq):
                    hs = slice(h * GLA_DK, (h + 1) * GLA_DK)
                    st[d][qd][hs, hs] = s0_ref[d, qd * hq + h]

    trif = trif_ref[...]
    trib = trib_ref[...]
    ones_bd = ones_ref[...]

    def step(i, carry):
        for d, (tri, acc) in enumerate(((trif, of_ref), (trib, ob_ref))):
            c = i if d == 0 else nc - 1 - i
            r0 = pl.multiple_of(c * C, C)
            rs = pl.ds(r0, C)
            gd = g_ref[rs, d * GLA_W:(d + 1) * GLA_W]
            acc[rs, :] = _gla_chunk(q_ref[rs, :], k_ref[rs, :], v_ref[rs, :], gd, st[d], d == 0, tri, ones_bd)
        return carry

    lax.fori_loop(0, nc, step, 0)

    gg = gg_ref[...]
    fr = min(n, 256)

    def fin(i, carry):
        rs = pl.ds(pl.multiple_of(i * fr, fr), fr)
        o = of_ref[rs, :] + ob_ref[rs, :]
        ms = _head_sums(jnp.concatenate(_split3(o * o)[:2], axis=0), ones_bd)
        ms = (ms[0:fr] + ms[fr:2 * fr]) * (1.0 / GLA_DV)
        o_ref[rs, :] = (o * lax.rsqrt(ms + EPS) * gg * sg_ref[rs, :]).astype(BF16)
        return carry

    lax.fori_loop(0, n // fr, fin, 0)

    if has_final:
        for d in range(2):
            for qd in range(nquad):
                s = st[d][qd][...]
                for h in range(GLA_QUAD // GLA_DK):
                    sf_ref[d, qd * (GLA_QUAD // GLA_DK) + h] = s[h * GLA_DK:(h + 1) * GLA_DK,
                                                                 h * GLA_DK:(h + 1) * GLA_DK]


def _gla(gq, gk, gv, g, sg, g_gla, consts, row0, nseq, n, s0=None, want_final=False):
    trif, trib, ones_bd = consts
    b0 = row0 // n
    W = GLA_W
    nquad = W // GLA_QUAD
    seq = lambda w: pl.BlockSpec((n, w), lambda b: (b0 + b, 0))
    full = lambda a: pl.BlockSpec(a.shape, lambda b: (0,) * a.ndim)
    in_specs = [seq(W), seq(W), seq(W), seq(2 * W), seq(W), full(g_gla), full(trif), full(trib), full(ones_bd)]
    args = [gq, gk, gv, g, sg, g_gla, trif, trib, ones_bd]
    if s0 is not None:
        s0_arr, layer = s0
        in_specs.append(pl.BlockSpec((None, None, 2, GLA_HEADS, GLA_DV, GLA_DK), lambda b: (b, layer, 0, 0, 0, 0)))
        args.append(s0_arr)
    out_specs = [pl.BlockSpec((n, W), lambda b: (b, 0))]
    out_shape = [jax.ShapeDtypeStruct((nseq * n, W), BF16)]
    if want_final:
        out_specs.append(pl.BlockSpec((None, 2, GLA_HEADS, GLA_DV, GLA_DK), lambda b: (b, 0, 0, 0, 0)))
        out_shape.append(jax.ShapeDtypeStruct((nseq, 2, GLA_HEADS, GLA_DV, GLA_DK), F32))
    scratch = [pltpu.VMEM((n, W), F32), pltpu.VMEM((n, W), F32)]
    scratch += [pltpu.VMEM((GLA_QUAD, GLA_QUAD), F32) for _ in range(2 * nquad)]
    res = pl.pallas_call(
        functools.partial(_gla_body, n=n, has_init=s0 is not None, has_final=want_final),
        grid=(nseq,),
        in_specs=in_specs,
        out_specs=out_specs,
        out_shape=out_shape,
        scratch_shapes=scratch,
        compiler_params=_cparams(("arbitrary",)),
        name="gla_lat" if s0 is not None else "gla_ctx",
    )(*args)
    return res


def _merge_body(x_ref, mod_ref, oc_ref, ol_ref, gc_ref, gl_ref, g1_ref, wgate_ref, wom_ref, wog_ref, wout_ref,
                g2_ref, wr_ref, x1_ref, h2_ref, aff_ref, *, D, n_ctx_tiles):
    is_ctx = pl.program_id(0) < n_ctx_tiles
    mod = mod_ref[0]
    x = x_ref[...]
    h = _modulated_norm(x, g1_ref[...], mod[:, D:2 * D], mod[:, 0:D]).astype(BF16)
    om = _dot(jnp.where(is_ctx, oc_ref[...], ol_ref[...]), wom_ref[...])
    merged = _sigmoid(_dot(h, wgate_ref[:, 0:D])) * om
    og = _dot(jnp.where(is_ctx, gc_ref[...], gl_ref[...]), wog_ref[...])
    merged = (merged + _sigmoid(_dot(h, wgate_ref[:, D:2 * D])) * og).astype(BF16)
    mix = _dot(merged, wout_ref[...])
    x1 = x + mod[:, 2 * D:3 * D] * mix
    x1_ref[...] = x1
    h2 = _modulated_norm(x1, g2_ref[...], mod[:, 4 * D:5 * D], mod[:, 3 * D:4 * D])
    h2_ref[...] = h2.astype(BF16)
    a, b, c = _split3(h2)
    wa, wb, wc = _split3(wr_ref[...])
    logits = (_dot(a, wa) + _dot(a, wb) + _dot(b, wa)) + (_dot(a, wc) + _dot(b, wb) + _dot(c, wa))
    lane = lax.broadcasted_iota(I32, logits.shape, 1)
    logits = jnp.where(lane < N_EXPERTS, logits, -jnp.inf)
    p = jnp.exp(logits - logits.max(axis=-1, keepdims=True))
    aff_ref[...] = p / p.sum(axis=-1, keepdims=True)


def _merge(x, mod, layer, o_ctx, o_lat, og_ctx, og_lat, g1, wgate, wom, wog, wout, g2, wr, cond_of_tile,
           n_ctx_tiles):
    T2, D = x.shape
    nt = T2 // TOKEN_TILE
    L, R = mod.shape[:2]
    n_lat_tiles = nt - n_ctx_tiles
    row = lambda w: pl.BlockSpec((TOKEN_TILE, w), lambda i: (i, 0))
    ctx = lambda w: pl.BlockSpec((TOKEN_TILE, w), lambda i: (jnp.minimum(i, n_ctx_tiles - 1), 0))
    lat = lambda w: pl.BlockSpec((TOKEN_TILE, w), lambda i: (jnp.clip(i - n_ctx_tiles, 0, n_lat_tiles - 1), 0))
    W = o_ctx.shape[1]
    params = [g1, wgate, wom, wog, wout, g2, wr]
    return pl.pallas_call(
        functools.partial(_merge_body, D=D, n_ctx_tiles=n_ctx_tiles),
        grid=(nt,),
        in_specs=[row(D), _mod_spec(R, D, layer, cond_of_tile), ctx(W), lat(W), ctx(GLA_W), lat(GLA_W)]
                 + [_layer_spec(a, layer) for a in params],
        out_specs=[row(D), row(D), row(LANES)],
        out_shape=[jax.ShapeDtypeStruct((T2, D), F32), jax.ShapeDtypeStruct((T2, D), BF16),
                   jax.ShapeDtypeStruct((T2, LANES), F32)],
        compiler_params=_cparams(("arbitrary",)),
        name="merge_router",
    )(x, mod.reshape(L * R, 1, 6 * D), o_ctx, o_lat, og_ctx, og_lat, *params)


def _topk_body(a_ref, slot_ref, cb_ref, *, T, cap):
    a = a_ref[0]
    E = a.shape[0]

    def count_ge(thr):
        return jnp.sum((a >= thr).astype(F32), axis=1, keepdims=True)

    hi = jnp.full((E, 1), 2.0, F32)
    for j in range(TOPK_EXP_BITS - 1, -1, -1):
        cand = hi * (2.0 ** -(2 ** j))
        hi = jnp.where(count_ge(cand) < cap, cand, hi)
    lo = jnp.where(hi > F32_MIN_NORMAL, 0.5 * hi, 0.0)

    def bisect(_, lh):
        lo, hi = lh
        mid = 0.5 * (lo + hi)
        up = count_ge(mid) >= cap
        return jnp.where(up, mid, lo), jnp.where(up, hi, mid)

    lo, hi = lax.fori_loop(0, TOPK_BISECT_STEPS, bisect, (lo, hi))
    gt = a >= hi
    eq = jnp.logical_and(a >= lo, a < hi)
    need = cap - jnp.sum(gt.astype(F32), axis=1, keepdims=True)

    r = lax.broadcasted_iota(I32, (LANES, LANES), 0)
    c = lax.broadcasted_iota(I32, (LANES, LANES), 1)
    triu = (r < c).astype(BF16)
    lane = lax.broadcasted_iota(I32, (E, LANES), 1)

    nb = T // LANES
    per_tile = ROW_TILE // LANES
    carry_eq = jnp.zeros((E, 1), F32)
    carry_sel = jnp.zeros((E, 1), F32)
    cb = jnp.zeros((E, LANES), I32)
    for j in range(nb):
        sl = slice(j * LANES, (j + 1) * LANES)
        eq_j = eq[:, sl].astype(BF16)
        pre = _dot(eq_j, triu) + carry_eq
        carry_eq = carry_eq + jnp.sum(eq_j.astype(F32), axis=1, keepdims=True)
        sel = jnp.logical_or(gt[:, sl], jnp.logical_and(eq[:, sl], pre < need))
        sel_b = sel.astype(BF16)
        if j % per_tile == 0:
            cb = jnp.where(lane == j // per_tile, carry_sel.astype(I32), cb)
        slot = (_dot(sel_b, triu) + carry_sel).astype(I32)
        carry_sel = carry_sel + jnp.sum(sel_b.astype(F32), axis=1, keepdims=True)
        slot_ref[0, :, sl] = jnp.where(sel, slot, -1)
    cb_ref[0] = jnp.where(lane == nb // per_tile, carry_sel.astype(I32), cb)


def _topk(aff_t, cap):
    G, E, T = aff_t.shape
    return pl.pallas_call(
        functools.partial(_topk_body, T=T, cap=cap),
        grid=(G,),
        in_specs=[pl.BlockSpec((1, E, T), lambda g: (g, 0, 0))],
        out_specs=[pl.BlockSpec((1, E, T), lambda g: (g, 0, 0)),
                   pl.BlockSpec((1, E, LANES), lambda g: (g, 0, 0))],
        out_shape=[jax.ShapeDtypeStruct((G, E, T), I32), jax.ShapeDtypeStruct((G, E, LANES), I32)],
        compiler_params=_cparams(("arbitrary",)),
        name="expert_topk",
    )(aff_t)


def _expert_body(cb_ref, slot_ref, aff_ref, h_ref, wg_ref, wu_ref, wd_ref, y_ref, xs_ref, acc_ref, ws_ref, *,
                 cap, win, nt, nfh, E):
    g = pl.program_id(0)
    e = pl.program_id(1)
    fh = pl.program_id(2)
    base = (g * E + e) * LANES

    @pl.when(fh == 0)
    def _gather():
        xs_ref[...] = jnp.zeros_like(xs_ref)
        ws_ref[...] = jnp.zeros_like(ws_ref)

        def tile(i, carry):
            lo = cb_ref[base + i]
            hi = cb_ref[base + i + 1]
            srow = slot_ref[i]
            arow = aff_ref[i]
            for w in range(cap // win):
                @pl.when(jnp.logical_and(lo < (w + 1) * win, hi > w * win))
                def _():
                    j = lax.broadcasted_iota(I32, (win, 1), 0) + w * win
                    hit = srow == j
                    hs = h_ref[pl.ds(pl.multiple_of(i * ROW_TILE, ROW_TILE), ROW_TILE), :]
                    xs_ref[w * win:(w + 1) * win, :] += _dot(hit.astype(BF16), hs)
                    ws_ref[w * win:(w + 1) * win, :] += jnp.sum(jnp.where(hit, arow, 0.0), axis=1, keepdims=True)
            return carry

        lax.fori_loop(0, nt, tile, 0)

    xb = xs_ref[...].astype(BF16)
    gate = _dot(xb, wg_ref[...].astype(BF16))
    up = _dot(xb, wu_ref[...].astype(BF16))
    hid = (gate * _sigmoid(gate) * up).astype(BF16)
    part = _dot(hid, wd_ref[...].astype(BF16))

    @pl.when(fh == 0)
    def _():
        acc_ref[...] = part

    @pl.when(fh > 0)
    def _():
        acc_ref[...] += part

    @pl.when(fh == nfh - 1)
    def _():
        y_ref[...] = (acc_ref[...] * ws_ref[...]).astype(BF16)


def _experts(cb_flat, slot5, aff5, h2, w_gate, w_up, w_down, layer, cap, win):
    G, E, nt = slot5.shape[:3]
    T = nt * ROW_TILE
    D = h2.shape[1]
    FF = w_gate.shape[-1]
    nfh = 2
    fb = FF // nfh
    grid_spec = pltpu.PrefetchScalarGridSpec(
        num_scalar_prefetch=1,
        grid=(G, E, nfh),
        in_specs=[pl.BlockSpec((None, None, nt, 1, ROW_TILE), lambda g, e, f, cb: (g, e, 0, 0, 0)),
                  pl.BlockSpec((None, None, nt, 1, ROW_TILE), lambda g, e, f, cb: (g, e, 0, 0, 0)),
                  pl.BlockSpec((T, D), lambda g, e, f, cb: (g, 0)),
                  pl.BlockSpec((None, None, D, fb), lambda g, e, f, cb: (layer, e, 0, f)),
                  pl.BlockSpec((None, None, D, fb), lambda g, e, f, cb: (layer, e, 0, f)),
                  pl.BlockSpec((None, None, fb, D), lambda g, e, f, cb: (layer, e, f, 0))],
        out_specs=pl.BlockSpec((None, None, cap, D), lambda g, e, f, cb: (g, e, 0, 0)),
        scratch_shapes=[pltpu.VMEM((cap, D), F32), pltpu.VMEM((cap, D), F32), pltpu.VMEM((cap, 1), F32)],
    )
    return pl.pallas_call(
        functools.partial(_expert_body, cap=cap, win=win, nt=nt, nfh=nfh, E=E),
        grid_spec=grid_spec,
        out_shape=jax.ShapeDtypeStruct((G, E, cap, D), BF16),
        compiler_params=_cparams(("arbitrary", "arbitrary", "arbitrary")),
        name="expert_ffn",
    )(cb_flat, slot5, aff5, h2, w_gate, w_up, w_down)


def _combine_body(cb_ref, x_ref, mod_ref, slot_ref, y_ref, spread_ref, gf_ref, o_ref, acc_ref, *,
                  D, cap, win, cw, E, tiles_per_group, final):
    i = pl.program_id(0)
    g = i // tiles_per_group
    ti = i % tiles_per_group
    slots = slot_ref[...]
    los = [cb_ref[(g * E + e) * LANES + ti] for e in range(E)]
    his = [cb_ref[(g * E + e) * LANES + ti + 1] for e in range(E)]
    starts = [jnp.minimum(jnp.bitwise_and(lo, -BF16_ROWS), cap - cw) for lo in los]
    fits = his[0] <= starts[0] + cw
    for e in range(1, E):
        fits = jnp.logical_and(fits, his[e] <= starts[e] + cw)

    @pl.when(fits)
    def _fast():
        sp1 = slots + 1
        digits = jnp.concatenate([jnp.right_shift(sp1, 4), jnp.bitwise_and(sp1, 15)], axis=1)
        spread = _dot(digits.astype(F32).astype(BF16), spread_ref[...])
        lane = lax.broadcasted_iota(I32, (1, cw), 1)
        tgt = jnp.concatenate([lane + (starts[e] + 1) for e in range(E)], axis=1).astype(F32)
        onehot = (spread == tgt).astype(BF16)
        rows = jnp.concatenate([y_ref[e, pl.ds(pl.multiple_of(starts[e], BF16_ROWS), cw), :] for e in range(E)],
                               axis=0)
        acc_ref[...] = _dot(onehot, rows)

    @pl.when(jnp.logical_not(fits))
    def _general():
        acc_ref[...] = jnp.zeros_like(acc_ref)
        for e in range(E):
            col = slots[:, e:e + 1]
            for w in range(cap // win):
                @pl.when(jnp.logical_and(los[e] < (w + 1) * win, his[e] > w * win))
                def _():
                    j = lax.broadcasted_iota(I32, (1, win), 1) + w * win
                    oh = (col == j).astype(BF16)
                    acc_ref[...] += _dot(oh, y_ref[e, w * win:(w + 1) * win, :])

    mod = mod_ref[0]
    x2 = x_ref[...] + mod[:, 5 * D:6 * D] * acc_ref[...]
    if final:
        x2 = _rms(x2, gf_ref[...])
    o_ref[...] = x2


def _combine(cb_flat, x1, mod_l, slot_t, y, g_final, cond_of_tile, cap, win, final):
    T2, D = x1.shape
    G, E = y.shape[:2]
    nt = T2 // ROW_TILE
    tpg = nt // G
    R = mod_l.shape[0]
    cw = min(LANES, cap)
    spread = np.zeros((2 * LANES, E * cw), np.float32)
    for e in range(E):
        spread[e, e * cw:(e + 1) * cw] = 16.0
        spread[LANES + e, e * cw:(e + 1) * cw] = 1.0
    spread = jnp.asarray(spread, BF16)
    grid_spec = pltpu.PrefetchScalarGridSpec(
        num_scalar_prefetch=1,
        grid=(nt,),
        in_specs=[pl.BlockSpec((ROW_TILE, D), lambda i, cb: (i, 0)),
                  pl.BlockSpec((1, 1, 6 * D), lambda i, cb: (cond_of_tile(i), 0, 0)),
                  pl.BlockSpec((ROW_TILE, LANES), lambda i, cb: (i, 0)),
                  pl.BlockSpec((None, E, cap, D), lambda i, cb: (i // tpg, 0, 0, 0)),
                  pl.BlockSpec(spread.shape, lambda i, cb: (0, 0)),
                  pl.BlockSpec((1, D), lambda i, cb: (0, 0))],
        out_specs=pl.BlockSpec((ROW_TILE, D), lambda i, cb: (i, 0)),
        scratch_shapes=[pltpu.VMEM((ROW_TILE, D), F32)],
    )
    return pl.pallas_call(
        functools.partial(_combine_body, D=D, cap=cap, win=win, cw=cw, E=E, tiles_per_group=tpg, final=final),
        grid_spec=grid_spec,
        out_shape=jax.ShapeDtypeStruct((T2, D), F32),
        compiler_params=_cparams(("arbitrary",)),
        name="moe_combine",
    )(cb_flat, x1, mod_l.reshape(R, 1, 6 * D), slot_t, y, spread, g_final)


def _pack_weights(w_in, w_uq, w_uk, w_uv, w_gla_gate, b_gla_gate):
    L, D, _ = w_in.shape
    sizes = (Q_RANK, KV_RANK, MLA_ROPE, GLA_W, GLA_W, GLA_W, 2 * GLA_GATE_RANK, GLA_W, D, D)
    idx = np.cumsum(sizes)[:-1]
    pq, pkv, kr, gq, gk, gv, glr, gog, ga, gb = jnp.split(w_in, [int(i) for i in idx], axis=-1)
    npair = MLA_ROPE // 4
    swap = np.concatenate([np.arange(npair, 2 * npair), np.arange(0, npair),
                           np.arange(3 * npair, 4 * npair), np.arange(2 * npair, 3 * npair)])

    def slot_rope(w):
        return jnp.pad(w, ((0, 0), (0, 0), (MLA_NOPE, HEAD_PAD - MLA_NOPE - MLA_ROPE)))

    glr_p = jnp.pad(glr, ((0, 0), (0, 0), (0, LANES - 2 * GLA_GATE_RANK)))
    wp = jnp.concatenate([pq, pkv, slot_rope(kr), slot_rope(kr[..., swap]), gq, gk, gv, glr_p, gog],
                         axis=-1).astype(BF16)
    wgate = jnp.concatenate([ga, gb], axis=-1).astype(BF16)

    uq = w_uq.reshape(L, Q_RANK, MLA_HEADS, MLA_NOPE + MLA_ROPE)
    pad_h = HEAD_PAD - MLA_NOPE - MLA_ROPE
    uq_n = jnp.pad(uq, ((0, 0), (0, 0), (0, 0), (0, pad_h))).reshape(L, Q_RANK, MLA_HEADS * HEAD_PAD)
    uq_s = jnp.concatenate([jnp.zeros_like(uq[..., :MLA_NOPE]), uq[..., MLA_NOPE:][..., swap]], axis=-1)
    uq_s = jnp.pad(uq_s, ((0, 0), (0, 0), (0, 0), (0, pad_h))).reshape(L, Q_RANK, MLA_HEADS * HEAD_PAD)
    wuq = jnp.concatenate([uq_n, uq_s], axis=-1).astype(BF16)

    uk = w_uk.reshape(L, KV_RANK, MLA_HEADS, MLA_NOPE)
    wuk = jnp.pad(uk, ((0, 0), (0, 0), (0, 0), (0, HEAD_PAD - MLA_NOPE))).reshape(
        L, KV_RANK, MLA_HEADS * HEAD_PAD).astype(BF16)
    wuv = w_uv.astype(BF16)

    wg = jnp.zeros((L, LANES, 2 * GLA_W), F32)
    wg = wg.at[:, 0:GLA_GATE_RANK, 0:GLA_W].set(w_gla_gate[:, 0])
    wg = wg.at[:, GLA_GATE_RANK:2 * GLA_GATE_RANK, GLA_W:].set(w_gla_gate[:, 1])
    bg = b_gla_gate.reshape(L, 1, 2 * GLA_W)
    return wp, wgate, wuq, wuk, wuv, wg.astype(BF16), bg


def _rope_tables(n_lat):
    npair = MLA_ROPE // 4
    freqs = ROPE_BASE ** (-jnp.arange(npair, dtype=F32) / npair)
    pos = jnp.arange(n_lat)
    ang_r = (pos // GRID_W).astype(F32)[:, None] * freqs
    ang_c = (pos % GRID_W).astype(F32)[:, None] * freqs
    cr, sr, cc, sc = jnp.cos(ang_r), jnp.sin(ang_r), jnp.cos(ang_c), jnp.sin(ang_c)
    cos32 = jnp.concatenate([cr, cr, cc, cc], axis=-1)
    sin32 = jnp.concatenate([-sr, sr, -sc, sc], axis=-1)
    pad_h = HEAD_PAD - MLA_NOPE - MLA_ROPE
    ones = jnp.ones((n_lat, MLA_NOPE), F32)
    cos_l = jnp.concatenate([ones, cos32, jnp.zeros((n_lat, pad_h), F32)], axis=-1)
    sin_l = jnp.pad(sin32, ((0, 0), (MLA_NOPE, pad_h)))
    cos_i = jnp.concatenate([jnp.ones((TOKEN_TILE, MLA_NOPE + MLA_ROPE), F32),
                             jnp.zeros((TOKEN_TILE, pad_h), F32)], -1)
    sin_i = jnp.zeros((TOKEN_TILE, HEAD_PAD), F32)
    return jnp.concatenate([cos_i, cos_l], 0), jnp.concatenate([sin_i, sin_l], 0)


def _gla_consts():
    C = GLA_CHUNK
    r = np.arange(C)
    trif = (r[None, :] <= r[:, None]).astype(np.float32)
    trib = (r[None, :] >= r[:, None]).astype(np.float32)
    h = np.arange(GLA_QUAD) // GLA_DK
    ones_bd = (h[:, None] == h[None, :]).astype(np.float32)
    return jnp.asarray(trif, BF16), jnp.asarray(trib, BF16), jnp.asarray(ones_bd, BF16)


def kernel(x_prompt, x_sample, cache_ckv, cache_krope, state_gla, c, c_ctx, w_mod, b_mod, g_norm1, g_norm2, w_in, g_q, g_kv, w_uq, w_uk, w_uv, w_o_mla, w_gla_gate, b_gla_gate, g_gla, w_o_gla, w_out, w_router, w_e_gate, w_e_up, w_e_down, g_final):
    B, N, D = x_prompt.shape
    DB, DN, _ = x_sample.shape
    L = w_in.shape[0]
    Tc, Tl = B * N, DB * DN
    assert Tc == Tl and Tc % DN == 0 and N % ROW_TILE == 0 and DN % TOKEN_TILE == 0 and Tc % TOKEN_TILE == 0
    assert N % GLA_CHUNK == 0 and DN % GLA_CHUNK == 0 and DN % GRID_W == 0 and TOKEN_TILE == ROW_TILE
    T = Tc
    G = 2
    cap = max(1, CAPACITY_FACTOR * T // N_EXPERTS)
    win = LANES if cap % LANES == 0 else cap
    assert cap % win == 0 and win % SUBLANES == 0 and 1 + DB <= SUBLANES

    def tile_maps(rows):
        nct, per_seq = Tc // rows, DN // rows
        cond = lambda i: jnp.where(i < nct, 0, 1 + (i - nct) // per_seq)
        tab = lambda i: jnp.where(i < nct, 0, 1 + (i - nct) % per_seq)
        return cond, tab

    cond_tok, tab_tok = tile_maps(TOKEN_TILE)
    cond_row, _ = tile_maps(ROW_TILE)
    n_ctx_tiles = Tc // TOKEN_TILE

    cvec = jnp.concatenate([c_ctx[None, :], c, jnp.zeros((SUBLANES - 1 - DB, D), F32)], axis=0)
    mod = _adaln_all(cvec, w_mod, b_mod)

    wp, wgate, wuq, wuk, wuv, wg, bg = _pack_weights(w_in, w_uq, w_uk, w_uv, w_gla_gate, b_gla_gate)
    cos_t, sin_t = _rope_tables(DN)
    gla_consts = _gla_consts()
    wom = w_o_mla.astype(BF16)
    wog = w_o_gla.astype(BF16)
    wout = w_out.astype(BF16)
    wr = jnp.pad(w_router, ((0, 0), (0, 0), (0, LANES - N_EXPERTS)))
    g1s, g2s, gqs, gkvs = g_norm1[:, None, :], g_norm2[:, None, :], g_q[:, None, :], g_kv[:, None, :]

    ckr_pad = jnp.pad(cache_krope, ((0, 0), (0, 0), (0, 0), (MLA_NOPE, HEAD_PAD - MLA_NOPE - MLA_ROPE)))
    kc_all, vc_all = _cache_kv(cache_ckv, ckr_pad, wuk, wuv)
    st_t = jnp.swapaxes(state_gla, -1, -2)

    x = jnp.concatenate([x_prompt.reshape(Tc, D), x_sample.reshape(Tl, D)], axis=0)
    ckv_list, kr_list, gla_list = [], [], []
    for l in range(L):
        pre = _inproj(x, mod, l, g1s, wp, gqs, gkvs, wuq, wuk, wuv, wg, bg, cos_t, sin_t, cond_tok, tab_tok)
        ckv_list.append(pre["ckv"][:Tc].reshape(B, N, KV_RANK))
        kr_list.append(pre["kr"][:Tc, MLA_NOPE:MLA_NOPE + MLA_ROPE].reshape(B, N, MLA_ROPE))

        o_ctx = _attention_ctx(pre["q"], pre["k"], pre["v"], B, N)
        o_lat = _attention_lat(pre["q"], pre["k"], pre["v"], kc_all, vc_all, l, Tc, DB, DN, min(DN, TOKEN_TILE))

        gg = g_gla[l][None]
        og_ctx, s_fin = _gla(pre["gq"], pre["gk"], pre["gv"], pre["g"], pre["sg"], gg, gla_consts, 0, B, N,
                             want_final=True)
        (og_lat,) = _gla(pre["gq"], pre["gk"], pre["gv"], pre["g"], pre["sg"], gg, gla_consts, Tc, DB, DN,
                         s0=(st_t, l))
        gla_list.append(jnp.swapaxes(s_fin, -1, -2))

        x1, h2, aff = _merge(x, mod, l, o_ctx, o_lat, og_ctx, og_lat, g1s, wgate, wom, wog, wout, g2s, wr,
                             cond_tok, n_ctx_tiles)

        aff_t = jnp.swapaxes(aff[:, :N_EXPERTS].reshape(G, T, N_EXPERTS), 1, 2)
        slot, cb = _topk(aff_t, cap)
        cb_flat = cb.reshape(-1)
        slot5 = slot.reshape(G, N_EXPERTS, T // ROW_TILE, 1, ROW_TILE)
        slot_t = jnp.pad(jnp.swapaxes(slot, 1, 2).reshape(G * T, N_EXPERTS),
                         ((0, 0), (0, LANES - N_EXPERTS)), constant_values=-1)
        aff5 = aff_t.reshape(G, N_EXPERTS, T // ROW_TILE, 1, ROW_TILE)
        y = _experts(cb_flat, slot5, aff5, h2, w_e_gate, w_e_up, w_e_down, l, cap, win)
        x = _combine(cb_flat, x1, mod[l], slot_t, y, g_final[None], cond_row, cap, win, final=(l == L - 1))

    y_prompt = x[:Tc].reshape(B, N, D)
    y_sample = x[Tc:].reshape(DB, DN, D)
    new_ckv = jnp.stack(ckv_list, axis=1)
    new_krope = jnp.stack(kr_list, axis=1)
    new_gla = jnp.stack(gla_list, axis=1)
    return (y_prompt, y_sample, new_ckv, new_krope, new_gla)
```

```python
import functools

import jax
import jax.numpy as jnp
import numpy as np
from jax import lax
from jax.experimental import pallas as pl
from jax.experimental.pallas import tpu as pltpu

F32 = jnp.float32
BF16 = jnp.bfloat16
I32 = jnp.int32

GRID_W = 64
EPS = 1e-6
MLA_HEADS = 8
MLA_NOPE = 64
MLA_ROPE = 32
MLA_V = 64
Q_RANK = 384
KV_RANK = 256
ROPE_BASE = 10000.0
GLA_HEADS = 8
GLA_DK = 64
GLA_DV = 64
GLA_W = GLA_HEADS * GLA_DK
GLA_GATE_RANK = 16
GLA_TAU = 16.0
GLA_CHUNK = 64
N_EXPERTS = 16
CAPACITY_FACTOR = 2

LANES = 128
SUBLANES = 8
BF16_ROWS = 16
HEAD_PAD = 128
VMEM_LIMIT = 56 * 1024 * 1024

ROW_TILE = 256
TOKEN_TILE = 256
GLA_QUAD = 4 * GLA_DK
GLA_DIAG = 8
GLA_SEQS_PER_STEP = 2
GLA_MILD_LOG2 = 64.0
F32_MIN_NORMAL = 2.0 ** -126
TOPK_EXP_BITS = 7
TOPK_BISECT_STEPS = 52
GLA_DK_SHIFT = GLA_DK.bit_length() - 1
assert 1 << GLA_DK_SHIFT == GLA_DK
LOG2E = float(np.log2(np.e))
Q_PRESCALE = (MLA_NOPE + MLA_ROPE) ** -0.5 * LOG2E

_SEG = {}
_off = 0
for _name, _width in (("pq", Q_RANK), ("pkv", KV_RANK), ("kr", HEAD_PAD), ("krs", HEAD_PAD),
                      ("gq", GLA_W), ("gk", GLA_W), ("gv", GLA_W), ("glr", LANES), ("gog", GLA_W)):
    _SEG[_name] = (_off, _width)
    _off += _width


def _cparams(sem, vmem=VMEM_LIMIT):
    return pltpu.CompilerParams(dimension_semantics=sem, vmem_limit_bytes=vmem)


def _dot(a, b):
    return jnp.dot(a, b, preferred_element_type=F32)


def _dot_nt(a, b):
    return lax.dot_general(a, b, (((1,), (1,)), ((), ())), preferred_element_type=F32)


def _dot_tn(a, b):
    return lax.dot_general(a, b, (((0,), (0,)), ((), ())), preferred_element_type=F32)


def _rms(x, g):
    return x * lax.rsqrt(jnp.mean(x * x, axis=-1, keepdims=True) + EPS) * g


def _modulated_norm(x, g, scale, shift):
    return _rms(x, g) * (1.0 + scale) + shift


def _sigmoid(x):
    return 0.5 * jnp.tanh(0.5 * x) + 0.5


def _split3(x):
    a = x.astype(BF16)
    r = x - a.astype(F32)
    b = r.astype(BF16)
    c = (r - b.astype(F32)).astype(BF16)
    return a, b, c


def _mod_body(c_ref, w_ref, b_ref, o_ref):
    c = c_ref[...]
    s = (c * _sigmoid(c)).astype(BF16)
    o_ref[0] = _dot(s, w_ref[0].astype(BF16)) + b_ref[0]


def _adaln_all(cvec, w_mod, b_mod):
    L, D, D6 = w_mod.shape
    R = cvec.shape[0]
    tn = 1536
    return pl.pallas_call(
        _mod_body,
        grid=(L, D6 // tn),
        in_specs=[pl.BlockSpec((R, D), lambda l, j: (0, 0)),
                  pl.BlockSpec((1, D, tn), lambda l, j: (l, 0, j)),
                  pl.BlockSpec((1, 1, tn), lambda l, j: (l, 0, j))],
        out_specs=pl.BlockSpec((1, R, tn), lambda l, j: (l, 0, j)),
        out_shape=jax.ShapeDtypeStruct((L, R, D6), F32),
        compiler_params=_cparams(("arbitrary", "arbitrary")),
        name="adaln_mod",
    )(cvec, w_mod, b_mod.reshape(L, 1, D6))


def _inproj_body(x_ref, mod_ref, g1_ref, w_ref, gq_ref, gkv_ref, wuq_ref, wuk_ref, wuv_ref,
                 wg_ref, bg_ref, ct_ref, st_ref,
                 q_ref, k_ref, v_ref, ckv_ref, kr_ref, gqo_ref, gko_ref, gvo_ref, g_ref,
                 sg_ref, *, D):
    mod = mod_ref[0]
    h = _modulated_norm(x_ref[...], g1_ref[...], mod[:, D:2 * D], mod[:, 0:D]).astype(BF16)

    def seg(name):
        a, w = _SEG[name]
        return _dot(h, w_ref[:, a:a + w])

    cos = ct_ref[...]
    sin = st_ref[...]
    nq = MLA_HEADS * HEAD_PAD

    cq = _rms(seg("pq"), gq_ref[...]).astype(BF16)
    qq = _dot(cq, wuq_ref[...])
    for hd in range(MLA_HEADS):
        a = hd * HEAD_PAD
        q_ref[:, a:a + HEAD_PAD] = ((qq[:, a:a + HEAD_PAD] * cos
                                     + qq[:, nq + a:nq + a + HEAD_PAD] * sin) * Q_PRESCALE).astype(BF16)

    ckv = _rms(seg("pkv"), gkv_ref[...])
    ckv_ref[...] = ckv
    ckv_b = ckv.astype(BF16)
    kr = seg("kr")
    kr_ref[...] = kr
    kr_rot = kr * cos + seg("krs") * sin
    kn = _dot(ckv_b, wuk_ref[...])
    for hd in range(MLA_HEADS):
        a = hd * HEAD_PAD
        k_ref[:, a:a + HEAD_PAD] = (kn[:, a:a + HEAD_PAD] + kr_rot).astype(BF16)
    v_ref[...] = _dot(ckv_b, wuv_ref[...]).astype(BF16)

    gqo_ref[...] = (seg("gq") * (GLA_DK ** -0.5)).astype(BF16)
    gko_ref[...] = seg("gk").astype(BF16)
    gvo_ref[...] = seg("gv").astype(BF16)
    logit = _dot(seg("glr").astype(BF16), wg_ref[...]) + bg_ref[...]
    g_ref[...] = (jnp.minimum(logit, 0.0) - jnp.log1p(jnp.exp(-jnp.abs(logit)))) * (1.0 / GLA_TAU)
    gog = seg("gog")
    sg_ref[...] = (gog * _sigmoid(gog)).astype(BF16)


def _layer_spec(a, layer):
    nd = a.ndim - 1
    return pl.BlockSpec((None,) + a.shape[1:], lambda i: (layer,) + (0,) * nd, pipeline_mode=pl.Buffered(1))


def _mod_spec(R, D, layer, cond_of_tile):
    return pl.BlockSpec((1, 1, 6 * D), lambda i: (layer * R + cond_of_tile(i), 0, 0))


def _inproj(x, mod, layer, g1, wp, gq, gkv, wuq, wuk, wuv, wg, bg, cos_t, sin_t, cond_of_tile, tab_of_tile):
    T2, D = x.shape
    nt = T2 // TOKEN_TILE
    L, R = mod.shape[:2]
    row = lambda w: pl.BlockSpec((TOKEN_TILE, w), lambda i: (i, 0))
    outs = [("q", MLA_HEADS * HEAD_PAD, BF16), ("k", MLA_HEADS * HEAD_PAD, BF16),
            ("v", MLA_HEADS * MLA_V, BF16), ("ckv", KV_RANK, F32), ("kr", HEAD_PAD, F32),
            ("gq", GLA_W, BF16), ("gk", GLA_W, BF16), ("gv", GLA_W, BF16), ("g", 2 * GLA_W, F32),
            ("sg", GLA_W, BF16)]
    params = [g1, wp, gq, gkv, wuq, wuk, wuv, wg, bg]
    res = pl.pallas_call(
        functools.partial(_inproj_body, D=D),
        grid=(nt,),
        in_specs=[row(D), _mod_spec(R, D, layer, cond_of_tile)]
                 + [_layer_spec(a, layer) for a in params]
                 + [pl.BlockSpec((TOKEN_TILE, HEAD_PAD), lambda i: (tab_of_tile(i), 0)),
                    pl.BlockSpec((TOKEN_TILE, HEAD_PAD), lambda i: (tab_of_tile(i), 0))],
        out_specs=[row(w) for _, w, _ in outs],
        out_shape=[jax.ShapeDtypeStruct((T2, w), dt) for _, w, dt in outs],
        compiler_params=_cparams(("arbitrary",)),
        name="inproj",
    )(x, mod.reshape(L * R, 1, 6 * D), *params, cos_t, sin_t)
    return dict(zip([n for n, _, _ in outs], res))


def _cache_kv_body(ckv_ref, krp_ref, wuk_ref, wuv_ref, k_ref, v_ref):
    c = ckv_ref[...].astype(BF16)
    kn = _dot(c, wuk_ref[...])
    krp = krp_ref[...]
    for hd in range(MLA_HEADS):
        a = hd * HEAD_PAD
        k_ref[:, a:a + HEAD_PAD] = (kn[:, a:a + HEAD_PAD] + krp).astype(BF16)
    v_ref[...] = _dot(c, wuv_ref[...]).astype(BF16)


def _cache_kv(cache_ckv, cache_kr_pad, wuk, wuv):
    DB, L, P, R = cache_ckv.shape
    nk = MLA_HEADS * HEAD_PAD
    nv = MLA_HEADS * MLA_V
    return pl.pallas_call(
        _cache_kv_body,
        grid=(L, DB),
        in_specs=[pl.BlockSpec((None, None, P, R), lambda l, b: (b, l, 0, 0)),
                  pl.BlockSpec((None, None, P, HEAD_PAD), lambda l, b: (b, l, 0, 0)),
                  pl.BlockSpec((None, R, nk), lambda l, b: (l, 0, 0)),
                  pl.BlockSpec((None, R, nv), lambda l, b: (l, 0, 0))],
        out_specs=[pl.BlockSpec((None, None, P, nk), lambda l, b: (l, b, 0, 0)),
                   pl.BlockSpec((None, None, P, nv), lambda l, b: (l, b, 0, 0))],
        out_shape=[jax.ShapeDtypeStruct((L, DB, P, nk), BF16),
                   jax.ShapeDtypeStruct((L, DB, P, nv), BF16)],
        compiler_params=_cparams(("arbitrary", "arbitrary")),
        name="cache_kv",
    )(cache_ckv, cache_kr_pad, wuk, wuv)


def _attn_body(*refs, nseg):
    q_ref = refs[0]
    k_refs = refs[1:1 + nseg]
    v_refs = refs[1 + nseg:1 + 2 * nseg]
    o_ref = refs[1 + 2 * nseg]
    lane = lax.broadcasted_iota(I32, (1, 2 * MLA_V), 1)
    outs = []
    for hh in range(2):
        qh = q_ref[:, hh * HEAD_PAD:(hh + 1) * HEAD_PAD]
        s = [_dot_nt(qh, kr[:, hh * HEAD_PAD:(hh + 1) * HEAD_PAD]) for kr in k_refs]
        m = s[0].max(axis=-1, keepdims=True)
        for sj in s[1:]:
            m = jnp.maximum(m, sj.max(axis=-1, keepdims=True))
        p = [jnp.exp2(sj - m) for sj in s]
        den = p[0].sum(axis=-1, keepdims=True)
        for pj in p[1:]:
            den = den + pj.sum(axis=-1, keepdims=True)
        o = None
        for pj, vr in zip(p, v_refs):
            t = _dot(pj.astype(BF16), vr[...])
            o = t if o is None else o + t
        outs.append(o * (1.0 / den))
    o_ref[...] = jnp.where(lane < MLA_V, outs[0], outs[1]).astype(BF16)


def _attention_ctx(q, k, v, nseq, n):
    hp = MLA_HEADS // 2
    return pl.pallas_call(
        functools.partial(_attn_body, nseg=1),
        grid=(nseq, hp),
        in_specs=[pl.BlockSpec((n, 2 * HEAD_PAD), lambda b, h: (b, h)),
                  pl.BlockSpec((n, 2 * HEAD_PAD), lambda b, h: (b, h)),
                  pl.BlockSpec((n, 2 * MLA_V), lambda b, h: (b, h))],
        out_specs=pl.BlockSpec((n, 2 * MLA_V), lambda b, h: (b, h)),
        out_shape=jax.ShapeDtypeStruct((nseq * n, MLA_HEADS * MLA_V), BF16),
        compiler_params=_cparams(("arbitrary", "arbitrary")),
        name="attn_ctx",
    )(q, k, v)


def _attention_lat(q, k, v, kc, vc, layer, row0, nseq, n, tq):
    hp = MLA_HEADS // 2
    P = kc.shape[2]
    qt = n // tq
    q0 = row0 // tq
    s0 = row0 // n
    return pl.pallas_call(
        functools.partial(_attn_body, nseg=2),
        grid=(nseq, hp, qt),
        in_specs=[pl.BlockSpec((tq, 2 * HEAD_PAD), lambda b, h, t: (q0 + b * qt + t, h)),
                  pl.BlockSpec((None, None, P, 2 * HEAD_PAD), lambda b, h, t: (layer, b, 0, h)),
                  pl.BlockSpec((n, 2 * HEAD_PAD), lambda b, h, t: (s0 + b, h)),
                  pl.BlockSpec((None, None, P, 2 * MLA_V), lambda b, h, t: (layer, b, 0, h)),
                  pl.BlockSpec((n, 2 * MLA_V), lambda b, h, t: (s0 + b, h))],
        out_specs=pl.BlockSpec((tq, 2 * MLA_V), lambda b, h, t: (b * qt + t, h)),
        out_shape=jax.ShapeDtypeStruct((nseq * n, MLA_HEADS * MLA_V), BF16),
        compiler_params=_cparams(("arbitrary", "arbitrary", "arbitrary")),
        name="attn_lat",
    )(q, kc, k, vc, v)


def _head_sums(x, ones_quad):
    return jnp.concatenate([_dot(x[:, a:a + GLA_QUAD], ones_quad) for a in range(0, GLA_W, GLA_QUAD)], axis=1)


def _gla_chunk(q, k, v, g, st_refs, fwd, tri, ones_bd, mild):
    C = GLA_CHUNK
    W = GLA_W
    nquad = W // GLA_QUAD
    rows = lax.broadcasted_iota(I32, (C, 1), 0)

    g1, g2, g3 = _split3(g * LOG2E)
    cum = _dot(tri, jnp.concatenate([g1, g2, g3], axis=1))
    cum = cum[:, 0:W] + cum[:, W:2 * W] + cum[:, 2 * W:3 * W]
    edge = C - 1 if fwd else 0
    last = cum[edge:edge + 1]
    q_in = (q * jnp.exp2(cum)).astype(BF16)
    k_st = (k * jnp.exp2(last - cum)).astype(BF16)
    v_b = v.astype(BF16)

    lane_q = lax.broadcasted_iota(I32, (1, GLA_QUAD), 1)
    head_masks = [jnp.right_shift(lane_q, GLA_DK_SHIFT) == h for h in range(GLA_QUAD // GLA_DK)]
    col_s = jnp.bitwise_and(lane_q, C - 1)

    def stack_heads(xq):
        return jnp.concatenate([jnp.where(mh, xq, jnp.zeros_like(xq)) for mh in head_masks], axis=0)

    a_acc = [jnp.zeros((C, GLA_QUAD), F32) for _ in range(nquad)]
    half = C // 2
    while half >= GLA_DIAG:
        blk = 2 * half
        pieces = []
        for p in range(C // blk):
            rr = p * blk + (half - 1 if fwd else half)
            pieces.append(jnp.broadcast_to(cum[rr:rr + 1], (blk, W)))
        ref = jnp.concatenate(pieces, axis=0) if len(pieces) > 1 else pieces[0]
        upper = jnp.bitwise_and(rows, blk - 1) >= half
        qmask = upper if fwd else jnp.logical_not(upper)
        qe = jnp.where(qmask, q * jnp.exp2(cum - ref), 0.0).astype(BF16)
        ke = jnp.where(qmask, 0.0, k * jnp.exp2(ref - cum)).astype(BF16)
        sh = blk.bit_length() - 1
        same = jnp.right_shift(rows, sh) == jnp.right_shift(col_s, sh)
        for qd in range(nquad):
            sl = slice(qd * GLA_QUAD, (qd + 1) * GLA_QUAD)
            a = _dot_nt(qe[:, sl], stack_heads(ke[:, sl]))
            a_acc[qd] = a_acc[qd] + jnp.where(same, a, 0.0)
        half //= 2

    if mild:
        nb = C // GLA_DIAG
        c3 = cum.reshape(nb, GLA_DIAG, W)
        er = 0 if fwd else GLA_DIAG - 1
        ref = jnp.broadcast_to(c3[:, er:er + 1, :], (nb, GLA_DIAG, W)).reshape(C, W)
        qe = (q * jnp.exp2(cum - ref)).astype(BF16)
        ke = (k * jnp.exp2(ref - cum)).astype(BF16)
        sh = GLA_DIAG.bit_length() - 1
        keep = jnp.right_shift(rows, sh) == jnp.right_shift(col_s, sh)
        keep = jnp.logical_and(keep, (rows >= col_s) if fwd else (rows <= col_s))
        for qd in range(nquad):
            sl = slice(qd * GLA_QUAD, (qd + 1) * GLA_QUAD)
            a = _dot_nt(qe[:, sl], stack_heads(ke[:, sl]))
            a_acc[qd] = a_acc[qd] + jnp.where(keep, a, 0.0)

    r2 =jnp.right_shift(lax.broadcasted_iota(I32, (GLA_QUAD, GLA_QUAD), 0), GLA_DK_SHIFT)
    c2 = jnp.right_shift(lax.broadcasted_iota(I32, (GLA_QUAD, GLA_QUAD), 1), GLA_DK_SHIFT)
    o_parts = []
    for qd in range(nquad):
        sl = slice(qd * GLA_QUAD, (qd + 1) * GLA_QUAD)
        st = st_refs[qd][...]
        o = _dot_nt(q_in[:, sl], st.astype(BF16))
        o = o + _dot(a_acc[qd].astype(BF16), stack_heads(v_b[:, sl]))
        o_parts.append(o)
        upd = _dot_tn(v_b[:, sl], k_st[:, sl])
        st_refs[qd][...] = st * jnp.exp2(last[:, sl]) + jnp.where(r2 == c2, upd, 0.0)
    o = jnp.concatenate(o_parts, axis=1)
    if mild:
        return o

    nb = C // GLA_DIAG
    q3 = q.reshape(nb, GLA_DIAG, W)
    k3 = k.reshape(nb, GLA_DIAG, W)
    v3 = v.reshape(nb, GLA_DIAG, W)
    c3 = cum.reshape(nb, GLA_DIAG, W)
    tl = lax.broadcasted_iota(I32, (1, GLA_DIAG, 1), 1)
    zs = []
    for s in range(GLA_DIAG):
        e = jnp.exp2(c3 - c3[:, s:s + 1, :])
        valid = (tl >= s) if fwd else (tl <= s)
        zs.append(jnp.where(valid, q3 * e * k3[:, s:s + 1, :], 0.0).reshape(C, W).astype(BF16))
    w_all = _head_sums(jnp.concatenate(zs, axis=0), ones_bd)
    od = jnp.zeros((nb, GLA_DIAG, W), F32)
    for s in range(GLA_DIAG):
        od = od + w_all[s * C:(s + 1) * C].reshape(nb, GLA_DIAG, W) * v3[:, s:s + 1, :]
    return o + od.reshape(C, W)


def _gla_body(*refs, n, spb, has_init, has_final):
    it = iter(refs)
    q_ref, k_ref, v_ref, g_ref, sg_ref, gg_ref, trif_ref, trib_ref, ones_ref = (next(it) for _ in range(9))
    s0_ref = next(it) if has_init else None
    o_ref = next(it)
    sf_ref = next(it) if has_final else None
    of_ref, ob_ref = next(it), next(it)
    nquad = GLA_W // GLA_QUAD
    st = [[[next(it) for _ in range(nquad)] for _ in range(2)] for _ in range(spb)]

    C = GLA_CHUNK
    nc = n // C
    hq = GLA_QUAD // GLA_DK
    for sq in range(spb):
        for d in range(2):
            for qd in range(nquad):
                st[sq][d][qd][...] = jnp.zeros((GLA_QUAD, GLA_QUAD), F32)
                if has_init:
                    for h in range(hq):
                        hs = slice(h * GLA_DK, (h + 1) * GLA_DK)
                        st[sq][d][qd][hs, hs] = s0_ref[sq, d, qd * hq + h]

    trif = trif_ref[...]
    trib = trib_ref[...]
    ones_bd = ones_ref[...]

    def step(i, carry, mild):
        for sq in range(spb):
            for d, (tri, acc) in enumerate(((trif, of_ref), (trib, ob_ref))):
                c = i if d == 0 else nc - 1 - i
                rs = pl.ds(pl.multiple_of(sq * n + c * C, C), C)
                gd = g_ref[rs, d * GLA_W:(d + 1) * GLA_W]
                qkv = [r[rs, :].astype(F32) for r in (q_ref, k_ref, v_ref)]
                acc[rs, :] = _gla_chunk(*qkv, gd, st[sq][d], d == 0, tri, ones_bd, mild)
        return carry

    steepest = jnp.max(-g_ref[...]) * (LOG2E * (GLA_DIAG - 1))
    is_mild = steepest < GLA_MILD_LOG2

    @pl.when(is_mild)
    def _():
        lax.fori_loop(0, nc, functools.partial(step, mild=True), 0)

    @pl.when(jnp.logical_not(is_mild))
    def _():
        lax.fori_loop(0, nc, functools.partial(step, mild=False), 0)

    gg = gg_ref[...]
    fr = min(n, 256)

    def fin(i, carry):
        rs = pl.ds(pl.multiple_of(i * fr, fr), fr)
        o = of_ref[rs, :] + ob_ref[rs, :]
        ms = _head_sums(jnp.concatenate(_split3(o * o)[:2], axis=0), ones_bd)
        ms = (ms[0:fr] + ms[fr:2 * fr]) * (1.0 / GLA_DV)
        o_ref[rs, :] = (o * lax.rsqrt(ms + EPS) * gg * sg_ref[rs, :].astype(F32)).astype(BF16)
        return carry

    lax.fori_loop(0, spb * n // fr, fin, 0)

    if has_final:
        for sq in range(spb):
            for d in range(2):
                for qd in range(nquad):
                    s = st[sq][d][qd][...]
                    for h in range(hq):
                        sf_ref[sq, d, qd * hq + h] = s[h * GLA_DK:(h + 1) * GLA_DK, h * GLA_DK:(h + 1) * GLA_DK]


def _gla(gq, gk, gv, g, sg, g_gla, consts, row0, nseq, n, s0=None, want_final=False):
    trif, trib, ones_bd = consts
    spb = GLA_SEQS_PER_STEP if nseq % GLA_SEQS_PER_STEP == 0 and (row0 // n) % GLA_SEQS_PER_STEP == 0 else 1
    b0 = row0 // (n * spb)
    W = GLA_W
    nquad = W // GLA_QUAD
    seq = lambda w: pl.BlockSpec((spb * n, w), lambda b: (b0 + b, 0))
    full = lambda a: pl.BlockSpec(a.shape, lambda b: (0,) * a.ndim)
    in_specs = [seq(W), seq(W), seq(W), seq(2 * W), seq(W), full(g_gla), full(trif), full(trib), full(ones_bd)]
    args = [gq, gk, gv, g, sg, g_gla, trif, trib, ones_bd]
    if s0 is not None:
        s0_arr, layer = s0
        in_specs.append(pl.BlockSpec((spb, None, 2, GLA_HEADS, GLA_DV, GLA_DK), lambda b: (b, layer, 0, 0, 0, 0)))
        args.append(s0_arr)
    out_specs = [pl.BlockSpec((spb * n, W), lambda b: (b, 0))]
    out_shape = [jax.ShapeDtypeStruct((nseq * n, W), BF16)]
    if want_final:
        out_specs.append(pl.BlockSpec((spb, 2, GLA_HEADS, GLA_DV, GLA_DK), lambda b: (b, 0, 0, 0, 0)))
        out_shape.append(jax.ShapeDtypeStruct((nseq, 2, GLA_HEADS, GLA_DV, GLA_DK), F32))
    scratch = [pltpu.VMEM((spb * n, W), F32), pltpu.VMEM((spb * n, W), F32)]
    scratch += [pltpu.VMEM((GLA_QUAD, GLA_QUAD), F32) for _ in range(spb * 2 * nquad)]
    res = pl.pallas_call(
        functools.partial(_gla_body, n=n, spb=spb, has_init=s0 is not None, has_final=want_final),
        grid=(nseq // spb,),
        in_specs=in_specs,
        out_specs=out_specs,
        out_shape=out_shape,
        scratch_shapes=scratch,
        compiler_params=_cparams(("arbitrary",)),
        name="gla_lat" if s0 is not None else "gla_ctx",
    )(*args)
    return res


def _merge_body(x_ref, mod_ref, oc_ref, ol_ref, gc_ref, gl_ref, g1_ref, wgate_ref, wom_ref, wog_ref, wout_ref,
                g2_ref, wr_ref, x1_ref, h2_ref, aff_ref, *, D, n_ctx_tiles):
    is_ctx = pl.program_id(0) < n_ctx_tiles
    mod = mod_ref[0]
    x = x_ref[...]
    h = _modulated_norm(x, g1_ref[...], mod[:, D:2 * D], mod[:, 0:D]).astype(BF16)
    om = _dot(jnp.where(is_ctx, oc_ref[...], ol_ref[...]), wom_ref[...])
    merged = _sigmoid(_dot(h, wgate_ref[:, 0:D])) * om
    og = _dot(jnp.where(is_ctx, gc_ref[...], gl_ref[...]), wog_ref[...])
    merged = (merged + _sigmoid(_dot(h, wgate_ref[:, D:2 * D])) * og).astype(BF16)
    mix = _dot(merged, wout_ref[...])
    x1 = x + mod[:, 2 * D:3 * D] * mix
    x1_ref[...] = x1
    h2 = _modulated_norm(x1, g2_ref[...], mod[:, 4 * D:5 * D], mod[:, 3 * D:4 * D])
    h2_ref[...] = h2.astype(BF16)
    a, b, c = _split3(h2)
    wa, wb, wc = _split3(wr_ref[...])
    logits = (_dot(a, wa) + _dot(a, wb) + _dot(b, wa)) + (_dot(a, wc) + _dot(b, wb) + _dot(c, wa))
    lane = lax.broadcasted_iota(I32, logits.shape, 1)
    logits = jnp.where(lane < N_EXPERTS, logits, -jnp.inf)
    p = jnp.exp(logits - logits.max(axis=-1, keepdims=True))
    aff_ref[...] = p / p.sum(axis=-1, keepdims=True)


def _merge(x, mod, layer, o_ctx, o_lat, og_ctx, og_lat, g1, wgate, wom, wog, wout, g2, wr, cond_of_tile,
           n_ctx_tiles):
    T2, D = x.shape
    nt = T2 // TOKEN_TILE
    L, R = mod.shape[:2]
    n_lat_tiles = nt - n_ctx_tiles
    row = lambda w: pl.BlockSpec((TOKEN_TILE, w), lambda i: (i, 0))
    ctx = lambda w: pl.BlockSpec((TOKEN_TILE, w), lambda i: (jnp.minimum(i, n_ctx_tiles - 1), 0))
    lat = lambda w: pl.BlockSpec((TOKEN_TILE, w), lambda i: (jnp.clip(i - n_ctx_tiles, 0, n_lat_tiles - 1), 0))
    W = o_ctx.shape[1]
    params = [g1, wgate, wom, wog, wout, g2, wr]
    return pl.pallas_call(
        functools.partial(_merge_body, D=D, n_ctx_tiles=n_ctx_tiles),
        grid=(nt,),
        in_specs=[row(D), _mod_spec(R, D, layer, cond_of_tile), ctx(W), lat(W), ctx(GLA_W), lat(GLA_W)]
                 + [_layer_spec(a, layer) for a in params],
        out_specs=[row(D), row(D), row(LANES)],
        out_shape=[jax.ShapeDtypeStruct((T2, D), F32), jax.ShapeDtypeStruct((T2, D), BF16),
                   jax.ShapeDtypeStruct((T2, LANES), F32)],
        compiler_params=_cparams(("arbitrary",)),
        name="merge_router",
    )(x, mod.reshape(L * R, 1, 6 * D), o_ctx, o_lat, og_ctx, og_lat, *params)


def _topk_body(a_ref, slot_ref, cb_ref, *, T, cap):
    a = a_ref[0]
    E = a.shape[0]

    def count_ge(thr):
        return jnp.sum((a >= thr).astype(F32), axis=1, keepdims=True)

    hi = jnp.full((E, 1), 2.0, F32)
    for j in range(TOPK_EXP_BITS - 1, -1, -1):
        cand = hi * (2.0 ** -(2 ** j))
        hi = jnp.where(count_ge(cand) < cap, cand, hi)
    lo = jnp.where(hi > F32_MIN_NORMAL, 0.5 * hi, 0.0)

    def bisect(_, lh):
        lo, hi = lh
        mid = 0.5 * (lo + hi)
        up = count_ge(mid) >= cap
        return jnp.where(up, mid, lo), jnp.where(up, hi, mid)

    lo, hi = lax.fori_loop(0, TOPK_BISECT_STEPS, bisect, (lo, hi))
    gt = a >= hi
    eq = jnp.logical_and(a >= lo, a < hi)
    need = cap - jnp.sum(gt.astype(F32), axis=1, keepdims=True)

    r = lax.broadcasted_iota(I32, (LANES, LANES), 0)
    c = lax.broadcasted_iota(I32, (LANES, LANES), 1)
    triu = (r < c).astype(BF16)
    lane = lax.broadcasted_iota(I32, (E, LANES), 1)

    nb = T // LANES
    per_tile = ROW_TILE // LANES
    carry_eq = jnp.zeros((E, 1), F32)
    carry_sel = jnp.zeros((E, 1), F32)
    cb = jnp.zeros((E, LANES), I32)
    for j in range(nb):
        sl = slice(j * LANES, (j + 1) * LANES)
        eq_j = eq[:, sl].astype(BF16)
        pre = _dot(eq_j, triu) + carry_eq
        carry_eq = carry_eq + jnp.sum(eq_j.astype(F32), axis=1, keepdims=True)
        sel = jnp.logical_or(gt[:, sl], jnp.logical_and(eq[:, sl], pre < need))
        sel_b = sel.astype(BF16)
        if j % per_tile == 0:
            cb = jnp.where(lane == j // per_tile, carry_sel.astype(I32), cb)
        slot = (_dot(sel_b, triu) + carry_sel).astype(I32)
        carry_sel = carry_sel + jnp.sum(sel_b.astype(F32), axis=1, keepdims=True)
        slot_ref[0, :, sl] = jnp.where(sel, slot, -1)
    cb_ref[0] = jnp.where(lane == nb // per_tile, carry_sel.astype(I32), cb)


def _topk(aff_t, cap):
    G, E, T = aff_t.shape
    return pl.pallas_call(
        functools.partial(_topk_body, T=T, cap=cap),
        grid=(G,),
        in_specs=[pl.BlockSpec((1, E, T), lambda g: (g, 0, 0))],
        out_specs=[pl.BlockSpec((1, E, T), lambda g: (g, 0, 0)),
                   pl.BlockSpec((1, E, LANES), lambda g: (g, 0, 0))],
        out_shape=[jax.ShapeDtypeStruct((G, E, T), I32), jax.ShapeDtypeStruct((G, E, LANES), I32)],
        compiler_params=_cparams(("arbitrary",)),
        name="expert_topk",
    )(aff_t)


def _expert_body(cb_ref, slot_ref, aff_ref, h_ref, wg_ref, wu_ref, wd_ref, y_ref, xs_ref, acc_ref, ws_ref, *,
                 cap, win, nt, nfh, E):
    g = pl.program_id(0)
    e = pl.program_id(1)
    fh = pl.program_id(2)
    base = (g * E + e) * LANES

    @pl.when(fh == 0)
    def _gather():
        xs_ref[...] = jnp.zeros_like(xs_ref)
        ws_ref[...] = jnp.zeros_like(ws_ref)

        def tile(i, carry):
            lo = cb_ref[base + i]
            hi = cb_ref[base + i + 1]
            srow = slot_ref[i]
            arow = aff_ref[i]
            for w in range(cap // win):
                @pl.when(jnp.logical_and(lo < (w + 1) * win, hi > w * win))
                def _():
                    j = lax.broadcasted_iota(I32, (win, 1), 0) + w * win
                    hit = srow == j
                    hs = h_ref[pl.ds(pl.multiple_of(i * ROW_TILE, ROW_TILE), ROW_TILE), :]
                    xs_ref[w * win:(w + 1) * win, :] += _dot(hit.astype(BF16), hs)
                    ws_ref[w * win:(w + 1) * win, :] += jnp.sum(jnp.where(hit, arow, 0.0), axis=1, keepdims=True)
            return carry

        lax.fori_loop(0, nt, tile, 0)

    xb = xs_ref[...].astype(BF16)
    gate = _dot(xb, wg_ref[...].astype(BF16))
    up = _dot(xb, wu_ref[...].astype(BF16))
    hid = (gate * _sigmoid(gate) * up).astype(BF16)
    part = _dot(hid, wd_ref[...].astype(BF16))

    @pl.when(fh == 0)
    def _():
        acc_ref[...] = part

    @pl.when(fh > 0)
    def _():
        acc_ref[...] += part

    @pl.when(fh == nfh - 1)
    def _():
        y_ref[...] = (acc_ref[...] * ws_ref[...]).astype(BF16)


def _experts(cb_flat, slot5, aff5, h2, w_gate, w_up, w_down, layer, cap, win):
    G, E, nt = slot5.shape[:3]
    T = nt * ROW_TILE
    D = h2.shape[1]
    FF = w_gate.shape[-1]
    nfh = 2
    fb = FF // nfh
    grid_spec = pltpu.PrefetchScalarGridSpec(
        num_scalar_prefetch=1,
        grid=(G, E, nfh),
        in_specs=[pl.BlockSpec((None, None, nt, 1, ROW_TILE), lambda g, e, f, cb: (g, e, 0, 0, 0)),
                  pl.BlockSpec((None, None, nt, 1, ROW_TILE), lambda g, e, f, cb: (g, e, 0, 0, 0)),
                  pl.BlockSpec((T, D), lambda g, e, f, cb: (g, 0)),
                  pl.BlockSpec((None, None, D, fb), lambda g, e, f, cb: (layer, e, 0, f)),
                  pl.BlockSpec((None, None, D, fb), lambda g, e, f, cb: (layer, e, 0, f)),
                  pl.BlockSpec((None, None, fb, D), lambda g, e, f, cb: (layer, e, f, 0))],
        out_specs=pl.BlockSpec((None, None, cap, D), lambda g, e, f, cb: (g, e, 0, 0)),
        scratch_shapes=[pltpu.VMEM((cap, D), F32), pltpu.VMEM((cap, D), F32), pltpu.VMEM((cap, 1), F32)],
    )
    return pl.pallas_call(
        functools.partial(_expert_body, cap=cap, win=win, nt=nt, nfh=nfh, E=E),
        grid_spec=grid_spec,
        out_shape=jax.ShapeDtypeStruct((G, E, cap, D), BF16),
        compiler_params=_cparams(("arbitrary", "arbitrary", "arbitrary")),
        name="expert_ffn",
    )(cb_flat, slot5, aff5, h2, w_gate, w_up, w_down)


def _combine_body(cb_ref, x_ref, mod_ref, slot_ref, y_ref, spread_ref, gf_ref, o_ref, acc_ref, *,
                  D, cap, win, cw, E, tiles_per_group, final):
    i = pl.program_id(0)
    g = i // tiles_per_group
    ti = i % tiles_per_group
    slots = slot_ref[...]
    los = [cb_ref[(g * E + e) * LANES + ti] for e in range(E)]
    his = [cb_ref[(g * E + e) * LANES + ti + 1] for e in range(E)]
    starts = [jnp.minimum(jnp.bitwise_and(lo, -BF16_ROWS), cap - cw) for lo in los]
    fits = his[0] <= starts[0] + cw
    for e in range(1, E):
        fits = jnp.logical_and(fits, his[e] <= starts[e] + cw)

    @pl.when(fits)
    def _fast():
        sp1 = slots + 1
        digits = jnp.concatenate([jnp.right_shift(sp1, 4), jnp.bitwise_and(sp1, 15)], axis=1)
        spread = _dot(digits.astype(F32).astype(BF16), spread_ref[...])
        lane = lax.broadcasted_iota(I32, (1, cw), 1)
        tgt = jnp.concatenate([lane + (starts[e] + 1) for e in range(E)], axis=1).astype(F32)
        onehot = (spread == tgt).astype(BF16)
        rows = jnp.concatenate([y_ref[e, pl.ds(pl.multiple_of(starts[e], BF16_ROWS), cw), :] for e in range(E)],
                               axis=0)
        acc_ref[...] = _dot(onehot, rows)

    @pl.when(jnp.logical_not(fits))
    def _general():
        acc_ref[...] = jnp.zeros_like(acc_ref)
        for e in range(E):
            col = slots[:, e:e + 1]
            for w in range(cap // win):
                @pl.when(jnp.logical_and(los[e] < (w + 1) * win, his[e] > w * win))
                def _():
                    j = lax.broadcasted_iota(I32, (1, win), 1) + w * win
                    oh = (col == j).astype(BF16)
                    acc_ref[...] += _dot(oh, y_ref[e, w * win:(w + 1) * win, :])

    mod = mod_ref[0]
    x2 = x_ref[...] + mod[:, 5 * D:6 * D] * acc_ref[...]
    if final:
        x2 = _rms(x2, gf_ref[...])
    o_ref[...] = x2


def _combine(cb_flat, x1, mod_l, slot_t, y, g_final, cond_of_tile, cap, win, final):
    T2, D = x1.shape
    G, E = y.shape[:2]
    nt = T2 // ROW_TILE
    tpg = nt // G
    R = mod_l.shape[0]
    cw = min(LANES, cap)
    spread = np.zeros((2 * LANES, E * cw), np.float32)
    for e in range(E):
        spread[e, e * cw:(e + 1) * cw] = 16.0
        spread[LANES + e, e * cw:(e + 1) * cw] = 1.0
    spread = jnp.asarray(spread, BF16)
    grid_spec = pltpu.PrefetchScalarGridSpec(
        num_scalar_prefetch=1,
        grid=(nt,),
        in_specs=[pl.BlockSpec((ROW_TILE, D), lambda i, cb: (i, 0)),
                  pl.BlockSpec((1, 1, 6 * D), lambda i, cb: (cond_of_tile(i), 0, 0)),
                  pl.BlockSpec((ROW_TILE, LANES), lambda i, cb: (i, 0)),
                  pl.BlockSpec((None, E, cap, D), lambda i, cb: (i // tpg, 0, 0, 0)),
                  pl.BlockSpec(spread.shape, lambda i, cb: (0, 0)),
                  pl.BlockSpec((1, D), lambda i, cb: (0, 0))],
        out_specs=pl.BlockSpec((ROW_TILE, D), lambda i, cb: (i, 0)),
        scratch_shapes=[pltpu.VMEM((ROW_TILE, D), F32)],
    )
    return pl.pallas_call(
        functools.partial(_combine_body, D=D, cap=cap, win=win, cw=cw, E=E, tiles_per_group=tpg, final=final),
        grid_spec=grid_spec,
        out_shape=jax.ShapeDtypeStruct((T2, D), F32),
        compiler_params=_cparams(("arbitrary",)),
        name="moe_combine",
    )(cb_flat, x1, mod_l.reshape(R, 1, 6 * D), slot_t, y, spread, g_final)


def _pack_weights(w_in, w_uq, w_uk, w_uv, w_gla_gate, b_gla_gate):
    L, D, _ = w_in.shape
    sizes = (Q_RANK, KV_RANK, MLA_ROPE, GLA_W, GLA_W, GLA_W, 2 * GLA_GATE_RANK, GLA_W, D, D)
    idx = np.cumsum(sizes)[:-1]
    pq, pkv, kr, gq, gk, gv, glr, gog, ga, gb = jnp.split(w_in, [int(i) for i in idx], axis=-1)
    npair = MLA_ROPE // 4
    swap = np.concatenate([np.arange(npair, 2 * npair), np.arange(0, npair),
                           np.arange(3 * npair, 4 * npair), np.arange(2 * npair, 3 * npair)])

    def slot_rope(w):
        return jnp.pad(w, ((0, 0), (0, 0), (MLA_NOPE, HEAD_PAD - MLA_NOPE - MLA_ROPE)))

    glr_p = jnp.pad(glr, ((0, 0), (0, 0), (0, LANES - 2 * GLA_GATE_RANK)))
    wp = jnp.concatenate([pq, pkv, slot_rope(kr), slot_rope(kr[..., swap]), gq, gk, gv, glr_p, gog],
                         axis=-1).astype(BF16)
    wgate = jnp.concatenate([ga, gb], axis=-1).astype(BF16)

    uq = w_uq.reshape(L, Q_RANK, MLA_HEADS, MLA_NOPE + MLA_ROPE)
    pad_h = HEAD_PAD - MLA_NOPE - MLA_ROPE
    uq_n = jnp.pad(uq, ((0, 0), (0, 0), (0, 0), (0, pad_h))).reshape(L, Q_RANK, MLA_HEADS * HEAD_PAD)
    uq_s = jnp.concatenate([jnp.zeros_like(uq[..., :MLA_NOPE]), uq[..., MLA_NOPE:][..., swap]], axis=-1)
    uq_s = jnp.pad(uq_s, ((0, 0), (0, 0), (0, 0), (0, pad_h))).reshape(L, Q_RANK, MLA_HEADS * HEAD_PAD)
    wuq = jnp.concatenate([uq_n, uq_s], axis=-1).astype(BF16)

    uk = w_uk.reshape(L, KV_RANK, MLA_HEADS, MLA_NOPE)
    wuk = jnp.pad(uk, ((0, 0), (0, 0), (0, 0), (0, HEAD_PAD - MLA_NOPE))).reshape(
        L, KV_RANK, MLA_HEADS * HEAD_PAD).astype(BF16)
    wuv = w_uv.astype(BF16)

    wg = jnp.zeros((L, LANES, 2 * GLA_W), F32)
    wg = wg.at[:, 0:GLA_GATE_RANK, 0:GLA_W].set(w_gla_gate[:, 0])
    wg = wg.at[:, GLA_GATE_RANK:2 * GLA_GATE_RANK, GLA_W:].set(w_gla_gate[:, 1])
    bg = b_gla_gate.reshape(L, 1, 2 * GLA_W)
    return wp, wgate, wuq, wuk, wuv, wg.astype(BF16), bg


def _rope_tables(n_lat):
    npair = MLA_ROPE // 4
    freqs = ROPE_BASE ** (-jnp.arange(npair, dtype=F32) / npair)
    pos = jnp.arange(n_lat)
    ang_r = (pos // GRID_W).astype(F32)[:, None] * freqs
    ang_c = (pos % GRID_W).astype(F32)[:, None] * freqs
    cr, sr, cc, sc = jnp.cos(ang_r), jnp.sin(ang_r), jnp.cos(ang_c), jnp.sin(ang_c)
    cos32 = jnp.concatenate([cr, cr, cc, cc], axis=-1)
    sin32 = jnp.concatenate([-sr, sr, -sc, sc], axis=-1)
    pad_h = HEAD_PAD - MLA_NOPE - MLA_ROPE
    ones = jnp.ones((n_lat, MLA_NOPE), F32)
    cos_l = jnp.concatenate([ones, cos32, jnp.zeros((n_lat, pad_h), F32)], axis=-1)
    sin_l = jnp.pad(sin32, ((0, 0), (MLA_NOPE, pad_h)))
    cos_i = jnp.concatenate([jnp.ones((TOKEN_TILE, MLA_NOPE + MLA_ROPE), F32),
                             jnp.zeros((TOKEN_TILE, pad_h), F32)], -1)
    sin_i = jnp.zeros((TOKEN_TILE, HEAD_PAD), F32)
    return jnp.concatenate([cos_i, cos_l], 0), jnp.concatenate([sin_i, sin_l], 0)


def _gla_consts():
    C = GLA_CHUNK
    r = np.arange(C)
    trif = (r[None, :] <= r[:, None]).astype(np.float32)
    trib = (r[None, :] >= r[:, None]).astype(np.float32)
    h = np.arange(GLA_QUAD) // GLA_DK
    ones_bd = (h[:, None] == h[None, :]).astype(np.float32)
    return jnp.asarray(trif, BF16), jnp.asarray(trib, BF16), jnp.asarray(ones_bd, BF16)


def kernel(x_prompt, x_sample, cache_ckv, cache_krope, state_gla, c, c_ctx, w_mod, b_mod, g_norm1, g_norm2, w_in, g_q, g_kv, w_uq, w_uk, w_uv, w_o_mla, w_gla_gate, b_gla_gate, g_gla, w_o_gla, w_out, w_router, w_e_gate, w_e_up, w_e_down, g_final):
    B, N, D = x_prompt.shape
    DB, DN, _ = x_sample.shape
    L = w_in.shape[0]
    Tc, Tl = B * N, DB * DN
    assert Tc == Tl and Tc % DN == 0 and N % ROW_TILE == 0 and DN % TOKEN_TILE == 0 and Tc % TOKEN_TILE == 0
    assert N % GLA_CHUNK == 0 and DN % GLA_CHUNK == 0 and DN % GRID_W == 0 and TOKEN_TILE == ROW_TILE
    T = Tc
    G = 2
    cap = max(1, CAPACITY_FACTOR * T // N_EXPERTS)
    win = LANES if cap % LANES == 0 else cap
    assert cap % win == 0 and win % SUBLANES == 0 and 1 + DB <= SUBLANES

    def tile_maps(rows):
        nct, per_seq = Tc // rows, DN // rows
        cond = lambda i: jnp.where(i < nct, 0, 1 + (i - nct) // per_seq)
        tab = lambda i: jnp.where(i < nct, 0, 1 + (i - nct) % per_seq)
        return cond, tab

    cond_tok, tab_tok = tile_maps(TOKEN_TILE)
    cond_row, _ = tile_maps(ROW_TILE)
    n_ctx_tiles = Tc // TOKEN_TILE

    cvec = jnp.concatenate([c_ctx[None, :], c, jnp.zeros((SUBLANES - 1 - DB, D), F32)], axis=0)
    mod = _adaln_all(cvec, w_mod, b_mod)

    wp, wgate, wuq, wuk, wuv, wg, bg = _pack_weights(w_in, w_uq, w_uk, w_uv, w_gla_gate, b_gla_gate)
    cos_t, sin_t = _rope_tables(DN)
    gla_consts = _gla_consts()
    wom = w_o_mla.astype(BF16)
    wog = w_o_gla.astype(BF16)
    wout = w_out.astype(BF16)
    wr = jnp.pad(w_router, ((0, 0), (0, 0), (0, LANES - N_EXPERTS)))
    g1s, g2s, gqs, gkvs = g_norm1[:, None, :], g_norm2[:, None, :], g_q[:, None, :], g_kv[:, None, :]

    ckr_pad = jnp.pad(cache_krope, ((0, 0), (0, 0), (0, 0), (MLA_NOPE, HEAD_PAD - MLA_NOPE - MLA_ROPE)))
    kc_all, vc_all = _cache_kv(cache_ckv, ckr_pad, wuk, wuv)
    st_t = jnp.swapaxes(state_gla, -1, -2)

    x = jnp.concatenate([x_prompt.reshape(Tc, D), x_sample.reshape(Tl, D)], axis=0)
    ckv_list, kr_list, gla_list = [], [], []
    for l in range(L):
        pre = _inproj(x, mod, l, g1s, wp, gqs, gkvs, wuq, wuk, wuv, wg, bg, cos_t, sin_t, cond_tok, tab_tok)
        ckv_list.append(pre["ckv"][:Tc].reshape(B, N, KV_RANK))
        kr_list.append(pre["kr"][:Tc, MLA_NOPE:MLA_NOPE + MLA_ROPE].reshape(B, N, MLA_ROPE))

        o_ctx = _attention_ctx(pre["q"], pre["k"], pre["v"], B, N)
        o_lat = _attention_lat(pre["q"], pre["k"], pre["v"], kc_all, vc_all, l, Tc, DB, DN, min(DN, TOKEN_TILE))

        gg = g_gla[l][None]
        og_ctx, s_fin = _gla(pre["gq"], pre["gk"], pre["gv"], pre["g"], pre["sg"], gg, gla_consts, 0, B, N,
                             want_final=True)
        (og_lat,) = _gla(pre["gq"], pre["gk"], pre["gv"], pre["g"], pre["sg"], gg, gla_consts, Tc, DB, DN,
                         s0=(st_t, l))
        gla_list.append(jnp.swapaxes(s_fin, -1, -2))

        x1, h2, aff = _merge(x, mod, l, o_ctx, o_lat, og_ctx, og_lat, g1s, wgate, wom, wog, wout, g2s, wr,
                             cond_tok, n_ctx_tiles)

        aff_t = jnp.swapaxes(aff[:, :N_EXPERTS].reshape(G, T, N_EXPERTS), 1, 2)
        slot, cb = _topk(aff_t, cap)
        cb_flat = cb.reshape(-1)
        slot5 = slot.reshape(G, N_EXPERTS, T // ROW_TILE, 1, ROW_TILE)
        slot_t = jnp.pad(jnp.swapaxes(slot, 1, 2).reshape(G * T, N_EXPERTS),
                         ((0, 0), (0, LANES - N_EXPERTS)), constant_values=-1)
        aff5 = aff_t.reshape(G, N_EXPERTS, T // ROW_TILE, 1, ROW_TILE)
        y = _experts(cb_flat, slot5, aff5, h2, w_e_gate, w_e_up, w_e_down, l, cap, win)
        x = _combine(cb_flat, x1, mod[l], slot_t, y, g_final[None], cond_row, cap, win, final=(l == L - 1))

    y_prompt = x[:Tc].reshape(B, N, D)
    y_sample = x[Tc:].reshape(DB, DN, D)
    new_ckv = jnp.stack(ckv_list, axis=1)
    new_krope = jnp.stack(kr_list, axis=1)
    new_gla = jnp.stack(gla_list, axis=1)
    return (y_prompt, y_sample, new_ckv, new_krope, new_gla)
```

```python
import functools

import jax
import jax.numpy as jnp
import numpy as np
from jax import lax
from jax.experimental import pallas as pl
from jax.experimental.pallas import tpu as pltpu

F32 = jnp.float32
BF16 = jnp.bfloat16
I32 = jnp.int32

GRID_W = 64
EPS = 1e-6
MLA_HEADS = 8
MLA_NOPE = 64
MLA_ROPE = 32
MLA_V = 64
Q_RANK = 384
KV_RANK = 256
ROPE_BASE = 10000.0
GLA_HEADS = 8
GLA_DK = 64
GLA_DV = 64
GLA_W = GLA_HEADS * GLA_DK
GLA_GATE_RANK = 16
GLA_TAU = 16.0
GLA_CHUNK = 64
N_EXPERTS = 16
CAPACITY_FACTOR = 2

LANES = 128
SUBLANES = 8
BF16_ROWS = 16
HEAD_PAD = 128
VMEM_LIMIT = 56 * 1024 * 1024

ROW_TILE = 256
TOKEN_TILE = 256
GLA_QUAD = 4 * GLA_DK
GLA_DIAG = 8
GLA_SEQS_PER_STEP = 2
GLA_MILD_LOG2 = 64.0
F32_MIN_NORMAL = 2.0 ** -126
TOPK_EXP_BITS = 7
TOPK_BISECT_STEPS = 52
GLA_DK_SHIFT = GLA_DK.bit_length() - 1
assert 1 << GLA_DK_SHIFT == GLA_DK
LOG2E = float(np.log2(np.e))
Q_PRESCALE = (MLA_NOPE + MLA_ROPE) ** -0.5 * LOG2E

_SEG = {}
_off = 0
for _name, _width in (("pq", Q_RANK), ("pkv", KV_RANK), ("kr", HEAD_PAD), ("krs", HEAD_PAD),
                      ("gq", GLA_W), ("gk", GLA_W), ("gv", GLA_W), ("glr", LANES), ("gog", GLA_W)):
    _SEG[_name] = (_off, _width)
    _off += _width


def _cparams(sem, vmem=VMEM_LIMIT):
    return pltpu.CompilerParams(dimension_semantics=sem, vmem_limit_bytes=vmem)


def _dot(a, b):
    return jnp.dot(a, b, preferred_element_type=F32)


def _dot_nt(a, b):
    return lax.dot_general(a, b, (((1,), (1,)), ((), ())), preferred_element_type=F32)


def _dot_tn(a, b):
    return lax.dot_general(a, b, (((0,), (0,)), ((), ())), preferred_element_type=F32)


def _rms(x, g):
    return x * lax.rsqrt(jnp.mean(x * x, axis=-1, keepdims=True) + EPS) * g


def _modulated_norm(x, g, scale, shift):
    return _rms(x, g) * (1.0 + scale) + shift


def _sigmoid(x):
    return 0.5 * jnp.tanh(0.5 * x) + 0.5


def _split3(x):
    a = x.astype(BF16)
    r = x - a.astype(F32)
    b = r.astype(BF16)
    c = (r - b.astype(F32)).astype(BF16)
    return a, b, c


def _mod_body(c_ref, w_ref, b_ref, o_ref):
    c = c_ref[...]
    s = (c * _sigmoid(c)).astype(BF16)
    o_ref[0] = _dot(s, w_ref[0].astype(BF16)) + b_ref[0]


def _adaln_all(cvec, w_mod, b_mod):
    L, D, D6 = w_mod.shape
    R = cvec.shape[0]
    tn = 1536
    return pl.pallas_call(
        _mod_body,
        grid=(L, D6 // tn),
        in_specs=[pl.BlockSpec((R, D), lambda l, j: (0, 0)),
                  pl.BlockSpec((1, D, tn), lambda l, j: (l, 0, j)),
                  pl.BlockSpec((1, 1, tn), lambda l, j: (l, 0, j))],
        out_specs=pl.BlockSpec((1, R, tn), lambda l, j: (l, 0, j)),
        out_shape=jax.ShapeDtypeStruct((L, R, D6), F32),
        compiler_params=_cparams(("arbitrary", "arbitrary")),
        name="adaln_mod",
    )(cvec, w_mod, b_mod.reshape(L, 1, D6))


def _inproj_body(xc_ref, xl_ref, mod_ref, g1_ref, w_ref, gq_ref, gkv_ref, wuq_ref, wuk_ref, wuv_ref,
                 wg_ref, bg_ref, ct_ref, st_ref,
                 q_ref, k_ref, v_ref, ckv_ref, kr_ref, gqo_ref, gko_ref, gvo_ref, g_ref,
                 sg_ref, *, D, n_ctx_tiles):
    is_ctx = pl.program_id(0) < n_ctx_tiles
    mod = mod_ref[0]
    x = jnp.where(is_ctx, xc_ref[...], xl_ref[...])
    h = _modulated_norm(x, g1_ref[...], mod[:, D:2 * D], mod[:, 0:D]).astype(BF16)

    def seg(name):
        a, w = _SEG[name]
        return _dot(h, w_ref[:, a:a + w])

    cos = ct_ref[...]
    sin = st_ref[...]
    nq = MLA_HEADS * HEAD_PAD

    cq = _rms(seg("pq"), gq_ref[...]).astype(BF16)
    qq = _dot(cq, wuq_ref[...])
    for hd in range(MLA_HEADS):
        a = hd * HEAD_PAD
        q_ref[:, a:a + HEAD_PAD] = ((qq[:, a:a + HEAD_PAD] * cos
                                     + qq[:, nq + a:nq + a + HEAD_PAD] * sin) * Q_PRESCALE).astype(BF16)

    ckv = _rms(seg("pkv"), gkv_ref[...])
    ckv_b = ckv.astype(BF16)
    kr = seg("kr")
    ckv_ref[...] = ckv
    kr_ref[...] = kr[:, MLA_NOPE:MLA_NOPE + MLA_ROPE]
    kr_rot = kr * cos + seg("krs") * sin
    kn = _dot(ckv_b, wuk_ref[...])
    for hd in range(MLA_HEADS):
        a = hd * HEAD_PAD
        k_ref[:, a:a + HEAD_PAD] = (kn[:, a:a + HEAD_PAD] + kr_rot).astype(BF16)
    v_ref[...] = _dot(ckv_b, wuv_ref[...]).astype(BF16)

    gqo_ref[...] = (seg("gq") * (GLA_DK ** -0.5)).astype(BF16)
    gko_ref[...] = seg("gk").astype(BF16)
    gvo_ref[...] = seg("gv").astype(BF16)
    logit = _dot(seg("glr").astype(BF16), wg_ref[...]) + bg_ref[...]
    g_ref[...] = (jnp.minimum(logit, 0.0) - jnp.log1p(jnp.exp(-jnp.abs(logit)))) * (1.0 / GLA_TAU)
    gog = seg("gog")
    sg_ref[...] = (gog * _sigmoid(gog)).astype(BF16)


def _layer_spec(a, layer):
    nd = a.ndim - 1
    return pl.BlockSpec((None,) + a.shape[1:], lambda i: (layer,) + (0,) * nd, pipeline_mode=pl.Buffered(1))


def _mod_spec(R, D, layer, cond_of_tile):
    return pl.BlockSpec((1, 1, 6 * D), lambda i: (layer * R + cond_of_tile(i), 0, 0))


def _ctx_lat_specs(n_ctx_tiles, n_lat_tiles, rows=TOKEN_TILE):
    ctx = lambda w: pl.BlockSpec((rows, w), lambda i, *_: (jnp.minimum(i, n_ctx_tiles - 1), 0))
    lat = lambda w: pl.BlockSpec((rows, w), lambda i, *_: (jnp.clip(i - n_ctx_tiles, 0, n_lat_tiles - 1), 0))
    return ctx, lat


def _inproj(xc, xl, mod, layer, g1, wp, gq, gkv, wuq, wuk, wuv, wg, bg, cos_t, sin_t, cond_of_tile, tab_of_tile):
    (Tc, D), Tl = xc.shape, xl.shape[0]
    T2 = Tc + Tl
    nct, nlt = Tc // TOKEN_TILE, Tl // TOKEN_TILE
    L, R = mod.shape[:2]
    row = lambda w: pl.BlockSpec((TOKEN_TILE, w), lambda i: (i, 0))
    ctx, lat = _ctx_lat_specs(nct, nlt)
    outs = [("q", MLA_HEADS * HEAD_PAD, BF16), ("k", MLA_HEADS * HEAD_PAD, BF16),
            ("v", MLA_HEADS * MLA_V, BF16), ("ckv", KV_RANK, F32), ("kr", MLA_ROPE, F32),
            ("gq", GLA_W, BF16), ("gk", GLA_W, BF16), ("gv", GLA_W, BF16), ("g", 2 * GLA_W, F32),
            ("sg", GLA_W, BF16)]
    ctx_only = ()
    params = [g1, wp, gq, gkv, wuq, wuk, wuv, wg, bg]
    res = pl.pallas_call(
        functools.partial(_inproj_body, D=D, n_ctx_tiles=nct),
        grid=(nct + nlt,),
        in_specs=[ctx(D), lat(D), _mod_spec(R, D, layer, cond_of_tile)]
                 + [_layer_spec(a, layer) for a in params]
                 + [pl.BlockSpec((TOKEN_TILE, HEAD_PAD), lambda i: (tab_of_tile(i), 0)),
                    pl.BlockSpec((TOKEN_TILE, HEAD_PAD), lambda i: (tab_of_tile(i), 0))],
        out_specs=[ctx(w) if nm in ctx_only else row(w) for nm, w, _ in outs],
        out_shape=[jax.ShapeDtypeStruct((Tc if nm in ctx_only else T2, w), dt) for nm, w, dt in outs],
        compiler_params=_cparams(("arbitrary",)),
        name="inproj",
    )(xc, xl, mod.reshape(L * R, 1, 6 * D), *params, cos_t, sin_t)
    return dict(zip([n for n, _, _ in outs], res))


def _cache_kv_body(ckv_ref, krp_ref, wuk_ref, wuv_ref, k_ref, v_ref):
    c = ckv_ref[...].astype(BF16)
    kn = _dot(c, wuk_ref[...])
    krp = krp_ref[...]
    for hd in range(MLA_HEADS):
        a = hd * HEAD_PAD
        k_ref[:, a:a + HEAD_PAD] = (kn[:, a:a + HEAD_PAD] + krp).astype(BF16)
    v_ref[...] = _dot(c, wuv_ref[...]).astype(BF16)


def _cache_kv(cache_ckv, cache_kr_pad, wuk, wuv):
    DB, L, P, R = cache_ckv.shape
    nk = MLA_HEADS * HEAD_PAD
    nv = MLA_HEADS * MLA_V
    return pl.pallas_call(
        _cache_kv_body,
        grid=(L, DB),
        in_specs=[pl.BlockSpec((None, None, P, R), lambda l, b: (b, l, 0, 0)),
                  pl.BlockSpec((None, None, P, HEAD_PAD), lambda l, b: (b, l, 0, 0)),
                  pl.BlockSpec((None, R, nk), lambda l, b: (l, 0, 0)),
                  pl.BlockSpec((None, R, nv), lambda l, b: (l, 0, 0))],
        out_specs=[pl.BlockSpec((None, None, P, nk), lambda l, b: (l, b, 0, 0)),
                   pl.BlockSpec((None, None, P, nv), lambda l, b: (l, b, 0, 0))],
        out_shape=[jax.ShapeDtypeStruct((L, DB, P, nk), BF16),
                   jax.ShapeDtypeStruct((L, DB, P, nv), BF16)],
        compiler_params=_cparams(("arbitrary", "arbitrary")),
        name="cache_kv",
    )(cache_ckv, cache_kr_pad, wuk, wuv)


def _attn_body(*refs, nseg, npairs):
    q_ref = refs[0]
    k_refs = refs[1:1 + nseg]
    v_refs = refs[1 + nseg:1 + 2 * nseg]
    o_ref = refs[1 + 2 * nseg]
    lane = lax.broadcasted_iota(I32, (1, 2 * MLA_V), 1)
    for pr in range(npairs):
        vs = slice(pr * 2 * MLA_V, (pr + 1) * 2 * MLA_V)
        outs = []
        for hh in range(2):
            hs = slice((2 * pr + hh) * HEAD_PAD, (2 * pr + hh + 1) * HEAD_PAD)
            qh = q_ref[:, hs]
            s = [_dot_nt(qh, kr[:, hs]) for kr in k_refs]
            m = s[0].max(axis=-1, keepdims=True)
            for sj in s[1:]:
                m = jnp.maximum(m, sj.max(axis=-1, keepdims=True))
            p = [jnp.exp2(sj - m) for sj in s]
            den = p[0].sum(axis=-1, keepdims=True)
            for pj in p[1:]:
                den = den + pj.sum(axis=-1, keepdims=True)
            o = None
            for pj, vr in zip(p, v_refs):
                t = _dot(pj.astype(BF16), vr[:, vs])
                o = t if o is None else o + t
            outs.append(o * (1.0 / den))
        o_ref[:, vs] = jnp.where(lane < MLA_V, outs[0], outs[1]).astype(BF16)


def _attention_ctx(q, k, v, nseq, n):
    hp = MLA_HEADS // 2
    return pl.pallas_call(
        functools.partial(_attn_body, nseg=1, npairs=hp),
        grid=(nseq,),
        in_specs=[pl.BlockSpec((n, MLA_HEADS * HEAD_PAD), lambda b: (b, 0)),
                  pl.BlockSpec((n, MLA_HEADS * HEAD_PAD), lambda b: (b, 0)),
                  pl.BlockSpec((n, MLA_HEADS * MLA_V), lambda b: (b, 0))],
        out_specs=pl.BlockSpec((n, MLA_HEADS * MLA_V), lambda b: (b, 0)),
        out_shape=jax.ShapeDtypeStruct((nseq * n, MLA_HEADS * MLA_V), BF16),
        compiler_params=_cparams(("arbitrary",)),
        name="attn_ctx",
    )(q, k, v)


def _attention_lat(q, k, v, kc, vc, layer, row0, nseq, n, tq):
    hp = MLA_HEADS // 2
    P = kc.shape[2]
    qt = n // tq
    q0 = row0 // tq
    s0 = row0 // n
    nq, nv = MLA_HEADS * HEAD_PAD, MLA_HEADS * MLA_V
    return pl.pallas_call(
        functools.partial(_attn_body, nseg=2, npairs=hp),
        grid=(nseq, qt),
        in_specs=[pl.BlockSpec((tq, nq), lambda b, t: (q0 + b * qt + t, 0)),
                  pl.BlockSpec((None, None, P, nq), lambda b, t: (layer, b, 0, 0)),
                  pl.BlockSpec((n, nq), lambda b, t: (s0 + b, 0)),
                  pl.BlockSpec((None, None, P, nv), lambda b, t: (layer, b, 0, 0)),
                  pl.BlockSpec((n, nv), lambda b, t: (s0 + b, 0))],
        out_specs=pl.BlockSpec((tq, nv), lambda b, t: (b * qt + t, 0)),
        out_shape=jax.ShapeDtypeStruct((nseq * n, nv), BF16),
        compiler_params=_cparams(("arbitrary", "arbitrary")),
        name="attn_lat",
    )(q, kc, k, vc, v)


def _head_sums(x, ones_quad):
    return jnp.concatenate([_dot(x[:, a:a + GLA_QUAD], ones_quad) for a in range(0, GLA_W, GLA_QUAD)], axis=1)


def _gla_chunk(q, k, v, g, st_refs, fwd, tri, ones_bd, mild):
    C = GLA_CHUNK
    W = GLA_W
    nquad = W // GLA_QUAD
    rows = lax.broadcasted_iota(I32, (C, 1), 0)

    g1, g2, g3 = _split3(g * LOG2E)
    cum = _dot(tri, jnp.concatenate([g1, g2, g3], axis=1))
    cum = cum[:, 0:W] + cum[:, W:2 * W] + cum[:, 2 * W:3 * W]
    edge = C - 1 if fwd else 0
    last = cum[edge:edge + 1]
    q_in = (q * jnp.exp2(cum)).astype(BF16)
    k_st = (k * jnp.exp2(last - cum)).astype(BF16)
    v_b = v.astype(BF16)

    lane_q = lax.broadcasted_iota(I32, (1, GLA_QUAD), 1)
    head_masks = [jnp.right_shift(lane_q, GLA_DK_SHIFT) == h for h in range(GLA_QUAD // GLA_DK)]
    col_s = jnp.bitwise_and(lane_q, C - 1)

    def stack_heads(xq):
        return jnp.concatenate([jnp.where(mh, xq, jnp.zeros_like(xq)) for mh in head_masks], axis=0)

    a_acc = [jnp.zeros((C, GLA_QUAD), F32) for _ in range(nquad)]
    half = C // 2
    while half >= GLA_DIAG:
        blk = 2 * half
        pieces = []
        for p in range(C // blk):
            rr = p * blk + (half - 1 if fwd else half)
            pieces.append(jnp.broadcast_to(cum[rr:rr + 1], (blk, W)))
        ref = jnp.concatenate(pieces, axis=0) if len(pieces) > 1 else pieces[0]
        upper = jnp.bitwise_and(rows, blk - 1) >= half
        qmask = upper if fwd else jnp.logical_not(upper)
        qe = jnp.where(qmask, q * jnp.exp2(cum - ref), 0.0).astype(BF16)
        ke = jnp.where(qmask, 0.0, k * jnp.exp2(ref - cum)).astype(BF16)
        sh = blk.bit_length() - 1
        same = jnp.right_shift(rows, sh) == jnp.right_shift(col_s, sh)
        for qd in range(nquad):
            sl = slice(qd * GLA_QUAD, (qd + 1) * GLA_QUAD)
            a = _dot_nt(qe[:, sl], stack_heads(ke[:, sl]))
            a_acc[qd] = a_acc[qd] + jnp.where(same, a, 0.0)
        half //= 2

    if mild:
        nb = C // GLA_DIAG
        c3 = cum.reshape(nb, GLA_DIAG, W)
        er = 0 if fwd else GLA_DIAG - 1
        ref = jnp.broadcast_to(c3[:, er:er + 1, :], (nb, GLA_DIAG, W)).reshape(C, W)
        qe = (q * jnp.exp2(cum - ref)).astype(BF16)
        ke = (k * jnp.exp2(ref - cum)).astype(BF16)
        sh = GLA_DIAG.bit_length() - 1
        keep = jnp.right_shift(rows, sh) == jnp.right_shift(col_s, sh)
        keep = jnp.logical_and(keep, (rows >= col_s) if fwd else (rows <= col_s))
        for qd in range(nquad):
            sl = slice(qd * GLA_QUAD, (qd + 1) * GLA_QUAD)
            a = _dot_nt(qe[:, sl], stack_heads(ke[:, sl]))
            a_acc[qd] = a_acc[qd] + jnp.where(keep, a, 0.0)

    r2 =jnp.right_shift(lax.broadcasted_iota(I32, (GLA_QUAD, GLA_QUAD), 0), GLA_DK_SHIFT)
    c2 = jnp.right_shift(lax.broadcasted_iota(I32, (GLA_QUAD, GLA_QUAD), 1), GLA_DK_SHIFT)
    o_parts = []
    for qd in range(nquad):
        sl = slice(qd * GLA_QUAD, (qd + 1) * GLA_QUAD)
        st = st_refs[qd][...]
        o = _dot_nt(q_in[:, sl], st.astype(BF16))
        o = o + _dot(a_acc[qd].astype(BF16), stack_heads(v_b[:, sl]))
        o_parts.append(o)
        upd = _dot_tn(v_b[:, sl], k_st[:, sl])
        st_refs[qd][...] = st * jnp.exp2(last[:, sl]) + jnp.where(r2 == c2, upd, 0.0)
    o = jnp.concatenate(o_parts, axis=1)
    if mild:
        return o

    nb = C // GLA_DIAG
    q3 = q.reshape(nb, GLA_DIAG, W)
    k3 = k.reshape(nb, GLA_DIAG, W)
    v3 = v.reshape(nb, GLA_DIAG, W)
    c3 = cum.reshape(nb, GLA_DIAG, W)
    tl = lax.broadcasted_iota(I32, (1, GLA_DIAG, 1), 1)
    zs = []
    for s in range(GLA_DIAG):
        e = jnp.exp2(c3 - c3[:, s:s + 1, :])
        valid = (tl >= s) if fwd else (tl <= s)
        zs.append(jnp.where(valid, q3 * e * k3[:, s:s + 1, :], 0.0).reshape(C, W).astype(BF16))
    w_all = _head_sums(jnp.concatenate(zs, axis=0), ones_bd)
    od = jnp.zeros((nb, GLA_DIAG, W), F32)
    for s in range(GLA_DIAG):
        od = od + w_all[s * C:(s + 1) * C].reshape(nb, GLA_DIAG, W) * v3[:, s:s + 1, :]
    return o + od.reshape(C, W)


def _gla_body(*refs, n, spb, has_init, has_final):
    it = iter(refs)
    q_ref, k_ref, v_ref, g_ref, sg_ref, gg_ref, trif_ref, trib_ref, ones_ref = (next(it) for _ in range(9))
    s0_ref = next(it) if has_init else None
    o_ref = next(it)
    sf_ref = next(it) if has_final else None
    of_ref, ob_ref = next(it), next(it)
    nquad = GLA_W // GLA_QUAD
    st = [[[next(it) for _ in range(nquad)] for _ in range(2)] for _ in range(spb)]

    C = GLA_CHUNK
    nc = n // C
    hq = GLA_QUAD // GLA_DK
    for sq in range(spb):
        for d in range(2):
            for qd in range(nquad):
                st[sq][d][qd][...] = jnp.zeros((GLA_QUAD, GLA_QUAD), F32)
                if has_init:
                    for h in range(hq):
                        hs = slice(h * GLA_DK, (h + 1) * GLA_DK)
                        st[sq][d][qd][hs, hs] = s0_ref[sq, d, qd * hq + h]

    trif = trif_ref[...]
    trib = trib_ref[...]
    ones_bd = ones_ref[...]

    def step(i, carry, mild):
        for sq in range(spb):
            for d, (tri, acc) in enumerate(((trif, of_ref), (trib, ob_ref))):
                c = i if d == 0 else nc - 1 - i
                rs = pl.ds(pl.multiple_of(sq * n + c * C, C), C)
                gd = g_ref[rs, d * GLA_W:(d + 1) * GLA_W]
                qkv = [r[rs, :].astype(F32) for r in (q_ref, k_ref, v_ref)]
                acc[rs, :] = _gla_chunk(*qkv, gd, st[sq][d], d == 0, tri, ones_bd, mild)
        return carry

    steepest = jnp.max(-g_ref[...]) * (LOG2E * (GLA_DIAG - 1))
    is_mild = steepest < GLA_MILD_LOG2

    @pl.when(is_mild)
    def _():
        lax.fori_loop(0, nc, functools.partial(step, mild=True), 0)

    @pl.when(jnp.logical_not(is_mild))
    def _():
        lax.fori_loop(0, nc, functools.partial(step, mild=False), 0)

    gg = gg_ref[...]
    fr = min(n, 256)

    def fin(i, carry):
        rs = pl.ds(pl.multiple_of(i * fr, fr), fr)
        o = of_ref[rs, :] + ob_ref[rs, :]
        ms = _head_sums(jnp.concatenate(_split3(o * o)[:2], axis=0), ones_bd)
        ms = (ms[0:fr] + ms[fr:2 * fr]) * (1.0 / GLA_DV)
        o_ref[rs, :] = (o * lax.rsqrt(ms + EPS) * gg * sg_ref[rs, :].astype(F32)).astype(BF16)
        return carry

    lax.fori_loop(0, spb * n // fr, fin, 0)

    if has_final:
        for sq in range(spb):
            for d in range(2):
                for qd in range(nquad):
                    s = st[sq][d][qd][...].T
                    for h in range(hq):
                        sf_ref[sq, d, qd * hq + h] = s[h * GLA_DK:(h + 1) * GLA_DK, h * GLA_DK:(h + 1) * GLA_DK]


def _gla(gq, gk, gv, g, sg, g_gla, consts, row0, nseq, n, s0=None, want_final=False):
    trif, trib, ones_bd = consts
    spb = GLA_SEQS_PER_STEP if nseq % GLA_SEQS_PER_STEP == 0 and (row0 // n) % GLA_SEQS_PER_STEP == 0 else 1
    b0 = row0 // (n * spb)
    W = GLA_W
    nquad = W // GLA_QUAD
    seq = lambda w: pl.BlockSpec((spb * n, w), lambda b: (b0 + b, 0))
    full = lambda a: pl.BlockSpec(a.shape, lambda b: (0,) * a.ndim)
    in_specs = [seq(W), seq(W), seq(W), seq(2 * W), seq(W), full(g_gla), full(trif), full(trib), full(ones_bd)]
    args = [gq, gk, gv, g, sg, g_gla, trif, trib, ones_bd]
    if s0 is not None:
        s0_arr, layer = s0
        in_specs.append(pl.BlockSpec((spb, None, 2, GLA_HEADS, GLA_DV, GLA_DK), lambda b: (b, layer, 0, 0, 0, 0)))
        args.append(s0_arr)
    out_specs = [pl.BlockSpec((spb * n, W), lambda b: (b, 0))]
    out_shape = [jax.ShapeDtypeStruct((nseq * n, W), BF16)]
    if want_final:
        out_specs.append(pl.BlockSpec((spb, 2, GLA_HEADS, GLA_DK, GLA_DV), lambda b: (b, 0, 0, 0, 0)))
        out_shape.append(jax.ShapeDtypeStruct((nseq, 2, GLA_HEADS, GLA_DK, GLA_DV), F32))
    scratch = [pltpu.VMEM((spb * n, W), F32), pltpu.VMEM((spb * n, W), F32)]
    scratch += [pltpu.VMEM((GLA_QUAD, GLA_QUAD), F32) for _ in range(spb * 2 * nquad)]
    res = pl.pallas_call(
        functools.partial(_gla_body, n=n, spb=spb, has_init=s0 is not None, has_final=want_final),
        grid=(nseq // spb,),
        in_specs=in_specs,
        out_specs=out_specs,
        out_shape=out_shape,
        scratch_shapes=scratch,
        compiler_params=_cparams(("arbitrary",)),
        name="gla_lat" if s0 is not None else "gla_ctx",
    )(*args)
    return res


def _merge_body(xc_ref, xl_ref, mod_ref, oc_ref, ol_ref, gc_ref, gl_ref, g1_ref, wgate_ref, wom_ref, wog_ref,
                wout_ref, g2_ref, wr_ref, x1_ref, h2_ref, aff_ref, *, D, n_ctx_tiles):
    is_ctx = pl.program_id(0) < n_ctx_tiles
    mod = mod_ref[0]
    x = jnp.where(is_ctx, xc_ref[...], xl_ref[...])
    h = _modulated_norm(x, g1_ref[...], mod[:, D:2 * D], mod[:, 0:D]).astype(BF16)
    om = _dot(jnp.where(is_ctx, oc_ref[...], ol_ref[...]), wom_ref[...])
    merged = _sigmoid(_dot(h, wgate_ref[:, 0:D])) * om
    og = _dot(jnp.where(is_ctx, gc_ref[...], gl_ref[...]), wog_ref[...])
    merged = (merged + _sigmoid(_dot(h, wgate_ref[:, D:2 * D])) * og).astype(BF16)
    mix = _dot(merged, wout_ref[...])
    x1 = x + mod[:, 2 * D:3 * D] * mix
    x1_ref[...] = x1
    h2 = _modulated_norm(x1, g2_ref[...], mod[:, 4 * D:5 * D], mod[:, 3 * D:4 * D])
    h2_ref[...] = h2.astype(BF16)
    a, b, c = _split3(h2)
    wa, wb, wc = _split3(wr_ref[...])
    logits = (_dot(a, wa) + _dot(a, wb) + _dot(b, wa)) + (_dot(a, wc) + _dot(b, wb) + _dot(c, wa))
    lane = lax.broadcasted_iota(I32, logits.shape, 1)
    logits = jnp.where(lane < N_EXPERTS, logits, -jnp.inf)
    p = jnp.exp(logits - logits.max(axis=-1, keepdims=True))
    aff_ref[...] = p / p.sum(axis=-1, keepdims=True)


def _merge(xc, xl, mod, layer, o_ctx, o_lat, og_ctx, og_lat, g1, wgate, wom, wog, wout, g2, wr, cond_of_tile):
    (Tc, D), Tl = xc.shape, xl.shape[0]
    T2 = Tc + Tl
    n_ctx_tiles, n_lat_tiles = Tc // TOKEN_TILE, Tl // TOKEN_TILE
    nt = n_ctx_tiles + n_lat_tiles
    L, R = mod.shape[:2]
    row = lambda w: pl.BlockSpec((TOKEN_TILE, w), lambda i: (i, 0))
    ctx, lat = _ctx_lat_specs(n_ctx_tiles, n_lat_tiles)
    W = o_ctx.shape[1]
    params = [g1, wgate, wom, wog, wout, g2, wr]
    return pl.pallas_call(
        functools.partial(_merge_body, D=D, n_ctx_tiles=n_ctx_tiles),
        grid=(nt,),
        in_specs=[ctx(D), lat(D), _mod_spec(R, D, layer, cond_of_tile), ctx(W), lat(W), ctx(GLA_W), lat(GLA_W)]
                 + [_layer_spec(a, layer) for a in params],
        out_specs=[row(D), row(D), row(LANES)],
        out_shape=[jax.ShapeDtypeStruct((T2, D), F32), jax.ShapeDtypeStruct((T2, D), BF16),
                   jax.ShapeDtypeStruct((T2, LANES), F32)],
        compiler_params=_cparams(("arbitrary",)),
        name="merge_router",
    )(xc, xl, mod.reshape(L * R, 1, 6 * D), o_ctx, o_lat, og_ctx, og_lat, *params)


def _topk_body(a_ref, slot_ref, cb_ref, *, T, cap):
    a = a_ref[0]
    E = a.shape[0]

    def count_ge(thr):
        return jnp.sum((a >= thr).astype(F32), axis=1, keepdims=True)

    hi = jnp.full((E, 1), 2.0, F32)
    for j in range(TOPK_EXP_BITS - 1, -1, -1):
        cand = hi * (2.0 ** -(2 ** j))
        hi = jnp.where(count_ge(cand) < cap, cand, hi)
    lo = jnp.where(hi > F32_MIN_NORMAL, 0.5 * hi, 0.0)

    def bisect(_, lh):
        lo, hi = lh
        mid = 0.5 * (lo + hi)
        up = count_ge(mid) >= cap
        return jnp.where(up, mid, lo), jnp.where(up, hi, mid)

    lo, hi = lax.fori_loop(0, TOPK_BISECT_STEPS, bisect, (lo, hi))
    gt = a >= hi
    eq = jnp.logical_and(a >= lo, a < hi)
    need = cap - jnp.sum(gt.astype(F32), axis=1, keepdims=True)

    r = lax.broadcasted_iota(I32, (LANES, LANES), 0)
    c = lax.broadcasted_iota(I32, (LANES, LANES), 1)
    triu = (r < c).astype(BF16)
    lane = lax.broadcasted_iota(I32, (E, LANES), 1)

    nb = T // LANES
    per_tile = ROW_TILE // LANES
    carry_eq = jnp.zeros((E, 1), F32)
    carry_sel = jnp.zeros((E, 1), F32)
    cb = jnp.zeros((E, LANES), I32)
    for j in range(nb):
        sl = slice(j * LANES, (j + 1) * LANES)
        eq_j = eq[:, sl].astype(BF16)
        pre = _dot(eq_j, triu) + carry_eq
        carry_eq = carry_eq + jnp.sum(eq_j.astype(F32), axis=1, keepdims=True)
        sel = jnp.logical_or(gt[:, sl], jnp.logical_and(eq[:, sl], pre < need))
        sel_b = sel.astype(BF16)
        if j % per_tile == 0:
            cb = jnp.where(lane == j // per_tile, carry_sel.astype(I32), cb)
        slot = (_dot(sel_b, triu) + carry_sel).astype(I32)
        carry_sel = carry_sel + jnp.sum(sel_b.astype(F32), axis=1, keepdims=True)
        slot_ref[0, :, sl] = jnp.where(sel, slot, -1)
    cb_ref[0] = jnp.where(lane == nb // per_tile, carry_sel.astype(I32), cb)


def _topk(aff_t, cap):
    G, E, T = aff_t.shape
    return pl.pallas_call(
        functools.partial(_topk_body, T=T, cap=cap),
        grid=(G,),
        in_specs=[pl.BlockSpec((1, E, T), lambda g: (g, 0, 0))],
        out_specs=[pl.BlockSpec((1, E, T), lambda g: (g, 0, 0)),
                   pl.BlockSpec((1, E, LANES), lambda g: (g, 0, 0))],
        out_shape=[jax.ShapeDtypeStruct((G, E, T), I32), jax.ShapeDtypeStruct((G, E, LANES), I32)],
        compiler_params=_cparams(("arbitrary",)),
        name="expert_topk",
    )(aff_t)


def _expert_body(cb_ref, slot_ref, aff_ref, h_ref, wg_ref, wu_ref, wd_ref, y_ref, xs_ref, acc_ref, ws_ref, *,
                 cap, win, nt, nfh, E):
    g = pl.program_id(0)
    e = pl.program_id(1)
    fh = pl.program_id(2)
    base = (g * E + e) * LANES

    @pl.when(fh == 0)
    def _gather():
        xs_ref[...] = jnp.zeros_like(xs_ref)
        ws_ref[...] = jnp.zeros_like(ws_ref)

        def tile(i, carry):
            lo = cb_ref[base + i]
            hi = cb_ref[base + i + 1]
            srow = slot_ref[i]
            arow = aff_ref[i]
            for w in range(cap // win):
                @pl.when(jnp.logical_and(lo < (w + 1) * win, hi > w * win))
                def _():
                    j = lax.broadcasted_iota(I32, (win, 1), 0) + w * win
                    hit = srow == j
                    hs = h_ref[pl.ds(pl.multiple_of(i * ROW_TILE, ROW_TILE), ROW_TILE), :]
                    xs_ref[w * win:(w + 1) * win, :] += _dot(hit.astype(BF16), hs)
                    ws_ref[w * win:(w + 1) * win, :] += jnp.sum(jnp.where(hit, arow, 0.0), axis=1, keepdims=True)
            return carry

        lax.fori_loop(0, nt, tile, 0)

    xb = xs_ref[...].astype(BF16)
    gate = _dot(xb, wg_ref[...].astype(BF16))
    up = _dot(xb, wu_ref[...].astype(BF16))
    hid = (gate * _sigmoid(gate) * up).astype(BF16)
    part = _dot(hid, wd_ref[...].astype(BF16))

    @pl.when(fh == 0)
    def _():
        acc_ref[...] = part

    @pl.when(fh > 0)
    def _():
        acc_ref[...] += part

    @pl.when(fh == nfh - 1)
    def _():
        y_ref[...] = (acc_ref[...] * ws_ref[...]).astype(BF16)


def _experts(cb_flat, slot5, aff5, h2, w_gate, w_up, w_down, layer, cap, win):
    G, E, nt = slot5.shape[:3]
    T = nt * ROW_TILE
    D = h2.shape[1]
    FF = w_gate.shape[-1]
    nfh = 2
    fb = FF // nfh
    grid_spec = pltpu.PrefetchScalarGridSpec(
        num_scalar_prefetch=1,
        grid=(G, E, nfh),
        in_specs=[pl.BlockSpec((None, None, nt, 1, ROW_TILE), lambda g, e, f, cb: (g, e, 0, 0, 0)),
                  pl.BlockSpec((None, None, nt, 1, ROW_TILE), lambda g, e, f, cb: (g, e, 0, 0, 0)),
                  pl.BlockSpec((T, D), lambda g, e, f, cb: (g, 0)),
                  pl.BlockSpec((None, None, D, fb), lambda g, e, f, cb: (layer, e, 0, f)),
                  pl.BlockSpec((None, None, D, fb), lambda g, e, f, cb: (layer, e, 0, f)),
                  pl.BlockSpec((None, None, fb, D), lambda g, e, f, cb: (layer, e, f, 0))],
        out_specs=pl.BlockSpec((None, None, cap, D), lambda g, e, f, cb: (g, e, 0, 0)),
        scratch_shapes=[pltpu.VMEM((cap, D), F32), pltpu.VMEM((cap, D), F32), pltpu.VMEM((cap, 1), F32)],
    )
    return pl.pallas_call(
        functools.partial(_expert_body, cap=cap, win=win, nt=nt, nfh=nfh, E=E),
        grid_spec=grid_spec,
        out_shape=jax.ShapeDtypeStruct((G, E, cap, D), BF16),
        compiler_params=_cparams(("arbitrary", "arbitrary", "arbitrary")),
        name="expert_ffn",
    )(cb_flat, slot5, aff5, h2, w_gate, w_up, w_down)


def _combine_body(cb_ref, x_ref, mod_ref, slot_ref, y_ref, spread_ref, gf_ref, oc_ref, ol_ref, acc_ref, *,
                  D, cap, win, cw, E, tiles_per_group, final):
    i = pl.program_id(0)
    g = i // tiles_per_group
    ti = i % tiles_per_group
    slots = slot_ref[...]
    los = [cb_ref[(g * E + e) * LANES + ti] for e in range(E)]
    his = [cb_ref[(g * E + e) * LANES + ti + 1] for e in range(E)]
    starts = [jnp.minimum(jnp.bitwise_and(lo, -BF16_ROWS), cap - cw) for lo in los]
    fits = his[0] <= starts[0] + cw
    for e in range(1, E):
        fits = jnp.logical_and(fits, his[e] <= starts[e] + cw)

    @pl.when(fits)
    def _fast():
        sp1 = slots + 1
        digits = jnp.concatenate([jnp.right_shift(sp1, 4), jnp.bitwise_and(sp1, 15)], axis=1)
        spread = _dot(digits.astype(F32).astype(BF16), spread_ref[...])
        lane = lax.broadcasted_iota(I32, (1, cw), 1)
        tgt = jnp.concatenate([lane + (starts[e] + 1) for e in range(E)], axis=1).astype(F32)
        onehot = (spread == tgt).astype(BF16)
        rows = jnp.concatenate([y_ref[e, pl.ds(pl.multiple_of(starts[e], BF16_ROWS), cw), :] for e in range(E)],
                               axis=0)
        acc_ref[...] = _dot(onehot, rows)

    @pl.when(jnp.logical_not(fits))
    def _general():
        acc_ref[...] = jnp.zeros_like(acc_ref)
        for e in range(E):
            col = slots[:, e:e + 1]
            for w in range(cap // win):
                @pl.when(jnp.logical_and(los[e] < (w + 1) * win, his[e] > w * win))
                def _():
                    j = lax.broadcasted_iota(I32, (1, win), 1) + w * win
                    oh = (col == j).astype(BF16)
                    acc_ref[...] += _dot(oh, y_ref[e, w * win:(w + 1) * win, :])

    mod = mod_ref[0]
    x2 = x_ref[...] + mod[:, 5 * D:6 * D] * acc_ref[...]
    if final:
        x2 = _rms(x2, gf_ref[...])

    @pl.when(g == 0)
    def _():
        oc_ref[...] = x2

    @pl.when(g != 0)
    def _():
        ol_ref[...] = x2


def _combine(cb_flat, x1, mod_l, slot_t, y, g_final, cond_of_tile, cap, win, final):
    T2, D = x1.shape
    G, E = y.shape[:2]
    nt = T2 // ROW_TILE
    tpg = nt // G
    R = mod_l.shape[0]
    cw = min(LANES, cap)
    spread = np.zeros((2 * LANES, E * cw), np.float32)
    for e in range(E):
        spread[e, e * cw:(e + 1) * cw] = 16.0
        spread[LANES + e, e * cw:(e + 1) * cw] = 1.0
    spread = jnp.asarray(spread, BF16)
    assert G == 2
    ctx, lat = _ctx_lat_specs(tpg, tpg, ROW_TILE)
    grid_spec = pltpu.PrefetchScalarGridSpec(
        num_scalar_prefetch=1,
        grid=(nt,),
        in_specs=[pl.BlockSpec((ROW_TILE, D), lambda i, cb: (i, 0)),
                  pl.BlockSpec((1, 1, 6 * D), lambda i, cb: (cond_of_tile(i), 0, 0)),
                  pl.BlockSpec((ROW_TILE, LANES), lambda i, cb: (i, 0)),
                  pl.BlockSpec((None, E, cap, D), lambda i, cb: (i // tpg, 0, 0, 0)),
                  pl.BlockSpec(spread.shape, lambda i, cb: (0, 0)),
                  pl.BlockSpec((1, D), lambda i, cb: (0, 0))],
        out_specs=[ctx(D), lat(D)],
        scratch_shapes=[pltpu.VMEM((ROW_TILE, D), F32)],
    )
    return pl.pallas_call(
        functools.partial(_combine_body, D=D, cap=cap, win=win, cw=cw, E=E, tiles_per_group=tpg, final=final),
        grid_spec=grid_spec,
        out_shape=[jax.ShapeDtypeStruct((T2 // G, D), F32), jax.ShapeDtypeStruct((T2 // G, D), F32)],
        compiler_params=_cparams(("arbitrary",)),
        name="moe_combine",
    )(cb_flat, x1, mod_l.reshape(R, 1, 6 * D), slot_t, y, spread, g_final)


def _pack_weights(w_in, w_uq, w_uk, w_uv, w_gla_gate, b_gla_gate):
    L, D, _ = w_in.shape
    sizes = (Q_RANK, KV_RANK, MLA_ROPE, GLA_W, GLA_W, GLA_W, 2 * GLA_GATE_RANK, GLA_W, D, D)
    idx = np.cumsum(sizes)[:-1]
    pq, pkv, kr, gq, gk, gv, glr, gog, ga, gb = jnp.split(w_in, [int(i) for i in idx], axis=-1)
    npair = MLA_ROPE // 4
    swap = np.concatenate([np.arange(npair, 2 * npair), np.arange(0, npair),
                           np.arange(3 * npair, 4 * npair), np.arange(2 * npair, 3 * npair)])

    def slot_rope(w):
        return jnp.pad(w, ((0, 0), (0, 0), (MLA_NOPE, HEAD_PAD - MLA_NOPE - MLA_ROPE)))

    glr_p = jnp.pad(glr, ((0, 0), (0, 0), (0, LANES - 2 * GLA_GATE_RANK)))
    wp = jnp.concatenate([pq, pkv, slot_rope(kr), slot_rope(kr[..., swap]), gq, gk, gv, glr_p, gog],
                         axis=-1).astype(BF16)
    wgate = jnp.concatenate([ga, gb], axis=-1).astype(BF16)

    uq = w_uq.reshape(L, Q_RANK, MLA_HEADS, MLA_NOPE + MLA_ROPE)
    pad_h = HEAD_PAD - MLA_NOPE - MLA_ROPE
    uq_n = jnp.pad(uq, ((0, 0), (0, 0), (0, 0), (0, pad_h))).reshape(L, Q_RANK, MLA_HEADS * HEAD_PAD)
    uq_s = jnp.concatenate([jnp.zeros_like(uq[..., :MLA_NOPE]), uq[..., MLA_NOPE:][..., swap]], axis=-1)
    uq_s = jnp.pad(uq_s, ((0, 0), (0, 0), (0, 0), (0, pad_h))).reshape(L, Q_RANK, MLA_HEADS * HEAD_PAD)
    wuq = jnp.concatenate([uq_n, uq_s], axis=-1).astype(BF16)

    uk = w_uk.reshape(L, KV_RANK, MLA_HEADS, MLA_NOPE)
    wuk = jnp.pad(uk, ((0, 0), (0, 0), (0, 0), (0, HEAD_PAD - MLA_NOPE))).reshape(
        L, KV_RANK, MLA_HEADS * HEAD_PAD).astype(BF16)
    wuv = w_uv.astype(BF16)

    wg = jnp.zeros((L, LANES, 2 * GLA_W), F32)
    wg = wg.at[:, 0:GLA_GATE_RANK, 0:GLA_W].set(w_gla_gate[:, 0])
    wg = wg.at[:, GLA_GATE_RANK:2 * GLA_GATE_RANK, GLA_W:].set(w_gla_gate[:, 1])
    bg = b_gla_gate.reshape(L, 1, 2 * GLA_W)
    return wp, wgate, wuq, wuk, wuv, wg.astype(BF16), bg


def _rope_tables(n_lat):
    npair = MLA_ROPE // 4
    freqs = ROPE_BASE ** (-jnp.arange(npair, dtype=F32) / npair)
    pos = jnp.arange(n_lat)
    ang_r = (pos // GRID_W).astype(F32)[:, None] * freqs
    ang_c = (pos % GRID_W).astype(F32)[:, None] * freqs
    cr, sr, cc, sc = jnp.cos(ang_r), jnp.sin(ang_r), jnp.cos(ang_c), jnp.sin(ang_c)
    cos32 = jnp.concatenate([cr, cr, cc, cc], axis=-1)
    sin32 = jnp.concatenate([-sr, sr, -sc, sc], axis=-1)
    pad_h = HEAD_PAD - MLA_NOPE - MLA_ROPE
    ones = jnp.ones((n_lat, MLA_NOPE), F32)
    cos_l = jnp.concatenate([ones, cos32, jnp.zeros((n_lat, pad_h), F32)], axis=-1)
    sin_l = jnp.pad(sin32, ((0, 0), (MLA_NOPE, pad_h)))
    cos_i = jnp.concatenate([jnp.ones((TOKEN_TILE, MLA_NOPE + MLA_ROPE), F32),
                             jnp.zeros((TOKEN_TILE, pad_h), F32)], -1)
    sin_i = jnp.zeros((TOKEN_TILE, HEAD_PAD), F32)
    return jnp.concatenate([cos_i, cos_l], 0), jnp.concatenate([sin_i, sin_l], 0)


def _gla_consts():
    C = GLA_CHUNK
    r = np.arange(C)
    trif = (r[None, :] <= r[:, None]).astype(np.float32)
    trib = (r[None, :] >= r[:, None]).astype(np.float32)
    h = np.arange(GLA_QUAD) // GLA_DK
    ones_bd = (h[:, None] == h[None, :]).astype(np.float32)
    return jnp.asarray(trif, BF16), jnp.asarray(trib, BF16), jnp.asarray(ones_bd, BF16)


def kernel(x_prompt, x_sample, cache_ckv, cache_krope, state_gla, c, c_ctx, w_mod, b_mod, g_norm1, g_norm2, w_in, g_q, g_kv, w_uq, w_uk, w_uv, w_o_mla, w_gla_gate, b_gla_gate, g_gla, w_o_gla, w_out, w_router, w_e_gate, w_e_up, w_e_down, g_final):
    B, N, D = x_prompt.shape
    DB, DN, _ = x_sample.shape
    L = w_in.shape[0]
    Tc, Tl = B * N, DB * DN
    assert Tc == Tl and Tc % DN == 0 and N % ROW_TILE == 0 and DN % TOKEN_TILE == 0 and Tc % TOKEN_TILE == 0
    assert N % GLA_CHUNK == 0 and DN % GLA_CHUNK == 0 and DN % GRID_W == 0 and TOKEN_TILE == ROW_TILE
    T = Tc
    G = 2
    cap = max(1, CAPACITY_FACTOR * T // N_EXPERTS)
    win = LANES if cap % LANES == 0 else cap
    assert cap % win == 0 and win % SUBLANES == 0 and 1 + DB <= SUBLANES

    def tile_maps(rows):
        nct, per_seq = Tc // rows, DN // rows
        cond = lambda i: jnp.where(i < nct, 0, 1 + (i - nct) // per_seq)
        tab = lambda i: jnp.where(i < nct, 0, 1 + (i - nct) % per_seq)
        return cond, tab

    cond_tok, tab_tok = tile_maps(TOKEN_TILE)
    cond_row, _ = tile_maps(ROW_TILE)
    n_ctx_tiles = Tc // TOKEN_TILE

    cvec = jnp.concatenate([c_ctx[None, :], c, jnp.zeros((SUBLANES - 1 - DB, D), F32)], axis=0)
    mod = _adaln_all(cvec, w_mod, b_mod)

    wp, wgate, wuq, wuk, wuv, wg, bg = _pack_weights(w_in, w_uq, w_uk, w_uv, w_gla_gate, b_gla_gate)
    cos_t, sin_t = _rope_tables(DN)
    gla_consts = _gla_consts()
    wom = w_o_mla.astype(BF16)
    wog = w_o_gla.astype(BF16)
    wout = w_out.astype(BF16)
    wr = jnp.pad(w_router, ((0, 0), (0, 0), (0, LANES - N_EXPERTS)))
    g1s, g2s, gqs, gkvs = g_norm1[:, None, :], g_norm2[:, None, :], g_q[:, None, :], g_kv[:, None, :]

    ckr_pad = jnp.pad(cache_krope, ((0, 0), (0, 0), (0, 0), (MLA_NOPE, HEAD_PAD - MLA_NOPE - MLA_ROPE)))
    kc_all, vc_all = _cache_kv(cache_ckv, ckr_pad, wuk, wuv)
    st_t = jnp.swapaxes(state_gla, -1, -2)

    xc, xl = x_prompt.reshape(Tc, D), x_sample.reshape(Tl, D)
    ckv_list, kr_list, gla_list = [], [], []
    for l in range(L):
        pre = _inproj(xc, xl, mod, l, g1s, wp, gqs, gkvs, wuq, wuk, wuv, wg, bg, cos_t, sin_t, cond_tok, tab_tok)
        ckv_list.append(pre["ckv"][:Tc].reshape(B, N, KV_RANK))
        kr_list.append(pre["kr"][:Tc].reshape(B, N, MLA_ROPE))

        o_ctx = _attention_ctx(pre["q"], pre["k"], pre["v"], B, N)
        o_lat = _attention_lat(pre["q"], pre["k"], pre["v"], kc_all, vc_all, l, Tc, DB, DN, min(DN, TOKEN_TILE))

        gg = g_gla[l][None]
        og_ctx, s_fin = _gla(pre["gq"], pre["gk"], pre["gv"], pre["g"], pre["sg"], gg, gla_consts, 0, B, N,
                             want_final=True)
        (og_lat,) = _gla(pre["gq"], pre["gk"], pre["gv"], pre["g"], pre["sg"], gg, gla_consts, Tc, DB, DN,
                         s0=(st_t, l))
        gla_list.append(s_fin)

        x1, h2, aff = _merge(xc, xl, mod, l, o_ctx, o_lat, og_ctx, og_lat, g1s, wgate, wom, wog, wout, g2s, wr,
                             cond_tok)

        aff_t = jnp.swapaxes(aff[:, :N_EXPERTS].reshape(G, T, N_EXPERTS), 1, 2)
        slot, cb = _topk(aff_t, cap)
        cb_flat = cb.reshape(-1)
        slot5 = slot.reshape(G, N_EXPERTS, T // ROW_TILE, 1, ROW_TILE)
        slot_t = jnp.pad(jnp.swapaxes(slot, 1, 2).reshape(G * T, N_EXPERTS),
                         ((0, 0), (0, LANES - N_EXPERTS)), constant_values=-1)
        aff5 = aff_t.reshape(G, N_EXPERTS, T // ROW_TILE, 1, ROW_TILE)
        y = _experts(cb_flat, slot5, aff5, h2, w_e_gate, w_e_up, w_e_down, l, cap, win)
        xc, xl = _combine(cb_flat, x1, mod[l], slot_t, y, g_final[None], cond_row, cap, win, final=(l == L - 1))

    y_prompt = xc.reshape(B, N, D)
    y_sample = xl.reshape(DB, DN, D)
    new_ckv = jnp.stack(ckv_list, axis=1)
    new_krope = jnp.stack(kr_list, axis=1)
    new_gla = jnp.stack(gla_list, axis=1)
    return (y_prompt, y_sample, new_ckv, new_krope, new_gla)
```

```python
import functools

import jax
import jax.numpy as jnp
import numpy as np
from jax import lax
from jax.experimental import pallas as pl
from jax.experimental.pallas import tpu as pltpu

F32 = jnp.float32
BF16 = jnp.bfloat16
I32 = jnp.int32

GRID_W = 64
EPS = 1e-6
MLA_HEADS = 8
MLA_NOPE = 64
MLA_ROPE = 32
MLA_V = 64
Q_RANK = 384
KV_RANK = 256
ROPE_BASE = 10000.0
GLA_HEADS = 8
GLA_DK = 64
GLA_DV = 64
GLA_W = GLA_HEADS * GLA_DK
GLA_GATE_RANK = 16
GLA_TAU = 16.0
GLA_CHUNK = 64
N_EXPERTS = 16
CAPACITY_FACTOR = 2

LANES = 128
SUBLANES = 8
BF16_ROWS = 16
HEAD_PAD = 128
VMEM_LIMIT = 56 * 1024 * 1024

ROW_TILE = 256
TOKEN_TILE = 256
GLA_QUAD = 4 * GLA_DK
GLA_DIAG = 8
GLA_SEQS_PER_STEP = 2
GLA_MILD_LOG2 = 64.0
F32_MIN_NORMAL = 2.0 ** -126
TOPK_EXP_BITS = 7
TOPK_BISECT_STEPS = 52
GLA_DK_SHIFT = GLA_DK.bit_length() - 1
assert 1 << GLA_DK_SHIFT == GLA_DK
LOG2E = float(np.log2(np.e))
Q_PRESCALE = (MLA_NOPE + MLA_ROPE) ** -0.5 * LOG2E

_SEG = {}
_off = 0
for _name, _width in (("pq", Q_RANK), ("pkv", KV_RANK), ("kr", HEAD_PAD), ("krs", HEAD_PAD),
                      ("gq", GLA_W), ("gk", GLA_W), ("gv", GLA_W), ("glr", LANES), ("gog", GLA_W)):
    _SEG[_name] = (_off, _width)
    _off += _width


def _cparams(sem, vmem=VMEM_LIMIT):
    return pltpu.CompilerParams(dimension_semantics=sem, vmem_limit_bytes=vmem)


def _dot(a, b):
    return jnp.dot(a, b, preferred_element_type=F32)


def _dot_nt(a, b):
    return lax.dot_general(a, b, (((1,), (1,)), ((), ())), preferred_element_type=F32)


def _dot_tn(a, b):
    return lax.dot_general(a, b, (((0,), (0,)), ((), ())), preferred_element_type=F32)


def _rms(x, g):
    return x * lax.rsqrt(jnp.mean(x * x, axis=-1, keepdims=True) + EPS) * g


def _modulated_norm(x, g, scale, shift):
    return _rms(x, g) * (1.0 + scale) + shift


def _sigmoid(x):
    return 0.5 * jnp.tanh(0.5 * x) + 0.5


def _split3(x):
    a = x.astype(BF16)
    r = x - a.astype(F32)
    b = r.astype(BF16)
    c = (r - b.astype(F32)).astype(BF16)
    return a, b, c


def _mod_body(c_ref, w_ref, b_ref, o_ref):
    c = c_ref[...]
    s = (c * _sigmoid(c)).astype(BF16)
    o_ref[0] = _dot(s, w_ref[0].astype(BF16)) + b_ref[0]


def _adaln_all(cvec, w_mod, b_mod):
    L, D, D6 = w_mod.shape
    R = cvec.shape[0]
    tn = 1536
    return pl.pallas_call(
        _mod_body,
        grid=(L, D6 // tn),
        in_specs=[pl.BlockSpec((R, D), lambda l, j: (0, 0)),
                  pl.BlockSpec((1, D, tn), lambda l, j: (l, 0, j)),
                  pl.BlockSpec((1, 1, tn), lambda l, j: (l, 0, j))],
        out_specs=pl.BlockSpec((1, R, tn), lambda l, j: (l, 0, j)),
        out_shape=jax.ShapeDtypeStruct((L, R, D6), F32),
        compiler_params=_cparams(("arbitrary", "arbitrary")),
        name="adaln_mod",
    )(cvec, w_mod, b_mod.reshape(L, 1, D6))


def _inproj_body(xc_ref, xl_ref, mod_ref, g1_ref, w_ref, gq_ref, gkv_ref, wuq_ref, wuk_ref, wuv_ref,
                 wg_ref, bg_ref, ct_ref, st_ref,
                 q_ref, k_ref, v_ref, ckv_ref, kr_ref, gqo_ref, gko_ref, gvo_ref, g_ref,
                 sg_ref, *, D, n_ctx_tiles):
    is_ctx = pl.program_id(0) < n_ctx_tiles
    mod = mod_ref[0]
    x = jnp.where(is_ctx, xc_ref[...], xl_ref[...])
    h = _modulated_norm(x, g1_ref[...], mod[:, D:2 * D], mod[:, 0:D]).astype(BF16)

    def seg(name):
        a, w = _SEG[name]
        return _dot(h, w_ref[:, a:a + w])

    cos = ct_ref[...]
    sin = st_ref[...]
    nq = MLA_HEADS * HEAD_PAD

    cq = _rms(seg("pq"), gq_ref[...]).astype(BF16)
    qq = _dot(cq, wuq_ref[...])
    for hd in range(MLA_HEADS):
        a = hd * HEAD_PAD
        q_ref[:, a:a + HEAD_PAD] = ((qq[:, a:a + HEAD_PAD] * cos
                                     + qq[:, nq + a:nq + a + HEAD_PAD] * sin) * Q_PRESCALE).astype(BF16)

    ckv = _rms(seg("pkv"), gkv_ref[...])
    ckv_b = ckv.astype(BF16)
    kr = seg("kr")
    ckv_ref[...] = ckv
    kr_ref[...] = kr[:, MLA_NOPE:MLA_NOPE + MLA_ROPE]
    kr_rot = kr * cos + seg("krs") * sin
    kn = _dot(ckv_b, wuk_ref[...])
    for hd in range(MLA_HEADS):
        a = hd * HEAD_PAD
        k_ref[:, a:a + HEAD_PAD] = (kn[:, a:a + HEAD_PAD] + kr_rot).astype(BF16)
    v_ref[...] = _dot(ckv_b, wuv_ref[...]).astype(BF16)

    gqo_ref[...] = (seg("gq") * (GLA_DK ** -0.5)).astype(BF16)
    gko_ref[...] = seg("gk").astype(BF16)
    gvo_ref[...] = seg("gv").astype(BF16)
    logit = _dot(seg("glr").astype(BF16), wg_ref[...]) + bg_ref[...]
    g_ref[...] = (jnp.minimum(logit, 0.0) - jnp.log1p(jnp.exp(-jnp.abs(logit)))) * (1.0 / GLA_TAU)
    gog = seg("gog")
    sg_ref[...] = (gog * _sigmoid(gog)).astype(BF16)


def _layer_spec(a, layer):
    nd = a.ndim - 1
    return pl.BlockSpec((None,) + a.shape[1:], lambda i: (layer,) + (0,) * nd, pipeline_mode=pl.Buffered(1))


def _mod_spec(R, D, layer, cond_of_tile):
    return pl.BlockSpec((1, 1, 6 * D), lambda i: (layer * R + cond_of_tile(i), 0, 0))


def _ctx_lat_specs(n_ctx_tiles, n_lat_tiles, rows=TOKEN_TILE):
    ctx = lambda w: pl.BlockSpec((rows, w), lambda i, *_: (jnp.minimum(i, n_ctx_tiles - 1), 0))
    lat = lambda w: pl.BlockSpec((rows, w), lambda i, *_: (jnp.clip(i - n_ctx_tiles, 0, n_lat_tiles - 1), 0))
    return ctx, lat


def _inproj(xc, xl, mod, layer, g1, wp, gq, gkv, wuq, wuk, wuv, wg, bg, cos_t, sin_t, cond_of_tile, tab_of_tile):
    (Tc, D), Tl = xc.shape, xl.shape[0]
    T2 = Tc + Tl
    nct, nlt = Tc // TOKEN_TILE, Tl // TOKEN_TILE
    L, R = mod.shape[:2]
    row = lambda w: pl.BlockSpec((TOKEN_TILE, w), lambda i: (i, 0))
    ctx, lat = _ctx_lat_specs(nct, nlt)
    outs = [("q", MLA_HEADS * HEAD_PAD, BF16), ("k", MLA_HEADS * HEAD_PAD, BF16),
            ("v", MLA_HEADS * MLA_V, BF16), ("ckv", KV_RANK, F32), ("kr", MLA_ROPE, F32),
            ("gq", GLA_W, BF16), ("gk", GLA_W, BF16), ("gv", GLA_W, BF16), ("g", 2 * GLA_W, F32),
            ("sg", GLA_W, BF16)]
    ctx_only = ()
    params = [g1, wp, gq, gkv, wuq, wuk, wuv, wg, bg]
    res = pl.pallas_call(
        functools.partial(_inproj_body, D=D, n_ctx_tiles=nct),
        grid=(nct + nlt,),
        in_specs=[ctx(D), lat(D), _mod_spec(R, D, layer, cond_of_tile)]
                 + [_layer_spec(a, layer) for a in params]
                 + [pl.BlockSpec((TOKEN_TILE, HEAD_PAD), lambda i: (tab_of_tile(i), 0)),
                    pl.BlockSpec((TOKEN_TILE, HEAD_PAD), lambda i: (tab_of_tile(i), 0))],
        out_specs=[ctx(w) if nm in ctx_only else row(w) for nm, w, _ in outs],
        out_shape=[jax.ShapeDtypeStruct((Tc if nm in ctx_only else T2, w), dt) for nm, w, dt in outs],
        compiler_params=_cparams(("arbitrary",)),
        name="inproj",
    )(xc, xl, mod.reshape(L * R, 1, 6 * D), *params, cos_t, sin_t)
    return dict(zip([n for n, _, _ in outs], res))


def _cache_kv_body(ckv_ref, krp_ref, wuk_ref, wuv_ref, k_ref, v_ref):
    c = ckv_ref[...].astype(BF16)
    kn = _dot(c, wuk_ref[...])
    krp = krp_ref[...]
    for hd in range(MLA_HEADS):
        a = hd * HEAD_PAD
        k_ref[:, a:a + HEAD_PAD] = (kn[:, a:a + HEAD_PAD] + krp).astype(BF16)
    v_ref[...] = _dot(c, wuv_ref[...]).astype(BF16)


def _cache_kv(cache_ckv, cache_kr_pad, wuk, wuv):
    DB, L, P, R = cache_ckv.shape
    nk = MLA_HEADS * HEAD_PAD
    nv = MLA_HEADS * MLA_V
    return pl.pallas_call(
        _cache_kv_body,
        grid=(L, DB),
        in_specs=[pl.BlockSpec((None, None, P, R), lambda l, b: (b, l, 0, 0)),
                  pl.BlockSpec((None, None, P, HEAD_PAD), lambda l, b: (b, l, 0, 0)),
                  pl.BlockSpec((None, R, nk), lambda l, b: (l, 0, 0)),
                  pl.BlockSpec((None, R, nv), lambda l, b: (l, 0, 0))],
        out_specs=[pl.BlockSpec((None, None, P, nk), lambda l, b: (l, b, 0, 0)),
                   pl.BlockSpec((None, None, P, nv), lambda l, b: (l, b, 0, 0))],
        out_shape=[jax.ShapeDtypeStruct((L, DB, P, nk), BF16),
                   jax.ShapeDtypeStruct((L, DB, P, nv), BF16)],
        compiler_params=_cparams(("arbitrary", "arbitrary")),
        name="cache_kv",
    )(cache_ckv, cache_kr_pad, wuk, wuv)


def _attn_body(*refs, nseg, npairs):
    q_ref = refs[0]
    k_refs = refs[1:1 + nseg]
    v_refs = refs[1 + nseg:1 + 2 * nseg]
    o_ref = refs[1 + 2 * nseg]
    lane = lax.broadcasted_iota(I32, (1, 2 * MLA_V), 1)
    for pr in range(npairs):
        vs = slice(pr * 2 * MLA_V, (pr + 1) * 2 * MLA_V)
        outs = []
        for hh in range(2):
            hs = slice((2 * pr + hh) * HEAD_PAD, (2 * pr + hh + 1) * HEAD_PAD)
            qh = q_ref[:, hs]
            s = [_dot_nt(qh, kr[:, hs]) for kr in k_refs]
            m = s[0].max(axis=-1, keepdims=True)
            for sj in s[1:]:
                m = jnp.maximum(m, sj.max(axis=-1, keepdims=True))
            p = [jnp.exp2(sj - m) for sj in s]
            den = p[0].sum(axis=-1, keepdims=True)
            for pj in p[1:]:
                den = den + pj.sum(axis=-1, keepdims=True)
            o = None
            for pj, vr in zip(p, v_refs):
                t = _dot(pj.astype(BF16), vr[:, vs])
                o = t if o is None else o + t
            outs.append(o * (1.0 / den))
        o_ref[:, vs] = jnp.where(lane < MLA_V, outs[0], outs[1]).astype(BF16)


def _attention_ctx(q, k, v, nseq, n):
    hp = MLA_HEADS // 2
    return pl.pallas_call(
        functools.partial(_attn_body, nseg=1, npairs=hp),
        grid=(nseq,),
        in_specs=[pl.BlockSpec((n, MLA_HEADS * HEAD_PAD), lambda b: (b, 0)),
                  pl.BlockSpec((n, MLA_HEADS * HEAD_PAD), lambda b: (b, 0)),
                  pl.BlockSpec((n, MLA_HEADS * MLA_V), lambda b: (b, 0))],
        out_specs=pl.BlockSpec((n, MLA_HEADS * MLA_V), lambda b: (b, 0)),
        out_shape=jax.ShapeDtypeStruct((nseq * n, MLA_HEADS * MLA_V), BF16),
        compiler_params=_cparams(("arbitrary",)),
        name="attn_ctx",
    )(q, k, v)


def _attention_lat(q, k, v, kc, vc, layer, row0, nseq, n, tq):
    hp = MLA_HEADS // 2
    P = kc.shape[2]
    qt = n // tq
    q0 = row0 // tq
    s0 = row0 // n
    nq, nv = MLA_HEADS * HEAD_PAD, MLA_HEADS * MLA_V
    return pl.pallas_call(
        functools.partial(_attn_body, nseg=2, npairs=hp),
        grid=(nseq, qt),
        in_specs=[pl.BlockSpec((tq, nq), lambda b, t: (q0 + b * qt + t, 0)),
                  pl.BlockSpec((None, None, P, nq), lambda b, t: (layer, b, 0, 0)),
                  pl.BlockSpec((n, nq), lambda b, t: (s0 + b, 0)),
                  pl.BlockSpec((None, None, P, nv), lambda b, t: (layer, b, 0, 0)),
                  pl.BlockSpec((n, nv), lambda b, t: (s0 + b, 0))],
        out_specs=pl.BlockSpec((tq, nv), lambda b, t: (b * qt + t, 0)),
        out_shape=jax.ShapeDtypeStruct((nseq * n, nv), BF16),
        compiler_params=_cparams(("arbitrary", "arbitrary")),
        name="attn_lat",
    )(q, kc, k, vc, v)


def _head_sums(x, ones_quad):
    return jnp.concatenate([_dot(x[:, a:a + GLA_QUAD], ones_quad) for a in range(0, GLA_W, GLA_QUAD)], axis=1)


def _gla_chunk(q, k, v, g, st_refs, fwd, tri, ones_bd, mild):
    C = GLA_CHUNK
    W = GLA_W
    nquad = W // GLA_QUAD
    rows = lax.broadcasted_iota(I32, (C, 1), 0)

    g1, g2, g3 = _split3(g * LOG2E)
    cum = _dot(tri, jnp.concatenate([g1, g2, g3], axis=1))
    cum = cum[:, 0:W] + cum[:, W:2 * W] + cum[:, 2 * W:3 * W]
    edge = C - 1 if fwd else 0
    last = cum[edge:edge + 1]
    q_in = (q * jnp.exp2(cum)).astype(BF16)
    k_st = (k * jnp.exp2(last - cum)).astype(BF16)
    v_b = v.astype(BF16)

    lane_q = lax.broadcasted_iota(I32, (1, GLA_QUAD), 1)
    head_masks = [jnp.right_shift(lane_q, GLA_DK_SHIFT) == h for h in range(GLA_QUAD // GLA_DK)]
    col_s = jnp.bitwise_and(lane_q, C - 1)

    def stack_heads(xq):
        return jnp.concatenate([jnp.where(mh, xq, jnp.zeros_like(xq)) for mh in head_masks], axis=0)

    a_acc = [jnp.zeros((C, GLA_QUAD), F32) for _ in range(nquad)]
    half = C // 2
    while half >= GLA_DIAG:
        blk = 2 * half
        pieces = []
        for p in range(C // blk):
            rr = p * blk + (half - 1 if fwd else half)
            pieces.append(jnp.broadcast_to(cum[rr:rr + 1], (blk, W)))
        ref = jnp.concatenate(pieces, axis=0) if len(pieces) > 1 else pieces[0]
        upper = jnp.bitwise_and(rows, blk - 1) >= half
        qmask = upper if fwd else jnp.logical_not(upper)
        qe = jnp.where(qmask, q * jnp.exp2(cum - ref), 0.0).astype(BF16)
        ke = jnp.where(qmask, 0.0, k * jnp.exp2(ref - cum)).astype(BF16)
        sh = blk.bit_length() - 1
        same = jnp.right_shift(rows, sh) == jnp.right_shift(col_s, sh)
        for qd in range(nquad):
            sl = slice(qd * GLA_QUAD, (qd + 1) * GLA_QUAD)
            a = _dot_nt(qe[:, sl], stack_heads(ke[:, sl]))
            a_acc[qd] = a_acc[qd] + jnp.where(same, a, 0.0)
        half //= 2

    if mild:
        nb = C // GLA_DIAG
        c3 = cum.reshape(nb, GLA_DIAG, W)
        er = 0 if fwd else GLA_DIAG - 1
        ref = jnp.broadcast_to(c3[:, er:er + 1, :], (nb, GLA_DIAG, W)).reshape(C, W)
        qe = (q * jnp.exp2(cum - ref)).astype(BF16)
        ke = (k * jnp.exp2(ref - cum)).astype(BF16)
        sh = GLA_DIAG.bit_length() - 1
        keep = jnp.right_shift(rows, sh) == jnp.right_shift(col_s, sh)
        keep = jnp.logical_and(keep, (rows >= col_s) if fwd else (rows <= col_s))
        for qd in range(nquad):
            sl = slice(qd * GLA_QUAD, (qd + 1) * GLA_QUAD)
            a = _dot_nt(qe[:, sl], stack_heads(ke[:, sl]))
            a_acc[qd] = a_acc[qd] + jnp.where(keep, a, 0.0)

    r2 =jnp.right_shift(lax.broadcasted_iota(I32, (GLA_QUAD, GLA_QUAD), 0), GLA_DK_SHIFT)
    c2 = jnp.right_shift(lax.broadcasted_iota(I32, (GLA_QUAD, GLA_QUAD), 1), GLA_DK_SHIFT)
    o_parts = []
    for qd in range(nquad):
        sl = slice(qd * GLA_QUAD, (qd + 1) * GLA_QUAD)
        st = st_refs[qd][...]
        o = _dot_nt(q_in[:, sl], st.astype(BF16))
        o = o + _dot(a_acc[qd].astype(BF16), stack_heads(v_b[:, sl]))
        o_parts.append(o)
        upd = _dot_tn(v_b[:, sl], k_st[:, sl])
        st_refs[qd][...] = st * jnp.exp2(last[:, sl]) + jnp.where(r2 == c2, upd, 0.0)
    o = jnp.concatenate(o_parts, axis=1)
    if mild:
        return o

    nb = C // GLA_DIAG
    q3 = q.reshape(nb, GLA_DIAG, W)
    k3 = k.reshape(nb, GLA_DIAG, W)
    v3 = v.reshape(nb, GLA_DIAG, W)
    c3 = cum.reshape(nb, GLA_DIAG, W)
    tl = lax.broadcasted_iota(I32, (1, GLA_DIAG, 1), 1)
    zs = []
    for s in range(GLA_DIAG):
        e = jnp.exp2(c3 - c3[:, s:s + 1, :])
        valid = (tl >= s) if fwd else (tl <= s)
        zs.append(jnp.where(valid, q3 * e * k3[:, s:s + 1, :], 0.0).reshape(C, W).astype(BF16))
    w_all = _head_sums(jnp.concatenate(zs, axis=0), ones_bd)
    od = jnp.zeros((nb, GLA_DIAG, W), F32)
    for s in range(GLA_DIAG):
        od = od + w_all[s * C:(s + 1) * C].reshape(nb, GLA_DIAG, W) * v3[:, s:s + 1, :]
    return o + od.reshape(C, W)


def _gla_body(*refs, n, spb, has_init, has_final):
    it = iter(refs)
    q_ref, k_ref, v_ref, g_ref, sg_ref, gg_ref, trif_ref, trib_ref, ones_ref = (next(it) for _ in range(9))
    s0_ref = next(it) if has_init else None
    o_ref = next(it)
    sf_ref = next(it) if has_final else None
    of_ref, ob_ref = next(it), next(it)
    nquad = GLA_W // GLA_QUAD
    st = [[[next(it) for _ in range(nquad)] for _ in range(2)] for _ in range(spb)]

    C = GLA_CHUNK
    nc = n // C
    hq = GLA_QUAD // GLA_DK
    for sq in range(spb):
        for d in range(2):
            for qd in range(nquad):
                st[sq][d][qd][...] = jnp.zeros((GLA_QUAD, GLA_QUAD), F32)
                if has_init:
                    for h in range(hq):
                        hs = slice(h * GLA_DK, (h + 1) * GLA_DK)
                        st[sq][d][qd][hs, hs] = s0_ref[sq, d, qd * hq + h]

    trif = trif_ref[...]
    trib = trib_ref[...]
    ones_bd = ones_ref[...]

    def step(i, carry, mild):
        for sq in range(spb):
            for d, (tri, acc) in enumerate(((trif, of_ref), (trib, ob_ref))):
                c = i if d == 0 else nc - 1 - i
                rs = pl.ds(pl.multiple_of(sq * n + c * C, C), C)
                gd = g_ref[rs, d * GLA_W:(d + 1) * GLA_W]
                qkv = [r[rs, :].astype(F32) for r in (q_ref, k_ref, v_ref)]
                acc[rs, :] = _gla_chunk(*qkv, gd, st[sq][d], d == 0, tri, ones_bd, mild)
        return carry

    steepest = jnp.max(-g_ref[...]) * (LOG2E * (GLA_DIAG - 1))
    is_mild = steepest < GLA_MILD_LOG2

    @pl.when(is_mild)
    def _():
        lax.fori_loop(0, nc, functools.partial(step, mild=True), 0)

    @pl.when(jnp.logical_not(is_mild))
    def _():
        lax.fori_loop(0, nc, functools.partial(step, mild=False), 0)

    gg = gg_ref[...]
    fr = min(n, 256)

    def fin(i, carry):
        rs = pl.ds(pl.multiple_of(i * fr, fr), fr)
        o = of_ref[rs, :] + ob_ref[rs, :]
        ms = _head_sums(jnp.concatenate(_split3(o * o)[:2], axis=0), ones_bd)
        ms = (ms[0:fr] + ms[fr:2 * fr]) * (1.0 / GLA_DV)
        o_ref[rs, :] = (o * lax.rsqrt(ms + EPS) * gg * sg_ref[rs, :].astype(F32)).astype(BF16)
        return carry

    lax.fori_loop(0, spb * n // fr, fin, 0)

    if has_final:
        for sq in range(spb):
            for d in range(2):
                for qd in range(nquad):
                    s = st[sq][d][qd][...].T
                    for h in range(hq):
                        sf_ref[sq, d, qd * hq + h] = s[h * GLA_DK:(h + 1) * GLA_DK, h * GLA_DK:(h + 1) * GLA_DK]


def _gla(gq, gk, gv, g, sg, g_gla, consts, row0, nseq, n, s0=None, want_final=False):
    trif, trib, ones_bd = consts
    spb = GLA_SEQS_PER_STEP if nseq % GLA_SEQS_PER_STEP == 0 and (row0 // n) % GLA_SEQS_PER_STEP == 0 else 1
    b0 = row0 // (n * spb)
    W = GLA_W
    nquad = W // GLA_QUAD
    seq = lambda w: pl.BlockSpec((spb * n, w), lambda b: (b0 + b, 0))
    full = lambda a: pl.BlockSpec(a.shape, lambda b: (0,) * a.ndim)
    in_specs = [seq(W), seq(W), seq(W), seq(2 * W), seq(W), full(g_gla), full(trif), full(trib), full(ones_bd)]
    args = [gq, gk, gv, g, sg, g_gla, trif, trib, ones_bd]
    if s0 is not None:
        s0_arr, layer = s0
        in_specs.append(pl.BlockSpec((spb, None, 2, GLA_HEADS, GLA_DV, GLA_DK), lambda b: (b, layer, 0, 0, 0, 0)))
        args.append(s0_arr)
    out_specs = [pl.BlockSpec((spb * n, W), lambda b: (b, 0))]
    out_shape = [jax.ShapeDtypeStruct((nseq * n, W), BF16)]
    if want_final:
        out_specs.append(pl.BlockSpec((spb, 2, GLA_HEADS, GLA_DK, GLA_DV), lambda b: (b, 0, 0, 0, 0)))
        out_shape.append(jax.ShapeDtypeStruct((nseq, 2, GLA_HEADS, GLA_DK, GLA_DV), F32))
    scratch = [pltpu.VMEM((spb * n, W), F32), pltpu.VMEM((spb * n, W), F32)]
    scratch += [pltpu.VMEM((GLA_QUAD, GLA_QUAD), F32) for _ in range(spb * 2 * nquad)]
    res = pl.pallas_call(
        functools.partial(_gla_body, n=n, spb=spb, has_init=s0 is not None, has_final=want_final),
        grid=(nseq // spb,),
        in_specs=in_specs,
        out_specs=out_specs,
        out_shape=out_shape,
        scratch_shapes=scratch,
        compiler_params=_cparams(("arbitrary",)),
        name="gla_lat" if s0 is not None else "gla_ctx",
    )(*args)
    return res


def _merge_body(xc_ref, xl_ref, mod_ref, oc_ref, ol_ref, gc_ref, gl_ref, g1_ref, wgate_ref, wom_ref, wog_ref,
                wout_ref, g2_ref, wr_ref, x1_ref, h2_ref, aff_ref, *, D, n_ctx_tiles):
    is_ctx = pl.program_id(0) < n_ctx_tiles
    mod = mod_ref[0]
    x = jnp.where(is_ctx, xc_ref[...], xl_ref[...])
    h = _modulated_norm(x, g1_ref[...], mod[:, D:2 * D], mod[:, 0:D]).astype(BF16)
    om = _dot(jnp.where(is_ctx, oc_ref[...], ol_ref[...]), wom_ref[...])
    merged = _sigmoid(_dot(h, wgate_ref[:, 0:D])) * om
    og = _dot(jnp.where(is_ctx, gc_ref[...], gl_ref[...]), wog_ref[...])
    merged = (merged + _sigmoid(_dot(h, wgate_ref[:, D:2 * D])) * og).astype(BF16)
    mix = _dot(merged, wout_ref[...])
    x1 = x + mod[:, 2 * D:3 * D] * mix
    x1_ref[...] = x1
    h2 = _modulated_norm(x1, g2_ref[...], mod[:, 4 * D:5 * D], mod[:, 3 * D:4 * D])
    h2_ref[...] = h2.astype(BF16)
    a, b, c = _split3(h2)
    wa, wb, wc = _split3(wr_ref[...])
    logits = (_dot(a, wa) + _dot(a, wb) + _dot(b, wa)) + (_dot(a, wc) + _dot(b, wb) + _dot(c, wa))
    lane = lax.broadcasted_iota(I32, logits.shape, 1)
    logits = jnp.where(lane < N_EXPERTS, logits, -jnp.inf)
    p = jnp.exp(logits - logits.max(axis=-1, keepdims=True))
    aff_ref[...] = p / p.sum(axis=-1, keepdims=True)


def _merge(xc, xl, mod, layer, o_ctx, o_lat, og_ctx, og_lat, g1, wgate, wom, wog, wout, g2, wr, cond_of_tile):
    (Tc, D), Tl = xc.shape, xl.shape[0]
    T2 = Tc + Tl
    n_ctx_tiles, n_lat_tiles = Tc // TOKEN_TILE, Tl // TOKEN_TILE
    nt = n_ctx_tiles + n_lat_tiles
    L, R = mod.shape[:2]
    row = lambda w: pl.BlockSpec((TOKEN_TILE, w), lambda i: (i, 0))
    ctx, lat = _ctx_lat_specs(n_ctx_tiles, n_lat_tiles)
    W = o_ctx.shape[1]
    params = [g1, wgate, wom, wog, wout, g2, wr]
    return pl.pallas_call(
        functools.partial(_merge_body, D=D, n_ctx_tiles=n_ctx_tiles),
        grid=(nt,),
        in_specs=[ctx(D), lat(D), _mod_spec(R, D, layer, cond_of_tile), ctx(W), lat(W), ctx(GLA_W), lat(GLA_W)]
                 + [_layer_spec(a, layer) for a in params],
        out_specs=[row(D), row(D), row(LANES)],
        out_shape=[jax.ShapeDtypeStruct((T2, D), F32), jax.ShapeDtypeStruct((T2, D), BF16),
                   jax.ShapeDtypeStruct((T2, LANES), F32)],
        compiler_params=_cparams(("arbitrary",)),
        name="merge_router",
    )(xc, xl, mod.reshape(L * R, 1, 6 * D), o_ctx, o_lat, og_ctx, og_lat, *params)


def _topk_body(a_ref, slot_ref, cb_ref, *, T, cap):
    a = a_ref[0]
    E = a.shape[0]

    def count_ge(thr):
        return jnp.sum((a >= thr).astype(F32), axis=1, keepdims=True)

    hi = jnp.full((E, 1), 2.0, F32)
    for j in range(TOPK_EXP_BITS - 1, -1, -1):
        cand = hi * (2.0 ** -(2 ** j))
        hi = jnp.where(count_ge(cand) < cap, cand, hi)
    lo = jnp.where(hi > F32_MIN_NORMAL, 0.5 * hi, 0.0)

    def bisect(_, lh):
        lo, hi = lh
        mid = 0.5 * (lo + hi)
        up = count_ge(mid) >= cap
        return jnp.where(up, mid, lo), jnp.where(up, hi, mid)

    lo, hi = lax.fori_loop(0, TOPK_BISECT_STEPS, bisect, (lo, hi))
    gt = a >= hi
    eq = jnp.logical_and(a >= lo, a < hi)
    need = cap - jnp.sum(gt.astype(F32), axis=1, keepdims=True)

    r = lax.broadcasted_iota(I32, (LANES, LANES), 0)
    c = lax.broadcasted_iota(I32, (LANES, LANES), 1)
    triu = (r < c).astype(BF16)
    lane = lax.broadcasted_iota(I32, (E, LANES), 1)

    nb = T // LANES
    per_tile = ROW_TILE // LANES
    carry_eq = jnp.zeros((E, 1), F32)
    carry_sel = jnp.zeros((E, 1), F32)
    cb = jnp.zeros((E, LANES), I32)
    for j in range(nb):
        sl = slice(j * LANES, (j + 1) * LANES)
        eq_j = eq[:, sl].astype(BF16)
        pre = _dot(eq_j, triu) + carry_eq
        carry_eq = carry_eq + jnp.sum(eq_j.astype(F32), axis=1, keepdims=True)
        sel = jnp.logical_or(gt[:, sl], jnp.logical_and(eq[:, sl], pre < need))
        sel_b = sel.astype(BF16)
        if j % per_tile == 0:
            cb = jnp.where(lane == j // per_tile, carry_sel.astype(I32), cb)
        slot = (_dot(sel_b, triu) + carry_sel).astype(I32)
        carry_sel = carry_sel + jnp.sum(sel_b.astype(F32), axis=1, keepdims=True)
        slot_ref[0, :, sl] = jnp.where(sel, slot, -1)
    cb_ref[0] = jnp.where(lane == nb // per_tile, carry_sel.astype(I32), cb)


def _topk(aff_t, cap):
    G, E, T = aff_t.shape
    return pl.pallas_call(
        functools.partial(_topk_body, T=T, cap=cap),
        grid=(G,),
        in_specs=[pl.BlockSpec((1, E, T), lambda g: (g, 0, 0))],
        out_specs=[pl.BlockSpec((1, E, T), lambda g: (g, 0, 0)),
                   pl.BlockSpec((1, E, LANES), lambda g: (g, 0, 0))],
        out_shape=[jax.ShapeDtypeStruct((G, E, T), I32), jax.ShapeDtypeStruct((G, E, LANES), I32)],
        compiler_params=_cparams(("arbitrary",)),
        name="expert_topk",
    )(aff_t)


def _expert_body(cb_ref, slot_ref, aff_ref, h_ref, wg_ref, wu_ref, wd_ref, y_ref, xs_ref, acc_ref, ws_ref, *,
                 cap, win, nt, nfh, E):
    g = pl.program_id(0)
    e = pl.program_id(1)
    fh = pl.program_id(2)
    base = (g * E + e) * LANES

    @pl.when(fh == 0)
    def _gather():
        xs_ref[...] = jnp.zeros_like(xs_ref)
        ws_ref[...] = jnp.zeros_like(ws_ref)
        slab = win

        def slab_start(i):
            return jnp.minimum(jnp.bitwise_and(cb_ref[base + i], -SUBLANES), cap - slab)

        def tile_fits(i, ok):
            return jnp.logical_and(ok, cb_ref[base + i + 1] <= slab_start(i) + slab)

        fits = lax.fori_loop(0, nt, tile_fits, jnp.bool_(True))

        def tile_slab(i, carry):
            r0 = pl.multiple_of(slab_start(i), SUBLANES)
            j = lax.broadcasted_iota(I32, (slab, 1), 0) + r0
            hit = slot_ref[i] == j
            hs = h_ref[pl.ds(pl.multiple_of(i * ROW_TILE, ROW_TILE), ROW_TILE), :]
            xs_ref[pl.ds(r0, slab), :] += _dot(hit.astype(BF16), hs)
            ws_ref[pl.ds(r0, slab), :] += jnp.sum(jnp.where(hit, aff_ref[i], 0.0), axis=1, keepdims=True)
            return carry

        @pl.when(fits)
        def _():
            lax.fori_loop(0, nt, tile_slab, 0, unroll=8)

        def tile(i, carry):
            lo = cb_ref[base + i]
            hi = cb_ref[base + i + 1]
            srow = slot_ref[i]
            arow = aff_ref[i]
            for w in range(cap // win):
                @pl.when(jnp.logical_and(lo < (w + 1) * win, hi > w * win))
                def _():
                    j = lax.broadcasted_iota(I32, (win, 1), 0) + w * win
                    hit = srow == j
                    hs = h_ref[pl.ds(pl.multiple_of(i * ROW_TILE, ROW_TILE), ROW_TILE), :]
                    xs_ref[w * win:(w + 1) * win, :] += _dot(hit.astype(BF16), hs)
                    ws_ref[w * win:(w + 1) * win, :] += jnp.sum(jnp.where(hit, arow, 0.0), axis=1, keepdims=True)
            return carry

        @pl.when(jnp.logical_not(fits))
        def _():
            lax.fori_loop(0, nt, tile, 0)

    xb = xs_ref[...].astype(BF16)
    gate = _dot(xb, wg_ref[...].astype(BF16))
    up = _dot(xb, wu_ref[...].astype(BF16))
    hid = (gate * _sigmoid(gate) * up).astype(BF16)
    part = _dot(hid, wd_ref[...].astype(BF16))

    @pl.when(fh == 0)
    def _():
        acc_ref[...] = part

    @pl.when(fh > 0)
    def _():
        acc_ref[...] += part

    @pl.when(fh == nfh - 1)
    def _():
        y_ref[...] = (acc_ref[...] * ws_ref[...]).astype(BF16)


def _experts(cb_flat, slot5, aff5, h2, w_gate, w_up, w_down, layer, cap, win):
    G, E, nt = slot5.shape[:3]
    T = nt * ROW_TILE
    D = h2.shape[1]
    FF = w_gate.shape[-1]
    nfh = 2
    fb = FF // nfh
    grid_spec = pltpu.PrefetchScalarGridSpec(
        num_scalar_prefetch=1,
        grid=(G, E, nfh),
        in_specs=[pl.BlockSpec((None, None, nt, 1, ROW_TILE), lambda g, e, f, cb: (g, e, 0, 0, 0)),
                  pl.BlockSpec((None, None, nt, 1, ROW_TILE), lambda g, e, f, cb: (g, e, 0, 0, 0)),
                  pl.BlockSpec((T, D), lambda g, e, f, cb: (g, 0)),
                  pl.BlockSpec((None, None, D, fb), lambda g, e, f, cb: (layer, e, 0, f)),
                  pl.BlockSpec((None, None, D, fb), lambda g, e, f, cb: (layer, e, 0, f)),
                  pl.BlockSpec((None, None, fb, D), lambda g, e, f, cb: (layer, e, f, 0))],
        out_specs=pl.BlockSpec((None, None, cap, D), lambda g, e, f, cb: (g, e, 0, 0)),
        scratch_shapes=[pltpu.VMEM((cap, D), F32), pltpu.VMEM((cap, D), F32), pltpu.VMEM((cap, 1), F32)],
    )
    return pl.pallas_call(
        functools.partial(_expert_body, cap=cap, win=win, nt=nt, nfh=nfh, E=E),
        grid_spec=grid_spec,
        out_shape=jax.ShapeDtypeStruct((G, E, cap, D), BF16),
        compiler_params=_cparams(("arbitrary", "arbitrary", "arbitrary")),
        name="expert_ffn",
    )(cb_flat, slot5, aff5, h2, w_gate, w_up, w_down)


def _combine_body(cb_ref, x_ref, mod_ref, slot_ref, y_ref, spread_ref, gf_ref, oc_ref, ol_ref, acc_ref, *,
                  D, cap, win, cw, E, tiles_per_group, final):
    i = pl.program_id(0)
    g = i // tiles_per_group
    ti = i % tiles_per_group
    slots = slot_ref[...]
    los = [cb_ref[(g * E + e) * LANES + ti] for e in range(E)]
    his = [cb_ref[(g * E + e) * LANES + ti + 1] for e in range(E)]
    starts = [jnp.minimum(jnp.bitwise_and(lo, -BF16_ROWS), cap - cw) for lo in los]
    fits = his[0] <= starts[0] + cw
    for e in range(1, E):
        fits = jnp.logical_and(fits, his[e] <= starts[e] + cw)

    @pl.when(fits)
    def _fast():
        sp1 = slots + 1
        digits = jnp.concatenate([jnp.right_shift(sp1, 4), jnp.bitwise_and(sp1, 15)], axis=1)
        spread = _dot(digits.astype(F32).astype(BF16), spread_ref[...])
        lane = lax.broadcasted_iota(I32, (1, cw), 1)
        tgt = jnp.concatenate([lane + (starts[e] + 1) for e in range(E)], axis=1).astype(F32)
        onehot = (spread == tgt).astype(BF16)
        rows = jnp.concatenate([y_ref[e, pl.ds(pl.multiple_of(starts[e], BF16_ROWS), cw), :] for e in range(E)],
                               axis=0)
        acc_ref[...] = _dot(onehot, rows)

    @pl.when(jnp.logical_not(fits))
    def _general():
        acc_ref[...] = jnp.zeros_like(acc_ref)
        for e in range(E):
            col = slots[:, e:e + 1]
            for w in range(cap // win):
                @pl.when(jnp.logical_and(los[e] < (w + 1) * win, his[e] > w * win))
                def _():
                    j = lax.broadcasted_iota(I32, (1, win), 1) + w * win
                    oh = (col == j).astype(BF16)
                    acc_ref[...] += _dot(oh, y_ref[e, w * win:(w + 1) * win, :])

    mod = mod_ref[0]
    x2 = x_ref[...] + mod[:, 5 * D:6 * D] * acc_ref[...]
    if final:
        x2 = _rms(x2, gf_ref[...])

    @pl.when(g == 0)
    def _():
        oc_ref[...] = x2

    @pl.when(g != 0)
    def _():
        ol_ref[...] = x2


def _combine(cb_flat, x1, mod_l, slot_t, y, g_final, cond_of_tile, cap, win, final):
    T2, D = x1.shape
    G, E = y.shape[:2]
    nt = T2 // ROW_TILE
    tpg = nt // G
    R = mod_l.shape[0]
    cw = min(LANES, cap)
    spread = np.zeros((2 * LANES, E * cw), np.float32)
    for e in range(E):
        spread[e, e * cw:(e + 1) * cw] = 16.0
        spread[LANES + e, e * cw:(e + 1) * cw] = 1.0
    spread = jnp.asarray(spread, BF16)
    assert G == 2
    ctx, lat = _ctx_lat_specs(tpg, tpg, ROW_TILE)
    grid_spec = pltpu.PrefetchScalarGridSpec(
        num_scalar_prefetch=1,
        grid=(nt,),
        in_specs=[pl.BlockSpec((ROW_TILE, D), lambda i, cb: (i, 0)),
                  pl.BlockSpec((1, 1, 6 * D), lambda i, cb: (cond_of_tile(i), 0, 0)),
                  pl.BlockSpec((ROW_TILE, LANES), lambda i, cb: (i, 0)),
                  pl.BlockSpec((None, E, cap, D), lambda i, cb: (i // tpg, 0, 0, 0)),
                  pl.BlockSpec(spread.shape, lambda i, cb: (0, 0)),
                  pl.BlockSpec((1, D), lambda i, cb: (0, 0))],
        out_specs=[ctx(D), lat(D)],
        scratch_shapes=[pltpu.VMEM((ROW_TILE, D), F32)],
    )
    return pl.pallas_call(
        functools.partial(_combine_body, D=D, cap=cap, win=win, cw=cw, E=E, tiles_per_group=tpg, final=final),
        grid_spec=grid_spec,
        out_shape=[jax.ShapeDtypeStruct((T2 // G, D), F32), jax.ShapeDtypeStruct((T2 // G, D), F32)],
        compiler_params=_cparams(("arbitrary",)),
        name="moe_combine",
    )(cb_flat, x1, mod_l.reshape(R, 1, 6 * D), slot_t, y, spread, g_final)


def _pack_weights(w_in, w_uq, w_uk, w_uv, w_gla_gate, b_gla_gate):
    L, D, _ = w_in.shape
    sizes = (Q_RANK, KV_RANK, MLA_ROPE, GLA_W, GLA_W, GLA_W, 2 * GLA_GATE_RANK, GLA_W, D, D)
    idx = np.cumsum(sizes)[:-1]
    pq, pkv, kr, gq, gk, gv, glr, gog, ga, gb = jnp.split(w_in, [int(i) for i in idx], axis=-1)
    npair = MLA_ROPE // 4
    swap = np.concatenate([np.arange(npair, 2 * npair), np.arange(0, npair),
                           np.arange(3 * npair, 4 * npair), np.arange(2 * npair, 3 * npair)])

    def slot_rope(w):
        return jnp.pad(w, ((0, 0), (0, 0), (MLA_NOPE, HEAD_PAD - MLA_NOPE - MLA_ROPE)))

    glr_p = jnp.pad(glr, ((0, 0), (0, 0), (0, LANES - 2 * GLA_GATE_RANK)))
    wp = jnp.concatenate([pq, pkv, slot_rope(kr), slot_rope(kr[..., swap]), gq, gk, gv, glr_p, gog],
                         axis=-1).astype(BF16)
    wgate = jnp.concatenate([ga, gb], axis=-1).astype(BF16)

    uq = w_uq.reshape(L, Q_RANK, MLA_HEADS, MLA_NOPE + MLA_ROPE)
    pad_h = HEAD_PAD - MLA_NOPE - MLA_ROPE
    uq_n = jnp.pad(uq, ((0, 0), (0, 0), (0, 0), (0, pad_h))).reshape(L, Q_RANK, MLA_HEADS * HEAD_PAD)
    uq_s = jnp.concatenate([jnp.zeros_like(uq[..., :MLA_NOPE]), uq[..., MLA_NOPE:][..., swap]], axis=-1)
    uq_s = jnp.pad(uq_s, ((0, 0), (0, 0), (0, 0), (0, pad_h))).reshape(L, Q_RANK, MLA_HEADS * HEAD_PAD)
    wuq = jnp.concatenate([uq_n, uq_s], axis=-1).astype(BF16)

    uk = w_uk.reshape(L, KV_RANK, MLA_HEADS, MLA_NOPE)
    wuk = jnp.pad(uk, ((0, 0), (0, 0), (0, 0), (0, HEAD_PAD - MLA_NOPE))).reshape(
        L, KV_RANK, MLA_HEADS * HEAD_PAD).astype(BF16)
    wuv = w_uv.astype(BF16)

    wg = jnp.zeros((L, LANES, 2 * GLA_W), F32)
    wg = wg.at[:, 0:GLA_GATE_RANK, 0:GLA_W].set(w_gla_gate[:, 0])
    wg = wg.at[:, GLA_GATE_RANK:2 * GLA_GATE_RANK, GLA_W:].set(w_gla_gate[:, 1])
    bg = b_gla_gate.reshape(L, 1, 2 * GLA_W)
    return wp, wgate, wuq, wuk, wuv, wg.astype(BF16), bg


def _rope_tables(n_lat):
    npair = MLA_ROPE // 4
    freqs = ROPE_BASE ** (-jnp.arange(npair, dtype=F32) / npair)
    pos = jnp.arange(n_lat)
    ang_r = (pos // GRID_W).astype(F32)[:, None] * freqs
    ang_c = (pos % GRID_W).astype(F32)[:, None] * freqs
    cr, sr, cc, sc = jnp.cos(ang_r), jnp.sin(ang_r), jnp.cos(ang_c), jnp.sin(ang_c)
    cos32 = jnp.concatenate([cr, cr, cc, cc], axis=-1)
    sin32 = jnp.concatenate([-sr, sr, -sc, sc], axis=-1)
    pad_h = HEAD_PAD - MLA_NOPE - MLA_ROPE
    ones = jnp.ones((n_lat, MLA_NOPE), F32)
    cos_l = jnp.concatenate([ones, cos32, jnp.zeros((n_lat, pad_h), F32)], axis=-1)
    sin_l = jnp.pad(sin32, ((0, 0), (MLA_NOPE, pad_h)))
    cos_i = jnp.concatenate([jnp.ones((TOKEN_TILE, MLA_NOPE + MLA_ROPE), F32),
                             jnp.zeros((TOKEN_TILE, pad_h), F32)], -1)
    sin_i = jnp.zeros((TOKEN_TILE, HEAD_PAD), F32)
    return jnp.concatenate([cos_i, cos_l], 0), jnp.concatenate([sin_i, sin_l], 0)


def _gla_consts():
    C = GLA_CHUNK
    r = np.arange(C)
    trif = (r[None, :] <= r[:, None]).astype(np.float32)
    trib = (r[None, :] >= r[:, None]).astype(np.float32)
    h = np.arange(GLA_QUAD) // GLA_DK
    ones_bd = (h[:, None] == h[None, :]).astype(np.float32)
    return jnp.asarray(trif, BF16), jnp.asarray(trib, BF16), jnp.asarray(ones_bd, BF16)


def kernel(x_prompt, x_sample, cache_ckv, cache_krope, state_gla, c, c_ctx, w_mod, b_mod, g_norm1, g_norm2, w_in, g_q, g_kv, w_uq, w_uk, w_uv, w_o_mla, w_gla_gate, b_gla_gate, g_gla, w_o_gla, w_out, w_router, w_e_gate, w_e_up, w_e_down, g_final):
    B, N, D = x_prompt.shape
    DB, DN, _ = x_sample.shape
    L = w_in.shape[0]
    Tc, Tl = B * N, DB * DN
    assert Tc == Tl and Tc % DN == 0 and N % ROW_TILE == 0 and DN % TOKEN_TILE == 0 and Tc % TOKEN_TILE == 0
    assert N % GLA_CHUNK == 0 and DN % GLA_CHUNK == 0 and DN % GRID_W == 0 and TOKEN_TILE == ROW_TILE
    T = Tc
    G = 2
    cap = max(1, CAPACITY_FACTOR * T // N_EXPERTS)
    win = LANES if cap % LANES == 0 else cap
    assert cap % win == 0 and win % SUBLANES == 0 and 1 + DB <= SUBLANES

    def tile_maps(rows):
        nct, per_seq = Tc // rows, DN // rows
        cond = lambda i: jnp.where(i < nct, 0, 1 + (i - nct) // per_seq)
        tab = lambda i: jnp.where(i < nct, 0, 1 + (i - nct) % per_seq)
        return cond, tab

    cond_tok, tab_tok = tile_maps(TOKEN_TILE)
    cond_row, _ = tile_maps(ROW_TILE)
    n_ctx_tiles = Tc // TOKEN_TILE

    cvec = jnp.concatenate([c_ctx[None, :], c, jnp.zeros((SUBLANES - 1 - DB, D), F32)], axis=0)
    mod = _adaln_all(cvec, w_mod, b_mod)

    wp, wgate, wuq, wuk, wuv, wg, bg = _pack_weights(w_in, w_uq, w_uk, w_uv, w_gla_gate, b_gla_gate)
    cos_t, sin_t = _rope_tables(DN)
    gla_consts = _gla_consts()
    wom = w_o_mla.astype(BF16)
    wog = w_o_gla.astype(BF16)
    wout = w_out.astype(BF16)
    wr = jnp.pad(w_router, ((0, 0), (0, 0), (0, LANES - N_EXPERTS)))
    g1s, g2s, gqs, gkvs = g_norm1[:, None, :], g_norm2[:, None, :], g_q[:, None, :], g_kv[:, None, :]

    ckr_pad = jnp.pad(cache_krope, ((0, 0), (0, 0), (0, 0), (MLA_NOPE, HEAD_PAD - MLA_NOPE - MLA_ROPE)))
    kc_all, vc_all = _cache_kv(cache_ckv, ckr_pad, wuk, wuv)
    st_t = jnp.swapaxes(state_gla, -1, -2)

    xc, xl = x_prompt.reshape(Tc, D), x_sample.reshape(Tl, D)
    ckv_list, kr_list, gla_list = [], [], []
    for l in range(L):
        pre = _inproj(xc, xl, mod, l, g1s, wp, gqs, gkvs, wuq, wuk, wuv, wg, bg, cos_t, sin_t, cond_tok, tab_tok)
        ckv_list.append(pre["ckv"][:Tc].reshape(B, N, KV_RANK))
        kr_list.append(pre["kr"][:Tc].reshape(B, N, MLA_ROPE))

        o_ctx = _attention_ctx(pre["q"], pre["k"], pre["v"], B, N)
        o_lat = _attention_lat(pre["q"], pre["k"], pre["v"], kc_all, vc_all, l, Tc, DB, DN, min(DN, TOKEN_TILE))

        gg = g_gla[l][None]
        og_ctx, s_fin = _gla(pre["gq"], pre["gk"], pre["gv"], pre["g"], pre["sg"], gg, gla_consts, 0, B, N,
                             want_final=True)
        (og_lat,) = _gla(pre["gq"], pre["gk"], pre["gv"], pre["g"], pre["sg"], gg, gla_consts, Tc, DB, DN,
                         s0=(st_t, l))
        gla_list.append(s_fin)

        x1, h2, aff = _merge(xc, xl, mod, l, o_ctx, o_lat, og_ctx, og_lat, g1s, wgate, wom, wog, wout, g2s, wr,
                             cond_tok)

        aff_t = jnp.swapaxes(aff[:, :N_EXPERTS].reshape(G, T, N_EXPERTS), 1, 2)
        slot, cb = _topk(aff_t, cap)
        cb_flat = cb.reshape(-1)
        slot5 = slot.reshape(G, N_EXPERTS, T // ROW_TILE, 1, ROW_TILE)
        slot_t = jnp.pad(jnp.swapaxes(slot, 1, 2).reshape(G * T, N_EXPERTS),
                         ((0, 0), (0, LANES - N_EXPERTS)), constant_values=-1)
        aff5 = aff_t.reshape(G, N_EXPERTS, T // ROW_TILE, 1, ROW_TILE)
        y = _experts(cb_flat, slot5, aff5, h2, w_e_gate, w_e_up, w_e_down, l, cap, win)
        xc, xl = _combine(cb_flat, x1, mod[l], slot_t, y, g_final[None], cond_row, cap, win, final=(l == L - 1))

    y_prompt = xc.reshape(B, N, D)
    y_sample = xl.reshape(DB, DN, D)
    new_ckv = jnp.stack(ckv_list, axis=1)
    new_krope = jnp.stack(kr_list, axis=1)
    new_gla = jnp.stack(gla_list, axis=1)
    return (y_prompt, y_sample, new_ckv, new_krope, new_gla)
```

```python
import functools

import jax
import jax.numpy as jnp
import numpy as np
from jax import lax
from jax.experimental import pallas as pl
from jax.experimental.pallas import tpu as pltpu

F32 = jnp.float32
BF16 = jnp.bfloat16
I32 = jnp.int32

GRID_W = 64
EPS = 1e-6
MLA_HEADS = 8
MLA_NOPE = 64
MLA_ROPE = 32
MLA_V = 64
Q_RANK = 384
KV_RANK = 256
ROPE_BASE = 10000.0
GLA_HEADS = 8
GLA_DK = 64
GLA_DV = 64
GLA_W = GLA_HEADS * GLA_DK
GLA_GATE_RANK = 16
GLA_TAU = 16.0
GLA_CHUNK = 64
N_EXPERTS = 16
CAPACITY_FACTOR = 2

LANES = 128
SUBLANES = 8
BF16_ROWS = 16
HEAD_PAD = 128
VMEM_LIMIT = 56 * 1024 * 1024

ROW_TILE = 256
TOKEN_TILE = 256
GLA_QUAD = 4 * GLA_DK
GLA_DIAG = 8
GLA_SEQS_PER_STEP = 2
GLA_MILD_LOG2 = 64.0
F32_MIN_NORMAL = 2.0 ** -126
TOPK_EXP_BITS = 7
TOPK_BISECT_STEPS = 52
GLA_DK_SHIFT = GLA_DK.bit_length() - 1
assert 1 << GLA_DK_SHIFT == GLA_DK
LOG2E = float(np.log2(np.e))
Q_PRESCALE = (MLA_NOPE + MLA_ROPE) ** -0.5 * LOG2E

_SEG = {}
_off = 0
for _name, _width in (("pq", Q_RANK), ("pkv", KV_RANK), ("kr", HEAD_PAD), ("krs", HEAD_PAD),
                      ("gq", GLA_W), ("gk", GLA_W), ("gv", GLA_W), ("glr", LANES), ("gog", GLA_W)):
    _SEG[_name] = (_off, _width)
    _off += _width


def _cparams(sem, vmem=VMEM_LIMIT):
    return pltpu.CompilerParams(dimension_semantics=sem, vmem_limit_bytes=vmem)


def _dot(a, b):
    return jnp.dot(a, b, preferred_element_type=F32)


def _dot_nt(a, b):
    return lax.dot_general(a, b, (((1,), (1,)), ((), ())), preferred_element_type=F32)


def _dot_tn(a, b):
    return lax.dot_general(a, b, (((0,), (0,)), ((), ())), preferred_element_type=F32)


def _rms(x, g):
    return x * lax.rsqrt(jnp.mean(x * x, axis=-1, keepdims=True) + EPS) * g


def _modulated_norm(x, g, scale, shift):
    return _rms(x, g) * (1.0 + scale) + shift


def _sigmoid(x):
    return 0.5 * jnp.tanh(0.5 * x) + 0.5


def _split3(x):
    a = x.astype(BF16)
    r = x - a.astype(F32)
    b = r.astype(BF16)
    c = (r - b.astype(F32)).astype(BF16)
    return a, b, c


def _mod_body(c_ref, w_ref, b_ref, o_ref):
    c = c_ref[...]
    s = (c * _sigmoid(c)).astype(BF16)
    o_ref[0] = _dot(s, w_ref[0].astype(BF16)) + b_ref[0]


def _adaln_all(cvec, w_mod, b_mod):
    L, D, D6 = w_mod.shape
    R = cvec.shape[0]
    tn = 1536
    return pl.pallas_call(
        _mod_body,
        grid=(L, D6 // tn),
        in_specs=[pl.BlockSpec((R, D), lambda l, j: (0, 0)),
                  pl.BlockSpec((1, D, tn), lambda l, j: (l, 0, j)),
                  pl.BlockSpec((1, 1, tn), lambda l, j: (l, 0, j))],
        out_specs=pl.BlockSpec((1, R, tn), lambda l, j: (l, 0, j)),
        out_shape=jax.ShapeDtypeStruct((L, R, D6), F32),
        compiler_params=_cparams(("arbitrary", "arbitrary")),
        name="adaln_mod",
    )(cvec, w_mod, b_mod.reshape(L, 1, D6))


def _inproj_body(xc_ref, xl_ref, mod_ref, g1_ref, w_ref, gq_ref, gkv_ref, wuq_ref, wuk_ref, wuv_ref,
                 wg_ref, bg_ref, ct_ref, st_ref,
                 q_ref, k_ref, v_ref, ckv_ref, kr_ref, gqo_ref, gko_ref, gvo_ref, g_ref,
                 sg_ref, *, D, n_ctx_tiles):
    is_ctx = pl.program_id(0) < n_ctx_tiles
    mod = mod_ref[0]
    x = jnp.where(is_ctx, xc_ref[...], xl_ref[...])
    h = _modulated_norm(x, g1_ref[...], mod[:, D:2 * D], mod[:, 0:D]).astype(BF16)

    split = _SEG["gq"][0]
    mla_part = _dot(h, w_ref[:, 0:split])
    gla_part = _dot(h, w_ref[:, split:])

    def seg(name):
        a, w = _SEG[name]
        return mla_part[:, a:a + w] if a < split else gla_part[:, a - split:a - split + w]

    cos = ct_ref[...]
    sin = st_ref[...]
    nq = MLA_HEADS * HEAD_PAD

    cq = _rms(seg("pq"), gq_ref[...]).astype(BF16)
    qq = _dot(cq, wuq_ref[...])
    for hd in range(MLA_HEADS):
        a = hd * HEAD_PAD
        q_ref[:, a:a + HEAD_PAD] = ((qq[:, a:a + HEAD_PAD] * cos
                                     + qq[:, nq + a:nq + a + HEAD_PAD] * sin) * Q_PRESCALE).astype(BF16)

    ckv = _rms(seg("pkv"), gkv_ref[...])
    ckv_b = ckv.astype(BF16)
    kr = seg("kr")
    ckv_ref[...] = ckv
    kr_ref[...] = kr[:, MLA_NOPE:MLA_NOPE + MLA_ROPE]
    kr_rot = kr * cos + seg("krs") * sin
    kn = _dot(ckv_b, wuk_ref[...])
    for hd in range(MLA_HEADS):
        a = hd * HEAD_PAD
        k_ref[:, a:a + HEAD_PAD] = (kn[:, a:a + HEAD_PAD] + kr_rot).astype(BF16)
    v_ref[...] = _dot(ckv_b, wuv_ref[...]).astype(BF16)

    gqo_ref[...] = (seg("gq") * (GLA_DK ** -0.5)).astype(BF16)
    gko_ref[...] = seg("gk").astype(BF16)
    gvo_ref[...] = seg("gv").astype(BF16)
    logit = _dot(seg("glr").astype(BF16), wg_ref[...]) + bg_ref[...]
    g_ref[...] = (jnp.minimum(logit, 0.0) - jnp.log1p(jnp.exp(-jnp.abs(logit)))) * (1.0 / GLA_TAU)
    gog = seg("gog")
    sg_ref[...] = (gog * _sigmoid(gog)).astype(BF16)


def _layer_spec(a, layer):
    nd = a.ndim - 1
    return pl.BlockSpec((None,) + a.shape[1:], lambda i: (layer,) + (0,) * nd, pipeline_mode=pl.Buffered(1))


def _mod_spec(R, D, layer, cond_of_tile):
    return pl.BlockSpec((1, 1, 6 * D), lambda i: (layer * R + cond_of_tile(i), 0, 0))


def _ctx_lat_specs(n_ctx_tiles, n_lat_tiles, rows=TOKEN_TILE):
    ctx = lambda w: pl.BlockSpec((rows, w), lambda i, *_: (jnp.minimum(i, n_ctx_tiles - 1), 0))
    lat = lambda w: pl.BlockSpec((rows, w), lambda i, *_: (jnp.clip(i - n_ctx_tiles, 0, n_lat_tiles - 1), 0))
    return ctx, lat


def _inproj(xc, xl, mod, layer, g1, wp, gq, gkv, wuq, wuk, wuv, wg, bg, cos_t, sin_t, cond_of_tile, tab_of_tile):
    (Tc, D), Tl = xc.shape, xl.shape[0]
    T2 = Tc + Tl
    nct, nlt = Tc // TOKEN_TILE, Tl // TOKEN_TILE
    L, R = mod.shape[:2]
    row = lambda w: pl.BlockSpec((TOKEN_TILE, w), lambda i: (i, 0))
    ctx, lat = _ctx_lat_specs(nct, nlt)
    outs = [("q", MLA_HEADS * HEAD_PAD, BF16), ("k", MLA_HEADS * HEAD_PAD, BF16),
            ("v", MLA_HEADS * MLA_V, BF16), ("ckv", KV_RANK, F32), ("kr", MLA_ROPE, F32),
            ("gq", GLA_W, BF16), ("gk", GLA_W, BF16), ("gv", GLA_W, BF16), ("g", 2 * GLA_W, F32),
            ("sg", GLA_W, BF16)]
    ctx_only = ()
    params = [g1, wp, gq, gkv, wuq, wuk, wuv, wg, bg]
    res = pl.pallas_call(
        functools.partial(_inproj_body, D=D, n_ctx_tiles=nct),
        grid=(nct + nlt,),
        in_specs=[ctx(D), lat(D), _mod_spec(R, D, layer, cond_of_tile)]
                 + [_layer_spec(a, layer) for a in params]
                 + [pl.BlockSpec((TOKEN_TILE, HEAD_PAD), lambda i: (tab_of_tile(i), 0)),
                    pl.BlockSpec((TOKEN_TILE, HEAD_PAD), lambda i: (tab_of_tile(i), 0))],
        out_specs=[ctx(w) if nm in ctx_only else row(w) for nm, w, _ in outs],
        out_shape=[jax.ShapeDtypeStruct((Tc if nm in ctx_only else T2, w), dt) for nm, w, dt in outs],
        compiler_params=_cparams(("arbitrary",)),
        name="inproj",
    )(xc, xl, mod.reshape(L * R, 1, 6 * D), *params, cos_t, sin_t)
    return dict(zip([n for n, _, _ in outs], res))


def _cache_kv_body(ckv_ref, krp_ref, wuk_ref, wuv_ref, k_ref, v_ref):
    c = ckv_ref[...].astype(BF16)
    kn = _dot(c, wuk_ref[...])
    krp = krp_ref[...]
    for hd in range(MLA_HEADS):
        a = hd * HEAD_PAD
        k_ref[:, a:a + HEAD_PAD] = (kn[:, a:a + HEAD_PAD] + krp).astype(BF16)
    v_ref[...] = _dot(c, wuv_ref[...]).astype(BF16)


def _cache_kv(cache_ckv, cache_kr_pad, wuk, wuv):
    DB, L, P, R = cache_ckv.shape
    nk = MLA_HEADS * HEAD_PAD
    nv = MLA_HEADS * MLA_V
    return pl.pallas_call(
        _cache_kv_body,
        grid=(L, DB),
        in_specs=[pl.BlockSpec((None, None, P, R), lambda l, b: (b, l, 0, 0)),
                  pl.BlockSpec((None, None, P, HEAD_PAD), lambda l, b: (b, l, 0, 0)),
                  pl.BlockSpec((None, R, nk), lambda l, b: (l, 0, 0)),
                  pl.BlockSpec((None, R, nv), lambda l, b: (l, 0, 0))],
        out_specs=[pl.BlockSpec((None, None, P, nk), lambda l, b: (l, b, 0, 0)),
                   pl.BlockSpec((None, None, P, nv), lambda l, b: (l, b, 0, 0))],
        out_shape=[jax.ShapeDtypeStruct((L, DB, P, nk), BF16),
                   jax.ShapeDtypeStruct((L, DB, P, nv), BF16)],
        compiler_params=_cparams(("arbitrary", "arbitrary")),
        name="cache_kv",
    )(cache_ckv, cache_kr_pad, wuk, wuv)


def _attn_body(*refs, nseg, npairs):
    q_ref = refs[0]
    k_refs = refs[1:1 + nseg]
    v_refs = refs[1 + nseg:1 + 2 * nseg]
    o_ref = refs[1 + 2 * nseg]
    lane = lax.broadcasted_iota(I32, (1, 2 * MLA_V), 1)
    for pr in range(npairs):
        vs = slice(pr * 2 * MLA_V, (pr + 1) * 2 * MLA_V)
        outs = []
        for hh in range(2):
            hs = slice((2 * pr + hh) * HEAD_PAD, (2 * pr + hh + 1) * HEAD_PAD)
            qh = q_ref[:, hs]
            s = [_dot_nt(qh, kr[:, hs]) for kr in k_refs]
            m = s[0].max(axis=-1, keepdims=True)
            for sj in s[1:]:
                m = jnp.maximum(m, sj.max(axis=-1, keepdims=True))
            p = [jnp.exp2(sj - m) for sj in s]
            den = p[0].sum(axis=-1, keepdims=True)
            for pj in p[1:]:
                den = den + pj.sum(axis=-1, keepdims=True)
            o = None
            for pj, vr in zip(p, v_refs):
                t = _dot(pj.astype(BF16), vr[:, vs])
                o = t if o is None else o + t
            outs.append(o * (1.0 / den))
        o_ref[:, vs] = jnp.where(lane < MLA_V, outs[0], outs[1]).astype(BF16)


def _attention_ctx(q, k, v, nseq, n):
    hp = MLA_HEADS // 2
    return pl.pallas_call(
        functools.partial(_attn_body, nseg=1, npairs=hp),
        grid=(nseq,),
        in_specs=[pl.BlockSpec((n, MLA_HEADS * HEAD_PAD), lambda b: (b, 0)),
                  pl.BlockSpec((n, MLA_HEADS * HEAD_PAD), lambda b: (b, 0)),
                  pl.BlockSpec((n, MLA_HEADS * MLA_V), lambda b: (b, 0))],
        out_specs=pl.BlockSpec((n, MLA_HEADS * MLA_V), lambda b: (b, 0)),
        out_shape=jax.ShapeDtypeStruct((nseq * n, MLA_HEADS * MLA_V), BF16),
        compiler_params=_cparams(("arbitrary",)),
        name="attn_ctx",
    )(q, k, v)


def _attention_lat(q, k, v, kc, vc, layer, row0, nseq, n, tq):
    hp = MLA_HEADS // 2
    P = kc.shape[2]
    qt = n // tq
    q0 = row0 // tq
    s0 = row0 // n
    nq, nv = MLA_HEADS * HEAD_PAD, MLA_HEADS * MLA_V
    return pl.pallas_call(
        functools.partial(_attn_body, nseg=2, npairs=hp),
        grid=(nseq, qt),
        in_specs=[pl.BlockSpec((tq, nq), lambda b, t: (q0 + b * qt + t, 0)),
                  pl.BlockSpec((None, None, P, nq), lambda b, t: (layer, b, 0, 0)),
                  pl.BlockSpec((n, nq), lambda b, t: (s0 + b, 0)),
                  pl.BlockSpec((None, None, P, nv), lambda b, t: (layer, b, 0, 0)),
                  pl.BlockSpec((n, nv), lambda b, t: (s0 + b, 0))],
        out_specs=pl.BlockSpec((tq, nv), lambda b, t: (b * qt + t, 0)),
        out_shape=jax.ShapeDtypeStruct((nseq * n, nv), BF16),
        compiler_params=_cparams(("arbitrary", "arbitrary")),
        name="attn_lat",
    )(q, kc, k, vc, v)


def _head_sums(x, ones_quad):
    return jnp.concatenate([_dot(x[:, a:a + GLA_QUAD], ones_quad) for a in range(0, GLA_W, GLA_QUAD)], axis=1)


def _gla_chunk(q, k, v, g, st_refs, fwd, tri, ones_bd, mild):
    C = GLA_CHUNK
    W = GLA_W
    nquad = W // GLA_QUAD
    rows = lax.broadcasted_iota(I32, (C, 1), 0)

    g1, g2, g3 = _split3(g * LOG2E)
    cum = _dot(tri, jnp.concatenate([g1, g2, g3], axis=1))
    cum = cum[:, 0:W] + cum[:, W:2 * W] + cum[:, 2 * W:3 * W]
    edge = C - 1 if fwd else 0
    last = cum[edge:edge + 1]
    q_in = (q * jnp.exp2(cum)).astype(BF16)
    k_st = (k * jnp.exp2(last - cum)).astype(BF16)
    v_b = v.astype(BF16)

    lane_q = lax.broadcasted_iota(I32, (1, GLA_QUAD), 1)
    head_masks = [jnp.right_shift(lane_q, GLA_DK_SHIFT) == h for h in range(GLA_QUAD // GLA_DK)]
    col_s = jnp.bitwise_and(lane_q, C - 1)

    def stack_heads(xq):
        return jnp.concatenate([jnp.where(mh, xq, jnp.zeros_like(xq)) for mh in head_masks], axis=0)

    a_acc = [jnp.zeros((C, GLA_QUAD), F32) for _ in range(nquad)]
    half = C // 2
    while half >= GLA_DIAG:
        blk = 2 * half
        pieces = []
        for p in range(C // blk):
            rr = p * blk + (half - 1 if fwd else half)
            pieces.append(jnp.broadcast_to(cum[rr:rr + 1], (blk, W)))
        ref = jnp.concatenate(pieces, axis=0) if len(pieces) > 1 else pieces[0]
        upper = jnp.bitwise_and(rows, blk - 1) >= half
        qmask = upper if fwd else jnp.logical_not(upper)
        qe = jnp.where(qmask, q * jnp.exp2(cum - ref), 0.0).astype(BF16)
        ke = jnp.where(qmask, 0.0, k * jnp.exp2(ref - cum)).astype(BF16)
        sh = blk.bit_length() - 1
        same = jnp.right_shift(rows, sh) == jnp.right_shift(col_s, sh)
        for qd in range(nquad):
            sl = slice(qd * GLA_QUAD, (qd + 1) * GLA_QUAD)
            a = _dot_nt(qe[:, sl], stack_heads(ke[:, sl]))
            a_acc[qd] = a_acc[qd] + jnp.where(same, a, 0.0)
        half //= 2

    if mild:
        nb = C // GLA_DIAG
        c3 = cum.reshape(nb, GLA_DIAG, W)
        er = 0 if fwd else GLA_DIAG - 1
        ref = jnp.broadcast_to(c3[:, er:er + 1, :], (nb, GLA_DIAG, W)).reshape(C, W)
        qe = (q * jnp.exp2(cum - ref)).astype(BF16)
        ke = (k * jnp.exp2(ref - cum)).astype(BF16)
        sh = GLA_DIAG.bit_length() - 1
        keep = jnp.right_shift(rows, sh) == jnp.right_shift(col_s, sh)
        keep = jnp.logical_and(keep, (rows >= col_s) if fwd else (rows <= col_s))
        for qd in range(nquad):
            sl = slice(qd * GLA_QUAD, (qd + 1) * GLA_QUAD)
            a = _dot_nt(qe[:, sl], stack_heads(ke[:, sl]))
            a_acc[qd] = a_acc[qd] + jnp.where(keep, a, 0.0)

    r2 =jnp.right_shift(lax.broadcasted_iota(I32, (GLA_QUAD, GLA_QUAD), 0), GLA_DK_SHIFT)
    c2 = jnp.right_shift(lax.broadcasted_iota(I32, (GLA_QUAD, GLA_QUAD), 1), GLA_DK_SHIFT)
    o_parts = []
    for qd in range(nquad):
        sl = slice(qd * GLA_QUAD, (qd + 1) * GLA_QUAD)
        st = st_refs[qd][...]
        o = _dot_nt(q_in[:, sl], st.astype(BF16))
        o = o + _dot(a_acc[qd].astype(BF16), stack_heads(v_b[:, sl]))
        o_parts.append(o)
        upd = _dot_tn(v_b[:, sl], k_st[:, sl])
        st_refs[qd][...] = st * jnp.exp2(last[:, sl]) + jnp.where(r2 == c2, upd, 0.0)
    o = jnp.concatenate(o_parts, axis=1)
    if mild:
        return o

    nb = C // GLA_DIAG
    q3 = q.reshape(nb, GLA_DIAG, W)
    k3 = k.reshape(nb, GLA_DIAG, W)
    v3 = v.reshape(nb, GLA_DIAG, W)
    c3 = cum.reshape(nb, GLA_DIAG, W)
    tl = lax.broadcasted_iota(I32, (1, GLA_DIAG, 1), 1)
    zs = []
    for s in range(GLA_DIAG):
        e = jnp.exp2(c3 - c3[:, s:s + 1, :])
        valid = (tl >= s) if fwd else (tl <= s)
        zs.append(jnp.where(valid, q3 * e * k3[:, s:s + 1, :], 0.0).reshape(C, W).astype(BF16))
    w_all = _head_sums(jnp.concatenate(zs, axis=0), ones_bd)
    od = jnp.zeros((nb, GLA_DIAG, W), F32)
    for s in range(GLA_DIAG):
        od = od + w_all[s * C:(s + 1) * C].reshape(nb, GLA_DIAG, W) * v3[:, s:s + 1, :]
    return o + od.reshape(C, W)


def _gla_body(*refs, n, spb, has_init, has_final):
    it = iter(refs)
    q_ref, k_ref, v_ref, g_ref, sg_ref, gg_ref, trif_ref, trib_ref, ones_ref = (next(it) for _ in range(9))
    s0_ref = next(it) if has_init else None
    o_ref = next(it)
    sf_ref = next(it) if has_final else None
    of_ref, ob_ref = next(it), next(it)
    nquad = GLA_W // GLA_QUAD
    st = [[[next(it) for _ in range(nquad)] for _ in range(2)] for _ in range(spb)]

    C = GLA_CHUNK
    nc = n // C
    hq = GLA_QUAD // GLA_DK
    for sq in range(spb):
        for d in range(2):
            for qd in range(nquad):
                st[sq][d][qd][...] = jnp.zeros((GLA_QUAD, GLA_QUAD), F32)
                if has_init:
                    for h in range(hq):
                        hs = slice(h * GLA_DK, (h + 1) * GLA_DK)
                        st[sq][d][qd][hs, hs] = s0_ref[sq, d, qd * hq + h]

    trif = trif_ref[...]
    trib = trib_ref[...]
    ones_bd = ones_ref[...]

    def step(i, carry, mild):
        for sq in range(spb):
            for d, (tri, acc) in enumerate(((trif, of_ref), (trib, ob_ref))):
                c = i if d == 0 else nc - 1 - i
                rs = pl.ds(pl.multiple_of(sq * n + c * C, C), C)
                gd = g_ref[rs, d * GLA_W:(d + 1) * GLA_W]
                qkv = [r[rs, :].astype(F32) for r in (q_ref, k_ref, v_ref)]
                acc[rs, :] = _gla_chunk(*qkv, gd, st[sq][d], d == 0, tri, ones_bd, mild)
        return carry

    steepest = jnp.max(-g_ref[...]) * (LOG2E * (GLA_DIAG - 1))
    is_mild = steepest < GLA_MILD_LOG2

    @pl.when(is_mild)
    def _():
        lax.fori_loop(0, nc, functools.partial(step, mild=True), 0)

    @pl.when(jnp.logical_not(is_mild))
    def _():
        lax.fori_loop(0, nc, functools.partial(step, mild=False), 0)

    gg = gg_ref[...]
    fr = min(n, 256)

    def fin(i, carry):
        rs = pl.ds(pl.multiple_of(i * fr, fr), fr)
        o = of_ref[rs, :] + ob_ref[rs, :]
        ms = _head_sums(jnp.concatenate(_split3(o * o)[:2], axis=0), ones_bd)
        ms = (ms[0:fr] + ms[fr:2 * fr]) * (1.0 / GLA_DV)
        o_ref[rs, :] = (o * lax.rsqrt(ms + EPS) * gg * sg_ref[rs, :].astype(F32)).astype(BF16)
        return carry

    lax.fori_loop(0, spb * n // fr, fin, 0)

    if has_final:
        for sq in range(spb):
            for d in range(2):
                for qd in range(nquad):
                    s = st[sq][d][qd][...].T
                    for h in range(hq):
                        sf_ref[sq, d, qd * hq + h] = s[h * GLA_DK:(h + 1) * GLA_DK, h * GLA_DK:(h + 1) * GLA_DK]


def _gla(gq, gk, gv, g, sg, g_gla, consts, row0, nseq, n, s0=None, want_final=False):
    trif, trib, ones_bd = consts
    spb = GLA_SEQS_PER_STEP if nseq % GLA_SEQS_PER_STEP == 0 and (row0 // n) % GLA_SEQS_PER_STEP == 0 else 1
    b0 = row0 // (n * spb)
    W = GLA_W
    nquad = W // GLA_QUAD
    seq = lambda w: pl.BlockSpec((spb * n, w), lambda b: (b0 + b, 0))
    full = lambda a: pl.BlockSpec(a.shape, lambda b: (0,) * a.ndim)
    in_specs = [seq(W), seq(W), seq(W), seq(2 * W), seq(W), full(g_gla), full(trif), full(trib), full(ones_bd)]
    args = [gq, gk, gv, g, sg, g_gla, trif, trib, ones_bd]
    if s0 is not None:
        s0_arr, layer = s0
        in_specs.append(pl.BlockSpec((spb, None, 2, GLA_HEADS, GLA_DV, GLA_DK), lambda b: (b, layer, 0, 0, 0, 0)))
        args.append(s0_arr)
    out_specs = [pl.BlockSpec((spb * n, W), lambda b: (b, 0))]
    out_shape = [jax.ShapeDtypeStruct((nseq * n, W), BF16)]
    if want_final:
        out_specs.append(pl.BlockSpec((spb, 2, GLA_HEADS, GLA_DK, GLA_DV), lambda b: (b, 0, 0, 0, 0)))
        out_shape.append(jax.ShapeDtypeStruct((nseq, 2, GLA_HEADS, GLA_DK, GLA_DV), F32))
    scratch = [pltpu.VMEM((spb * n, W), F32), pltpu.VMEM((spb * n, W), F32)]
    scratch += [pltpu.VMEM((GLA_QUAD, GLA_QUAD), F32) for _ in range(spb * 2 * nquad)]
    res = pl.pallas_call(
        functools.partial(_gla_body, n=n, spb=spb, has_init=s0 is not None, has_final=want_final),
        grid=(nseq // spb,),
        in_specs=in_specs,
        out_specs=out_specs,
        out_shape=out_shape,
        scratch_shapes=scratch,
        compiler_params=_cparams(("arbitrary",)),
        name="gla_lat" if s0 is not None else "gla_ctx",
    )(*args)
    return res


def _merge_body(xc_ref, xl_ref, mod_ref, oc_ref, ol_ref, gc_ref, gl_ref, g1_ref, wgate_ref, wom_ref, wog_ref,
                wout_ref, g2_ref, wr_ref, x1_ref, h2_ref, aff_ref, *, D, n_ctx_tiles):
    is_ctx = pl.program_id(0) < n_ctx_tiles
    mod = mod_ref[0]
    x = jnp.where(is_ctx, xc_ref[...], xl_ref[...])
    h = _modulated_norm(x, g1_ref[...], mod[:, D:2 * D], mod[:, 0:D]).astype(BF16)
    om = _dot(jnp.where(is_ctx, oc_ref[...], ol_ref[...]), wom_ref[...])
    merged = _sigmoid(_dot(h, wgate_ref[:, 0:D])) * om
    og = _dot(jnp.where(is_ctx, gc_ref[...], gl_ref[...]), wog_ref[...])
    merged = (merged + _sigmoid(_dot(h, wgate_ref[:, D:2 * D])) * og).astype(BF16)
    mix = _dot(merged, wout_ref[...])
    x1 = x + mod[:, 2 * D:3 * D] * mix
    x1_ref[...] = x1
    h2 = _modulated_norm(x1, g2_ref[...], mod[:, 4 * D:5 * D], mod[:, 3 * D:4 * D])
    h2_ref[...] = h2.astype(BF16)
    a, b, _ = _split3(h2)
    wa, wb, _ = _split3(wr_ref[...])
    logits = _dot(jnp.concatenate([a, a, b], axis=1), jnp.concatenate([wa, wb, wa], axis=0))
    lane = lax.broadcasted_iota(I32, logits.shape, 1)
    logits = jnp.where(lane < N_EXPERTS, logits, -jnp.inf)
    p = jnp.exp(logits - logits.max(axis=-1, keepdims=True))
    aff_ref[...] = p / p.sum(axis=-1, keepdims=True)


def _merge(xc, xl, mod, layer, o_ctx, o_lat, og_ctx, og_lat, g1, wgate, wom, wog, wout, g2, wr, cond_of_tile):
    (Tc, D), Tl = xc.shape, xl.shape[0]
    T2 = Tc + Tl
    n_ctx_tiles, n_lat_tiles = Tc // TOKEN_TILE, Tl // TOKEN_TILE
    nt = n_ctx_tiles + n_lat_tiles
    L, R = mod.shape[:2]
    row = lambda w: pl.BlockSpec((TOKEN_TILE, w), lambda i: (i, 0))
    ctx, lat = _ctx_lat_specs(n_ctx_tiles, n_lat_tiles)
    W = o_ctx.shape[1]
    params = [g1, wgate, wom, wog, wout, g2, wr]
    return pl.pallas_call(
        functools.partial(_merge_body, D=D, n_ctx_tiles=n_ctx_tiles),
        grid=(nt,),
        in_specs=[ctx(D), lat(D), _mod_spec(R, D, layer, cond_of_tile), ctx(W), lat(W), ctx(GLA_W), lat(GLA_W)]
                 + [_layer_spec(a, layer) for a in params],
        out_specs=[row(D), row(D), row(LANES)],
        out_shape=[jax.ShapeDtypeStruct((T2, D), F32), jax.ShapeDtypeStruct((T2, D), BF16),
                   jax.ShapeDtypeStruct((T2, LANES), F32)],
        compiler_params=_cparams(("arbitrary",)),
        name="merge_router",
    )(xc, xl, mod.reshape(L * R, 1, 6 * D), o_ctx, o_lat, og_ctx, og_lat, *params)


def _topk_body(a_ref, slot_ref, cb_ref, *, T, cap):
    a = a_ref[0]
    E = a.shape[0]

    def count_ge(thr):
        return jnp.sum((a >= thr).astype(F32), axis=1, keepdims=True)

    hi = jnp.full((E, 1), 2.0, F32)
    for j in range(TOPK_EXP_BITS - 1, -1, -1):
        cand = hi * (2.0 ** -(2 ** j))
        hi = jnp.where(count_ge(cand) < cap, cand, hi)
    lo = jnp.where(hi > F32_MIN_NORMAL, 0.5 * hi, 0.0)

    def bisect(_, lh):
        lo, hi = lh
        mid = 0.5 * (lo + hi)
        up = count_ge(mid) >= cap
        return jnp.where(up, mid, lo), jnp.where(up, hi, mid)

    lo, hi = lax.fori_loop(0, TOPK_BISECT_STEPS, bisect, (lo, hi))
    gt = a >= hi
    eq = jnp.logical_and(a >= lo, a < hi)
    need = cap - jnp.sum(gt.astype(F32), axis=1, keepdims=True)

    r = lax.broadcasted_iota(I32, (LANES, LANES), 0)
    c = lax.broadcasted_iota(I32, (LANES, LANES), 1)
    triu = (r < c).astype(BF16)
    lane = lax.broadcasted_iota(I32, (E, LANES), 1)

    nb = T // LANES
    per_tile = ROW_TILE // LANES
    carry_eq = jnp.zeros((E, 1), F32)
    carry_sel = jnp.zeros((E, 1), F32)
    cb = jnp.zeros((E, LANES), I32)
    for j in range(nb):
        sl = slice(j * LANES, (j + 1) * LANES)
        eq_j = eq[:, sl].astype(BF16)
        pre = _dot(eq_j, triu) + carry_eq
        carry_eq = carry_eq + jnp.sum(eq_j.astype(F32), axis=1, keepdims=True)
        sel = jnp.logical_or(gt[:, sl], jnp.logical_and(eq[:, sl], pre < need))
        sel_b = sel.astype(BF16)
        if j % per_tile == 0:
            cb = jnp.where(lane == j // per_tile, carry_sel.astype(I32), cb)
        slot = (_dot(sel_b, triu) + carry_sel).astype(I32)
        carry_sel = carry_sel + jnp.sum(sel_b.astype(F32), axis=1, keepdims=True)
        slot_ref[0, :, sl] = jnp.where(sel, slot, -1)
    cb_ref[0] = jnp.where(lane == nb // per_tile, carry_sel.astype(I32), cb)


def _topk(aff_t, cap):
    G, E, T = aff_t.shape
    return pl.pallas_call(
        functools.partial(_topk_body, T=T, cap=cap),
        grid=(G,),
        in_specs=[pl.BlockSpec((1, E, T), lambda g: (g, 0, 0))],
        out_specs=[pl.BlockSpec((1, E, T), lambda g: (g, 0, 0)),
                   pl.BlockSpec((1, E, LANES), lambda g: (g, 0, 0))],
        out_shape=[jax.ShapeDtypeStruct((G, E, T), I32), jax.ShapeDtypeStruct((G, E, LANES), I32)],
        compiler_params=_cparams(("arbitrary",)),
        name="expert_topk",
    )(aff_t)


def _expert_body(cb_ref, slot_ref, aff_ref, h_ref, wg_ref, wu_ref, wd_ref, y_ref, xs_ref, acc_ref, ws_ref, *,
                 cap, win, nt, nfh, E):
    g = pl.program_id(0)
    e = pl.program_id(1)
    fh = pl.program_id(2)
    base = (g * E + e) * LANES

    @pl.when(fh == 0)
    def _gather():
        xs_ref[...] = jnp.zeros_like(xs_ref)
        ws_ref[...] = jnp.zeros_like(ws_ref)
        slab = win

        def slab_start(i):
            return jnp.minimum(jnp.bitwise_and(cb_ref[base + i], -SUBLANES), cap - slab)

        def tile_fits(i, ok):
            return jnp.logical_and(ok, cb_ref[base + i + 1] <= slab_start(i) + slab)

        fits = lax.fori_loop(0, nt, tile_fits, jnp.bool_(True))

        def tile_slab(i, carry):
            r0 = pl.multiple_of(slab_start(i), SUBLANES)
            j = lax.broadcasted_iota(I32, (slab, 1), 0) + r0
            hit = slot_ref[i] == j
            hs = h_ref[pl.ds(pl.multiple_of(i * ROW_TILE, ROW_TILE), ROW_TILE), :]
            xs_ref[pl.ds(r0, slab), :] += _dot(hit.astype(BF16), hs)
            ws_ref[pl.ds(r0, slab), :] += jnp.sum(jnp.where(hit, aff_ref[i], 0.0), axis=1, keepdims=True)
            return carry

        @pl.when(fits)
        def _():
            lax.fori_loop(0, nt, tile_slab, 0, unroll=8)

        def tile(i, carry):
            lo = cb_ref[base + i]
            hi = cb_ref[base + i + 1]
            srow = slot_ref[i]
            arow = aff_ref[i]
            for w in range(cap // win):
                @pl.when(jnp.logical_and(lo < (w + 1) * win, hi > w * win))
                def _():
                    j = lax.broadcasted_iota(I32, (win, 1), 0) + w * win
                    hit = srow == j
                    hs = h_ref[pl.ds(pl.multiple_of(i * ROW_TILE, ROW_TILE), ROW_TILE), :]
                    xs_ref[w * win:(w + 1) * win, :] += _dot(hit.astype(BF16), hs)
                    ws_ref[w * win:(w + 1) * win, :] += jnp.sum(jnp.where(hit, arow, 0.0), axis=1, keepdims=True)
            return carry

        @pl.when(jnp.logical_not(fits))
        def _():
            lax.fori_loop(0, nt, tile, 0)

    xb = xs_ref[...].astype(BF16)
    gate = _dot(xb, wg_ref[...].astype(BF16))
    up = _dot(xb, wu_ref[...].astype(BF16))
    hid = (gate * _sigmoid(gate) * up).astype(BF16)
    part = _dot(hid, wd_ref[...].astype(BF16))

    @pl.when(fh == 0)
    def _():
        acc_ref[...] = part

    @pl.when(fh > 0)
    def _():
        acc_ref[...] += part

    @pl.when(fh == nfh - 1)
    def _():
        y_ref[...] = (acc_ref[...] * ws_ref[...]).astype(BF16)


def _experts(cb_flat, slot5, aff5, h2, w_gate, w_up, w_down, layer, cap, win):
    G, E, nt = slot5.shape[:3]
    T = nt * ROW_TILE
    D = h2.shape[1]
    FF = w_gate.shape[-1]
    nfh = 1
    fb = FF // nfh
    grid_spec = pltpu.PrefetchScalarGridSpec(
        num_scalar_prefetch=1,
        grid=(G, E, nfh),
        in_specs=[pl.BlockSpec((None, None, nt, 1, ROW_TILE), lambda g, e, f, cb: (g, e, 0, 0, 0)),
                  pl.BlockSpec((None, None, nt, 1, ROW_TILE), lambda g, e, f, cb: (g, e, 0, 0, 0)),
                  pl.BlockSpec((T, D), lambda g, e, f, cb: (g, 0), pipeline_mode=pl.Buffered(1)),
                  pl.BlockSpec((None, None, D, fb), lambda g, e, f, cb: (layer, e, 0, f)),
                  pl.BlockSpec((None, None, D, fb), lambda g, e, f, cb: (layer, e, 0, f)),
                  pl.BlockSpec((None, None, fb, D), lambda g, e, f, cb: (layer, e, f, 0))],
        out_specs=pl.BlockSpec((None, None, cap, D), lambda g, e, f, cb: (g, e, 0, 0)),
        scratch_shapes=[pltpu.VMEM((cap, D), F32), pltpu.VMEM((cap, D), F32), pltpu.VMEM((cap, 1), F32)],
    )
    return pl.pallas_call(
        functools.partial(_expert_body, cap=cap, win=win, nt=nt, nfh=nfh, E=E),
        grid_spec=grid_spec,
        out_shape=jax.ShapeDtypeStruct((G, E, cap, D), BF16),
        compiler_params=_cparams(("arbitrary", "arbitrary", "arbitrary")),
        name="expert_ffn",
    )(cb_flat, slot5, aff5, h2, w_gate, w_up, w_down)


def _combine_body(cb_ref, x_ref, mod_ref, slot_ref, y_ref, spread_ref, gf_ref, oc_ref, ol_ref, acc_ref, *,
                  D, cap, win, cw, E, tiles_per_group, final):
    i = pl.program_id(0)
    g = i // tiles_per_group
    ti = i % tiles_per_group
    slots = slot_ref[...]
    los = [cb_ref[(g * E + e) * LANES + ti] for e in range(E)]
    his = [cb_ref[(g * E + e) * LANES + ti + 1] for e in range(E)]
    starts = [jnp.minimum(jnp.bitwise_and(lo, -BF16_ROWS), cap - cw) for lo in los]
    fits = his[0] <= starts[0] + cw
    for e in range(1, E):
        fits = jnp.logical_and(fits, his[e] <= starts[e] + cw)

    @pl.when(fits)
    def _fast():
        sp1 = slots + 1
        digits = jnp.concatenate([jnp.right_shift(sp1, 4), jnp.bitwise_and(sp1, 15)], axis=1)
        spread = _dot(digits.astype(F32).astype(BF16), spread_ref[...])
        lane = lax.broadcasted_iota(I32, (1, cw), 1)
        tgt = jnp.concatenate([lane + (starts[e] + 1) for e in range(E)], axis=1).astype(F32)
        onehot = (spread == tgt).astype(BF16)
        rows = jnp.concatenate([y_ref[e, pl.ds(pl.multiple_of(starts[e], BF16_ROWS), cw), :] for e in range(E)],
                               axis=0)
        acc_ref[...] = _dot(onehot, rows)

    @pl.when(jnp.logical_not(fits))
    def _general():
        acc_ref[...] = jnp.zeros_like(acc_ref)
        for e in range(E):
            col = slots[:, e:e + 1]
            for w in range(cap // win):
                @pl.when(jnp.logical_and(los[e] < (w + 1) * win, his[e] > w * win))
                def _():
                    j = lax.broadcasted_iota(I32, (1, win), 1) + w * win
                    oh = (col == j).astype(BF16)
                    acc_ref[...] += _dot(oh, y_ref[e, w * win:(w + 1) * win, :])

    mod = mod_ref[0]
    x2 = x_ref[...] + mod[:, 5 * D:6 * D] * acc_ref[...]
    if final:
        x2 = _rms(x2, gf_ref[...])

    @pl.when(g == 0)
    def _():
        oc_ref[...] = x2

    @pl.when(g != 0)
    def _():
        ol_ref[...] = x2


def _combine(cb_flat, x1, mod_l, slot_t, y, g_final, cond_of_tile, cap, win, final):
    T2, D = x1.shape
    G, E = y.shape[:2]
    nt = T2 // ROW_TILE
    tpg = nt // G
    R = mod_l.shape[0]
    cw = min(LANES, cap)
    spread = np.zeros((2 * LANES, E * cw), np.float32)
    for e in range(E):
        spread[e, e * cw:(e + 1) * cw] = 16.0
        spread[LANES + e, e * cw:(e + 1) * cw] = 1.0
    spread = jnp.asarray(spread, BF16)
    assert G == 2
    ctx, lat = _ctx_lat_specs(tpg, tpg, ROW_TILE)
    grid_spec = pltpu.PrefetchScalarGridSpec(
        num_scalar_prefetch=1,
        grid=(nt,),
        in_specs=[pl.BlockSpec((ROW_TILE, D), lambda i, cb: (i, 0)),
                  pl.BlockSpec((1, 1, 6 * D), lambda i, cb: (cond_of_tile(i), 0, 0)),
                  pl.BlockSpec((ROW_TILE, LANES), lambda i, cb: (i, 0)),
                  pl.BlockSpec((None, E, cap, D), lambda i, cb: (i // tpg, 0, 0, 0)),
                  pl.BlockSpec(spread.shape, lambda i, cb: (0, 0)),
                  pl.BlockSpec((1, D), lambda i, cb: (0, 0))],
        out_specs=[ctx(D), lat(D)],
        scratch_shapes=[pltpu.VMEM((ROW_TILE, D), F32)],
    )
    return pl.pallas_call(
        functools.partial(_combine_body, D=D, cap=cap, win=win, cw=cw, E=E, tiles_per_group=tpg, final=final),
        grid_spec=grid_spec,
        out_shape=[jax.ShapeDtypeStruct((T2 // G, D), F32), jax.ShapeDtypeStruct((T2 // G, D), F32)],
        compiler_params=_cparams(("arbitrary",)),
        name="moe_combine",
    )(cb_flat, x1, mod_l.reshape(R, 1, 6 * D), slot_t, y, spread, g_final)


def _pack_weights(w_in, w_uq, w_uk, w_uv, w_gla_gate, b_gla_gate):
    L, D, _ = w_in.shape
    sizes = (Q_RANK, KV_RANK, MLA_ROPE, GLA_W, GLA_W, GLA_W, 2 * GLA_GATE_RANK, GLA_W, D, D)
    idx = np.cumsum(sizes)[:-1]
    pq, pkv, kr, gq, gk, gv, glr, gog, ga, gb = jnp.split(w_in, [int(i) for i in idx], axis=-1)
    npair = MLA_ROPE // 4
    swap = np.concatenate([np.arange(npair, 2 * npair), np.arange(0, npair),
                           np.arange(3 * npair, 4 * npair), np.arange(2 * npair, 3 * npair)])

    def slot_rope(w):
        return jnp.pad(w, ((0, 0), (0, 0), (MLA_NOPE, HEAD_PAD - MLA_NOPE - MLA_ROPE)))

    glr_p = jnp.pad(glr, ((0, 0), (0, 0), (0, LANES - 2 * GLA_GATE_RANK)))
    wp = jnp.concatenate([pq, pkv, slot_rope(kr), slot_rope(kr[..., swap]), gq, gk, gv, glr_p, gog],
                         axis=-1).astype(BF16)
    wgate = jnp.concatenate([ga, gb], axis=-1).astype(BF16)

    uq = w_uq.reshape(L, Q_RANK, MLA_HEADS, MLA_NOPE + MLA_ROPE)
    pad_h = HEAD_PAD - MLA_NOPE - MLA_ROPE
    uq_n = jnp.pad(uq, ((0, 0), (0, 0), (0, 0), (0, pad_h))).reshape(L, Q_RANK, MLA_HEADS * HEAD_PAD)
    uq_s = jnp.concatenate([jnp.zeros_like(uq[..., :MLA_NOPE]), uq[..., MLA_NOPE:][..., swap]], axis=-1)
    uq_s = jnp.pad(uq_s, ((0, 0), (0, 0), (0, 0), (0, pad_h))).reshape(L, Q_RANK, MLA_HEADS * HEAD_PAD)
    wuq = jnp.concatenate([uq_n, uq_s], axis=-1).astype(BF16)

    uk = w_uk.reshape(L, KV_RANK, MLA_HEADS, MLA_NOPE)
    wuk = jnp.pad(uk, ((0, 0), (0, 0), (0, 0), (0, HEAD_PAD - MLA_NOPE))).reshape(
        L, KV_RANK, MLA_HEADS * HEAD_PAD).astype(BF16)
    wuv = w_uv.astype(BF16)

    wg = jnp.zeros((L, LANES, 2 * GLA_W), F32)
    wg = wg.at[:, 0:GLA_GATE_RANK, 0:GLA_W].set(w_gla_gate[:, 0])
    wg = wg.at[:, GLA_GATE_RANK:2 * GLA_GATE_RANK, GLA_W:].set(w_gla_gate[:, 1])
    bg = b_gla_gate.reshape(L, 1, 2 * GLA_W)
    return wp, wgate, wuq, wuk, wuv, wg.astype(BF16), bg


def _rope_tables(n_lat):
    npair = MLA_ROPE // 4
    freqs = ROPE_BASE ** (-jnp.arange(npair, dtype=F32) / npair)
    pos = jnp.arange(n_lat)
    ang_r = (pos // GRID_W).astype(F32)[:, None] * freqs
    ang_c = (pos % GRID_W).astype(F32)[:, None] * freqs
    cr, sr, cc, sc = jnp.cos(ang_r), jnp.sin(ang_r), jnp.cos(ang_c), jnp.sin(ang_c)
    cos32 = jnp.concatenate([cr, cr, cc, cc], axis=-1)
    sin32 = jnp.concatenate([-sr, sr, -sc, sc], axis=-1)
    pad_h = HEAD_PAD - MLA_NOPE - MLA_ROPE
    ones = jnp.ones((n_lat, MLA_NOPE), F32)
    cos_l = jnp.concatenate([ones, cos32, jnp.zeros((n_lat, pad_h), F32)], axis=-1)
    sin_l = jnp.pad(sin32, ((0, 0), (MLA_NOPE, pad_h)))
    cos_i = jnp.concatenate([jnp.ones((TOKEN_TILE, MLA_NOPE + MLA_ROPE), F32),
                             jnp.zeros((TOKEN_TILE, pad_h), F32)], -1)
    sin_i = jnp.zeros((TOKEN_TILE, HEAD_PAD), F32)
    return jnp.concatenate([cos_i, cos_l], 0), jnp.concatenate([sin_i, sin_l], 0)


def _gla_consts():
    C = GLA_CHUNK
    r = np.arange(C)
    trif = (r[None, :] <= r[:, None]).astype(np.float32)
    trib = (r[None, :] >= r[:, None]).astype(np.float32)
    h = np.arange(GLA_QUAD) // GLA_DK
    ones_bd = (h[:, None] == h[None, :]).astype(np.float32)
    return jnp.asarray(trif, BF16), jnp.asarray(trib, BF16), jnp.asarray(ones_bd, BF16)


def kernel(x_prompt, x_sample, cache_ckv, cache_krope, state_gla, c, c_ctx, w_mod, b_mod, g_norm1, g_norm2, w_in, g_q, g_kv, w_uq, w_uk, w_uv, w_o_mla, w_gla_gate, b_gla_gate, g_gla, w_o_gla, w_out, w_router, w_e_gate, w_e_up, w_e_down, g_final):
    B, N, D = x_prompt.shape
    DB, DN, _ = x_sample.shape
    L = w_in.shape[0]
    Tc, Tl = B * N, DB * DN
    assert Tc == Tl and Tc % DN == 0 and N % ROW_TILE == 0 and DN % TOKEN_TILE == 0 and Tc % TOKEN_TILE == 0
    assert N % GLA_CHUNK == 0 and DN % GLA_CHUNK == 0 and DN % GRID_W == 0 and TOKEN_TILE == ROW_TILE
    T = Tc
    G = 2
    cap = max(1, CAPACITY_FACTOR * T // N_EXPERTS)
    win = LANES if cap % LANES == 0 else cap
    assert cap % win == 0 and win % SUBLANES == 0 and 1 + DB <= SUBLANES

    def tile_maps(rows):
        nct, per_seq = Tc // rows, DN // rows
        cond = lambda i: jnp.where(i < nct, 0, 1 + (i - nct) // per_seq)
        tab = lambda i: jnp.where(i < nct, 0, 1 + (i - nct) % per_seq)
        return cond, tab

    cond_tok, tab_tok = tile_maps(TOKEN_TILE)
    cond_row, _ = tile_maps(ROW_TILE)
    n_ctx_tiles = Tc // TOKEN_TILE

    cvec = jnp.concatenate([c_ctx[None, :], c, jnp.zeros((SUBLANES - 1 - DB, D), F32)], axis=0)
    mod = _adaln_all(cvec, w_mod, b_mod)

    wp, wgate, wuq, wuk, wuv, wg, bg = _pack_weights(w_in, w_uq, w_uk, w_uv, w_gla_gate, b_gla_gate)
    cos_t, sin_t = _rope_tables(DN)
    gla_consts = _gla_consts()
    wom = w_o_mla.astype(BF16)
    wog = w_o_gla.astype(BF16)
    wout = w_out.astype(BF16)
    wr = jnp.pad(w_router, ((0, 0), (0, 0), (0, LANES - N_EXPERTS)))
    g1s, g2s, gqs, gkvs = g_norm1[:, None, :], g_norm2[:, None, :], g_q[:, None, :], g_kv[:, None, :]

    ckr_pad = jnp.pad(cache_krope, ((0, 0), (0, 0), (0, 0), (MLA_NOPE, HEAD_PAD - MLA_NOPE - MLA_ROPE)))
    kc_all, vc_all = _cache_kv(cache_ckv, ckr_pad, wuk, wuv)
    st_t = jnp.swapaxes(state_gla, -1, -2)

    xc, xl = x_prompt.reshape(Tc, D), x_sample.reshape(Tl, D)
    ckv_list, kr_list, gla_list = [], [], []
    for l in range(L):
        pre = _inproj(xc, xl, mod, l, g1s, wp, gqs, gkvs, wuq, wuk, wuv, wg, bg, cos_t, sin_t, cond_tok, tab_tok)
        ckv_list.append(pre["ckv"][:Tc].reshape(B, N, KV_RANK))
        kr_list.append(pre["kr"][:Tc].reshape(B, N, MLA_ROPE))

        o_ctx = _attention_ctx(pre["q"], pre["k"], pre["v"], B, N)
        o_lat = _attention_lat(pre["q"], pre["k"], pre["v"], kc_all, vc_all, l, Tc, DB, DN, min(DN, 2 * TOKEN_TILE))

        gg = g_gla[l][None]
        og_ctx, s_fin = _gla(pre["gq"], pre["gk"], pre["gv"], pre["g"], pre["sg"], gg, gla_consts, 0, B, N,
                             want_final=True)
        (og_lat,) = _gla(pre["gq"], pre["gk"], pre["gv"], pre["g"], pre["sg"], gg, gla_consts, Tc, DB, DN,
                         s0=(st_t, l))
        gla_list.append(s_fin)

        x1, h2, aff = _merge(xc, xl, mod, l, o_ctx, o_lat, og_ctx, og_lat, g1s, wgate, wom, wog, wout, g2s, wr,
                             cond_tok)

        aff_t = jnp.swapaxes(aff[:, :N_EXPERTS].reshape(G, T, N_EXPERTS), 1, 2)
        slot, cb = _topk(aff_t, cap)
        cb_flat = cb.reshape(-1)
        slot5 = slot.reshape(G, N_EXPERTS, T // ROW_TILE, 1, ROW_TILE)
        slot_t = jnp.pad(jnp.swapaxes(slot, 1, 2).reshape(G * T, N_EXPERTS),
                         ((0, 0), (0, LANES - N_EXPERTS)), constant_values=-1)
        aff5 = aff_t.reshape(G, N_EXPERTS, T // ROW_TILE, 1, ROW_TILE)
        y = _experts(cb_flat, slot5, aff5, h2, w_e_gate, w_e_up, w_e_down, l, cap, win)
        xc, xl = _combine(cb_flat, x1, mod[l], slot_t, y, g_final[None], cond_row, cap, win, final=(l == L - 1))

    y_prompt = xc.reshape(B, N, D)
    y_sample = xl.reshape(DB, DN, D)
    new_ckv = jnp.stack(ckv_list, axis=1)
    new_krope = jnp.stack(kr_list, axis=1)
    new_gla = jnp.stack(gla_list, axis=1)
    return (y_prompt, y_sample, new_ckv, new_krope, new_gla)
```

```python
import functools

import jax
import jax.numpy as jnp
import numpy as np
from jax import lax
from jax.experimental import pallas as pl
from jax.experimental.pallas import tpu as pltpu

F32 = jnp.float32
BF16 = jnp.bfloat16
I32 = jnp.int32

GRID_W = 64
EPS = 1e-6
MLA_HEADS = 8
MLA_NOPE = 64
MLA_ROPE = 32
MLA_V = 64
Q_RANK = 384
KV_RANK = 256
ROPE_BASE = 10000.0
GLA_HEADS = 8
GLA_DK = 64
GLA_DV = 64
GLA_W = GLA_HEADS * GLA_DK
GLA_GATE_RANK = 16
GLA_TAU = 16.0
GLA_CHUNK = 64
N_EXPERTS = 16
CAPACITY_FACTOR = 2

LANES = 128
SUBLANES = 8
BF16_ROWS = 16
HEAD_PAD = 128
VMEM_LIMIT = 56 * 1024 * 1024

ROW_TILE = 256
TOKEN_TILE = 256
INPROJ_TILE = 512
GLA_QUAD = 4 * GLA_DK
GLA_DIAG = 8
GLA_ROWS_PER_STEP = 2048
GLA_SEQS_PER_STEP = 2
GLA_MILD_LOG2 = 64.0
F32_MIN_NORMAL = 2.0 ** -126
TOPK_EXP_BITS = 7
TOPK_BISECT_STEPS = 52
GLA_DK_SHIFT = GLA_DK.bit_length() - 1
assert 1 << GLA_DK_SHIFT == GLA_DK
LOG2E = float(np.log2(np.e))
Q_PRESCALE = (MLA_NOPE + MLA_ROPE) ** -0.5 * LOG2E

_SEG = {}
_off = 0
for _name, _width in (("pq", Q_RANK), ("pkv", KV_RANK), ("kr", HEAD_PAD), ("krs", HEAD_PAD),
                      ("gq", GLA_W), ("gk", GLA_W), ("gv", GLA_W), ("glr", LANES), ("gog", GLA_W)):
    _SEG[_name] = (_off, _width)
    _off += _width


def _cparams(sem, vmem=VMEM_LIMIT):
    return pltpu.CompilerParams(dimension_semantics=sem, vmem_limit_bytes=vmem)


def _dot(a, b):
    return jnp.dot(a, b, preferred_element_type=F32)


def _dot_nt(a, b):
    return lax.dot_general(a, b, (((1,), (1,)), ((), ())), preferred_element_type=F32)


def _dot_tn(a, b):
    return lax.dot_general(a, b, (((0,), (0,)), ((), ())), preferred_element_type=F32)


def _rms(x, g):
    return x * lax.rsqrt(jnp.mean(x * x, axis=-1, keepdims=True) + EPS) * g


def _modulated_norm(x, g, scale, shift):
    return _rms(x, g) * (1.0 + scale) + shift


def _sigmoid(x):
    return 0.5 * jnp.tanh(0.5 * x) + 0.5


def _split3(x):
    a = x.astype(BF16)
    r = x - a.astype(F32)
    b = r.astype(BF16)
    c = (r - b.astype(F32)).astype(BF16)
    return a, b, c


def _mod_body(c_ref, w_ref, b_ref, o_ref):
    c = c_ref[...]
    s = (c * _sigmoid(c)).astype(BF16)
    o_ref[0] = _dot(s, w_ref[0].astype(BF16)) + b_ref[0]


def _adaln_all(cvec, w_mod, b_mod):
    L, D, D6 = w_mod.shape
    R = cvec.shape[0]
    tn = 1536
    return pl.pallas_call(
        _mod_body,
        grid=(L, D6 // tn),
        in_specs=[pl.BlockSpec((R, D), lambda l, j: (0, 0)),
                  pl.BlockSpec((1, D, tn), lambda l, j: (l, 0, j)),
                  pl.BlockSpec((1, 1, tn), lambda l, j: (l, 0, j))],
        out_specs=pl.BlockSpec((1, R, tn), lambda l, j: (l, 0, j)),
        out_shape=jax.ShapeDtypeStruct((L, R, D6), F32),
        compiler_params=_cparams(("arbitrary", "arbitrary")),
        name="adaln_mod",
    )(cvec, w_mod, b_mod.reshape(L, 1, D6))


def _inproj_body(xc_ref, xl_ref, mod_ref, g1_ref, w_ref, gq_ref, gkv_ref, wuq_ref, wuk_ref, wuv_ref,
                 wg_ref, bg_ref, ct_ref, st_ref,
                 q_ref, k_ref, v_ref, ckv_ref, kr_ref, gqo_ref, gko_ref, gvo_ref, g_ref,
                 sg_ref, *, D, n_ctx_tiles):
    is_ctx = pl.program_id(0) < n_ctx_tiles
    mod = mod_ref[0]
    x = jnp.where(is_ctx, xc_ref[...], xl_ref[...])
    h = _modulated_norm(x, g1_ref[...], mod[:, D:2 * D], mod[:, 0:D]).astype(BF16)

    split = _SEG["gq"][0]
    mla_part = _dot(h, w_ref[:, 0:split])
    gla_part = _dot(h, w_ref[:, split:])

    def seg(name):
        a, w = _SEG[name]
        return mla_part[:, a:a + w] if a < split else gla_part[:, a - split:a - split + w]

    cos = ct_ref[...]
    sin = st_ref[...]
    nq = MLA_HEADS * HEAD_PAD

    cq = _rms(seg("pq"), gq_ref[...]).astype(BF16)
    qq = _dot(cq, wuq_ref[...])
    for hd in range(MLA_HEADS):
        a = hd * HEAD_PAD
        q_ref[:, a:a + HEAD_PAD] = ((qq[:, a:a + HEAD_PAD] * cos
                                     + qq[:, nq + a:nq + a + HEAD_PAD] * sin) * Q_PRESCALE).astype(BF16)

    ckv = _rms(seg("pkv"), gkv_ref[...])
    ckv_b = ckv.astype(BF16)
    kr = seg("kr")
    ckv_ref[...] = ckv
    kr_ref[...] = kr[:, MLA_NOPE:MLA_NOPE + MLA_ROPE]
    kr_rot = kr * cos + seg("krs") * sin
    kn = _dot(ckv_b, wuk_ref[...])
    for hd in range(MLA_HEADS):
        a = hd * HEAD_PAD
        k_ref[:, a:a + HEAD_PAD] = (kn[:, a:a + HEAD_PAD] + kr_rot).astype(BF16)
    v_ref[...] = _dot(ckv_b, wuv_ref[...]).astype(BF16)

    gqo_ref[...] = (seg("gq") * (GLA_DK ** -0.5)).astype(BF16)
    gko_ref[...] = seg("gk").astype(BF16)
    gvo_ref[...] = seg("gv").astype(BF16)
    logit = _dot(seg("glr").astype(BF16), wg_ref[...]) + bg_ref[...]
    g_ref[...] = (jnp.minimum(logit, 0.0) - jnp.log1p(jnp.exp(-jnp.abs(logit)))) * (1.0 / GLA_TAU)
    gog = seg("gog")
    sg_ref[...] = (gog * _sigmoid(gog)).astype(BF16)


def _layer_spec(a, layer):
    nd = a.ndim - 1
    return pl.BlockSpec((None,) + a.shape[1:], lambda i: (layer,) + (0,) * nd, pipeline_mode=pl.Buffered(1))


def _mod_spec(R, D, layer, cond_of_tile):
    return pl.BlockSpec((1, 1, 6 * D), lambda i: (layer * R + cond_of_tile(i), 0, 0))


def _ctx_lat_specs(n_ctx_tiles, n_lat_tiles, rows=TOKEN_TILE):
    ctx = lambda w: pl.BlockSpec((rows, w), lambda i, *_: (jnp.minimum(i, n_ctx_tiles - 1), 0))
    lat = lambda w: pl.BlockSpec((rows, w), lambda i, *_: (jnp.clip(i - n_ctx_tiles, 0, n_lat_tiles - 1), 0))
    return ctx, lat


def _inproj(xc, xl, mod, layer, g1, wp, gq, gkv, wuq, wuk, wuv, wg, bg, cos_t, sin_t, cond_of_tile, tab_of_tile):
    (Tc, D), Tl = xc.shape, xl.shape[0]
    T2 = Tc + Tl
    nct, nlt = Tc // INPROJ_TILE, Tl // INPROJ_TILE
    L, R = mod.shape[:2]
    row = lambda w: pl.BlockSpec((INPROJ_TILE, w), lambda i: (i, 0))
    ctx, lat = _ctx_lat_specs(nct, nlt, INPROJ_TILE)
    outs = [("q", MLA_HEADS * HEAD_PAD, BF16), ("k", MLA_HEADS * HEAD_PAD, BF16),
            ("v", MLA_HEADS * MLA_V, BF16), ("ckv", KV_RANK, F32), ("kr", MLA_ROPE, F32),
            ("gq", GLA_W, BF16), ("gk", GLA_W, BF16), ("gv", GLA_W, BF16), ("g", 2 * GLA_W, F32),
            ("sg", GLA_W, BF16)]
    ctx_only = ()
    params = [g1, wp, gq, gkv, wuq, wuk, wuv, wg, bg]
    res = pl.pallas_call(
        functools.partial(_inproj_body, D=D, n_ctx_tiles=nct),
        grid=(nct + nlt,),
        in_specs=[ctx(D), lat(D), _mod_spec(R, D, layer, cond_of_tile)]
                 + [_layer_spec(a, layer) for a in params]
                 + [pl.BlockSpec((INPROJ_TILE, HEAD_PAD), lambda i: (tab_of_tile(i), 0)),
                    pl.BlockSpec((INPROJ_TILE, HEAD_PAD), lambda i: (tab_of_tile(i), 0))],
        out_specs=[ctx(w) if nm in ctx_only else row(w) for nm, w, _ in outs],
        out_shape=[jax.ShapeDtypeStruct((Tc if nm in ctx_only else T2, w), dt) for nm, w, dt in outs],
        compiler_params=_cparams(("arbitrary",)),
        name="inproj",
    )(xc, xl, mod.reshape(L * R, 1, 6 * D), *params, cos_t, sin_t)
    return dict(zip([n for n, _, _ in outs], res))


def _cache_kv_body(ckv_ref, krp_ref, wuk_ref, wuv_ref, k_ref, v_ref):
    c = ckv_ref[...].astype(BF16)
    kn = _dot(c, wuk_ref[...])
    krp = krp_ref[...]
    for hd in range(MLA_HEADS):
        a = hd * HEAD_PAD
        k_ref[:, a:a + HEAD_PAD] = (kn[:, a:a + HEAD_PAD] + krp).astype(BF16)
    v_ref[...] = _dot(c, wuv_ref[...]).astype(BF16)


def _cache_kv(cache_ckv, cache_kr_pad, wuk, wuv):
    DB, L, P, R = cache_ckv.shape
    nk = MLA_HEADS * HEAD_PAD
    nv = MLA_HEADS * MLA_V
    return pl.pallas_call(
        _cache_kv_body,
        grid=(L, DB),
        in_specs=[pl.BlockSpec((None, None, P, R), lambda l, b: (b, l, 0, 0)),
                  pl.BlockSpec((None, None, P, HEAD_PAD), lambda l, b: (b, l, 0, 0)),
                  pl.BlockSpec((None, R, nk), lambda l, b: (l, 0, 0)),
                  pl.BlockSpec((None, R, nv), lambda l, b: (l, 0, 0))],
        out_specs=[pl.BlockSpec((None, None, P, nk), lambda l, b: (l, b, 0, 0)),
                   pl.BlockSpec((None, None, P, nv), lambda l, b: (l, b, 0, 0))],
        out_shape=[jax.ShapeDtypeStruct((L, DB, P, nk), BF16),
                   jax.ShapeDtypeStruct((L, DB, P, nv), BF16)],
        compiler_params=_cparams(("arbitrary", "arbitrary")),
        name="cache_kv",
    )(cache_ckv, cache_kr_pad, wuk, wuv)


def _attn_body(*refs, nseg, npairs):
    q_ref = refs[0]
    k_refs = refs[1:1 + nseg]
    v_refs = refs[1 + nseg:1 + 2 * nseg]
    o_ref = refs[1 + 2 * nseg]
    lane = lax.broadcasted_iota(I32, (1, 2 * MLA_V), 1)
    for pr in range(npairs):
        vs = slice(pr * 2 * MLA_V, (pr + 1) * 2 * MLA_V)
        outs = []
        for hh in range(2):
            hs = slice((2 * pr + hh) * HEAD_PAD, (2 * pr + hh + 1) * HEAD_PAD)
            qh = q_ref[:, hs]
            s = [_dot_nt(qh, kr[:, hs]) for kr in k_refs]
            m = s[0].max(axis=-1, keepdims=True)
            for sj in s[1:]:
                m = jnp.maximum(m, sj.max(axis=-1, keepdims=True))
            p = [jnp.exp2(sj - m) for sj in s]
            den = p[0].sum(axis=-1, keepdims=True)
            for pj in p[1:]:
                den = den + pj.sum(axis=-1, keepdims=True)
            o = None
            for pj, vr in zip(p, v_refs):
                t = _dot(pj.astype(BF16), vr[:, vs])
                o = t if o is None else o + t
            outs.append(o * (1.0 / den))
        o_ref[:, vs] = jnp.where(lane < MLA_V, outs[0], outs[1]).astype(BF16)


def _attention_ctx(q, k, v, nseq, n):
    hp = MLA_HEADS // 2
    return pl.pallas_call(
        functools.partial(_attn_body, nseg=1, npairs=hp),
        grid=(nseq,),
        in_specs=[pl.BlockSpec((n, MLA_HEADS * HEAD_PAD), lambda b: (b, 0)),
                  pl.BlockSpec((n, MLA_HEADS * HEAD_PAD), lambda b: (b, 0)),
                  pl.BlockSpec((n, MLA_HEADS * MLA_V), lambda b: (b, 0))],
        out_specs=pl.BlockSpec((n, MLA_HEADS * MLA_V), lambda b: (b, 0)),
        out_shape=jax.ShapeDtypeStruct((nseq * n, MLA_HEADS * MLA_V), BF16),
        compiler_params=_cparams(("arbitrary",)),
        name="attn_ctx",
    )(q, k, v)


def _attention_lat(q, k, v, kc, vc, layer, row0, nseq, n, tq):
    hp = MLA_HEADS // 2
    P = kc.shape[2]
    qt = n // tq
    q0 = row0 // tq
    s0 = row0 // n
    nq, nv = MLA_HEADS * HEAD_PAD, MLA_HEADS * MLA_V
    return pl.pallas_call(
        functools.partial(_attn_body, nseg=2, npairs=hp),
        grid=(nseq, qt),
        in_specs=[pl.BlockSpec((tq, nq), lambda b, t: (q0 + b * qt + t, 0)),
                  pl.BlockSpec((None, None, P, nq), lambda b, t: (layer, b, 0, 0)),
                  pl.BlockSpec((n, nq), lambda b, t: (s0 + b, 0)),
                  pl.BlockSpec((None, None, P, nv), lambda b, t: (layer, b, 0, 0)),
                  pl.BlockSpec((n, nv), lambda b, t: (s0 + b, 0))],
        out_specs=pl.BlockSpec((tq, nv), lambda b, t: (b * qt + t, 0)),
        out_shape=jax.ShapeDtypeStruct((nseq * n, nv), BF16),
        compiler_params=_cparams(("arbitrary", "arbitrary")),
        name="attn_lat",
    )(q, kc, k, vc, v)


def _head_sums(x, ones_quad):
    return jnp.concatenate([_dot(x[:, a:a + GLA_QUAD], ones_quad) for a in range(0, GLA_W, GLA_QUAD)], axis=1)


def _gla_chunk(q, k, v, g, st_refs, fwd, tri, ones_bd, mild):
    C = GLA_CHUNK
    W = GLA_W
    nquad = W // GLA_QUAD
    rows = lax.broadcasted_iota(I32, (C, 1), 0)

    g1, g2, g3 = _split3(g * LOG2E)
    cum = _dot(tri, jnp.concatenate([g1, g2, g3], axis=1))
    cum = cum[:, 0:W] + cum[:, W:2 * W] + cum[:, 2 * W:3 * W]
    edge = C - 1 if fwd else 0
    last = cum[edge:edge + 1]
    q_in = (q * jnp.exp2(cum)).astype(BF16)
    k_st = (k * jnp.exp2(last - cum)).astype(BF16)
    v_b = v.astype(BF16)

    lane_q = lax.broadcasted_iota(I32, (1, GLA_QUAD), 1)
    head_masks = [jnp.right_shift(lane_q, GLA_DK_SHIFT) == h for h in range(GLA_QUAD // GLA_DK)]
    col_s = jnp.bitwise_and(lane_q, C - 1)

    def stack_heads(xq):
        return jnp.concatenate([jnp.where(mh, xq, jnp.zeros_like(xq)) for mh in head_masks], axis=0)

    a_acc = [jnp.zeros((C, GLA_QUAD), F32) for _ in range(nquad)]
    half = C // 2
    while half >= GLA_DIAG:
        blk = 2 * half
        pieces = []
        for p in range(C // blk):
            rr = p * blk + (half - 1 if fwd else half)
            pieces.append(jnp.broadcast_to(cum[rr:rr + 1], (blk, W)))
        ref = jnp.concatenate(pieces, axis=0) if len(pieces) > 1 else pieces[0]
        upper = jnp.bitwise_and(rows, blk - 1) >= half
        qmask = upper if fwd else jnp.logical_not(upper)
        qe = jnp.where(qmask, q * jnp.exp2(cum - ref), 0.0).astype(BF16)
        ke = jnp.where(qmask, 0.0, k * jnp.exp2(ref - cum)).astype(BF16)
        sh = blk.bit_length() - 1
        same = jnp.right_shift(rows, sh) == jnp.right_shift(col_s, sh)
        for qd in range(nquad):
            sl = slice(qd * GLA_QUAD, (qd + 1) * GLA_QUAD)
            a = _dot_nt(qe[:, sl], stack_heads(ke[:, sl]))
            a_acc[qd] = a_acc[qd] + jnp.where(same, a, 0.0)
        half //= 2

    if mild:
        nb = C // GLA_DIAG
        c3 = cum.reshape(nb, GLA_DIAG, W)
        er = 0 if fwd else GLA_DIAG - 1
        ref = jnp.broadcast_to(c3[:, er:er + 1, :], (nb, GLA_DIAG, W)).reshape(C, W)
        qe = (q * jnp.exp2(cum - ref)).astype(BF16)
        ke = (k * jnp.exp2(ref - cum)).astype(BF16)
        sh = GLA_DIAG.bit_length() - 1
        keep = jnp.right_shift(rows, sh) == jnp.right_shift(col_s, sh)
        keep = jnp.logical_and(keep, (rows >= col_s) if fwd else (rows <= col_s))
        for qd in range(nquad):
            sl = slice(qd * GLA_QUAD, (qd + 1) * GLA_QUAD)
            a = _dot_nt(qe[:, sl], stack_heads(ke[:, sl]))
            a_acc[qd] = a_acc[qd] + jnp.where(keep, a, 0.0)

    r2 =jnp.right_shift(lax.broadcasted_iota(I32, (GLA_QUAD, GLA_QUAD), 0), GLA_DK_SHIFT)
    c2 = jnp.right_shift(lax.broadcasted_iota(I32, (GLA_QUAD, GLA_QUAD), 1), GLA_DK_SHIFT)
    o_parts = []
    for qd in range(nquad):
        sl = slice(qd * GLA_QUAD, (qd + 1) * GLA_QUAD)
        st = st_refs[qd][...]
        o = _dot_nt(q_in[:, sl], st.astype(BF16))
        o = o + _dot(a_acc[qd].astype(BF16), stack_heads(v_b[:, sl]))
        o_parts.append(o)
        upd = _dot_tn(v_b[:, sl], k_st[:, sl])
        st_refs[qd][...] = st * jnp.exp2(last[:, sl]) + jnp.where(r2 == c2, upd, 0.0)
    o = jnp.concatenate(o_parts, axis=1)
    if mild:
        return o

    nb = C // GLA_DIAG
    q3 = q.reshape(nb, GLA_DIAG, W)
    k3 = k.reshape(nb, GLA_DIAG, W)
    v3 = v.reshape(nb, GLA_DIAG, W)
    c3 = cum.reshape(nb, GLA_DIAG, W)
    tl = lax.broadcasted_iota(I32, (1, GLA_DIAG, 1), 1)
    zs = []
    for s in range(GLA_DIAG):
        e = jnp.exp2(c3 - c3[:, s:s + 1, :])
        valid = (tl >= s) if fwd else (tl <= s)
        zs.append(jnp.where(valid, q3 * e * k3[:, s:s + 1, :], 0.0).reshape(C, W).astype(BF16))
    w_all = _head_sums(jnp.concatenate(zs, axis=0), ones_bd)
    od = jnp.zeros((nb, GLA_DIAG, W), F32)
    for s in range(GLA_DIAG):
        od = od + w_all[s * C:(s + 1) * C].reshape(nb, GLA_DIAG, W) * v3[:, s:s + 1, :]
    return o + od.reshape(C, W)


def _gla_body(*refs, n, spb, has_init, has_final):
    it = iter(refs)
    q_ref, k_ref, v_ref, g_ref, sg_ref, gg_ref, trif_ref, trib_ref, ones_ref = (next(it) for _ in range(9))
    s0_ref = next(it) if has_init else None
    o_ref = next(it)
    sf_ref = next(it) if has_final else None
    of_ref, ob_ref = next(it), next(it)
    nquad = GLA_W // GLA_QUAD
    st = [[[next(it) for _ in range(nquad)] for _ in range(2)] for _ in range(spb)]

    C = GLA_CHUNK
    nc = n // C
    hq = GLA_QUAD // GLA_DK
    for sq in range(spb):
        for d in range(2):
            for qd in range(nquad):
                st[sq][d][qd][...] = jnp.zeros((GLA_QUAD, GLA_QUAD), F32)
                if has_init:
                    for h in range(hq):
                        hs = slice(h * GLA_DK, (h + 1) * GLA_DK)
                        st[sq][d][qd][hs, hs] = s0_ref[sq, d, qd * hq + h]

    trif = trif_ref[...]
    trib = trib_ref[...]
    ones_bd = ones_ref[...]

    def step(i, carry, mild):
        for sq in range(spb):
            for d, (tri, acc) in enumerate(((trif, of_ref), (trib, ob_ref))):
                c = i if d == 0 else nc - 1 - i
                rs = pl.ds(pl.multiple_of(sq * n + c * C, C), C)
                gd = g_ref[rs, d * GLA_W:(d + 1) * GLA_W]
                qkv = [r[rs, :].astype(F32) for r in (q_ref, k_ref, v_ref)]
                acc[rs, :] = _gla_chunk(*qkv, gd, st[sq][d], d == 0, tri, ones_bd, mild)
        return carry

    steepest = jnp.max(-g_ref[...]) * (LOG2E * (GLA_DIAG - 1))
    is_mild = steepest < GLA_MILD_LOG2

    @pl.when(is_mild)
    def _():
        lax.fori_loop(0, nc, functools.partial(step, mild=True), 0)

    @pl.when(jnp.logical_not(is_mild))
    def _():
        lax.fori_loop(0, nc, functools.partial(step, mild=False), 0)

    gg = gg_ref[...]
    fr = min(n, 256)

    def fin(i, carry):
        rs = pl.ds(pl.multiple_of(i * fr, fr), fr)
        o = of_ref[rs, :] + ob_ref[rs, :]
        ms = _head_sums(jnp.concatenate(_split3(o * o)[:2], axis=0), ones_bd)
        ms = (ms[0:fr] + ms[fr:2 * fr]) * (1.0 / GLA_DV)
        o_ref[rs, :] = (o * lax.rsqrt(ms + EPS) * gg * sg_ref[rs, :].astype(F32)).astype(BF16)
        return carry

    lax.fori_loop(0, spb * n // fr, fin, 0)

    if has_final:
        for sq in range(spb):
            for d in range(2):
                for qd in range(nquad):
                    s = st[sq][d][qd][...].T
                    for h in range(hq):
                        sf_ref[sq, d, qd * hq + h] = s[h * GLA_DK:(h + 1) * GLA_DK, h * GLA_DK:(h + 1) * GLA_DK]


def _gla(gq, gk, gv, g, sg, g_gla, consts, row0, nseq, n, s0=None, want_final=False):
    trif, trib, ones_bd = consts
    spb = 1
    while (2 * spb <= GLA_SEQS_PER_STEP and 2 * spb * n <= GLA_ROWS_PER_STEP and nseq % (2 * spb) == 0
           and (row0 // n) % (2 * spb) == 0):
        spb *= 2
    b0 = row0 // (n * spb)
    W = GLA_W
    nquad = W // GLA_QUAD
    seq = lambda w: pl.BlockSpec((spb * n, w), lambda b: (b0 + b, 0))
    full = lambda a: pl.BlockSpec(a.shape, lambda b: (0,) * a.ndim)
    in_specs = [seq(W), seq(W), seq(W), seq(2 * W), seq(W), full(g_gla), full(trif), full(trib), full(ones_bd)]
    args = [gq, gk, gv, g, sg, g_gla, trif, trib, ones_bd]
    if s0 is not None:
        s0_arr, layer = s0
        in_specs.append(pl.BlockSpec((spb, None, 2, GLA_HEADS, GLA_DV, GLA_DK), lambda b: (b, layer, 0, 0, 0, 0)))
        args.append(s0_arr)
    out_specs = [pl.BlockSpec((spb * n, W), lambda b: (b, 0))]
    out_shape = [jax.ShapeDtypeStruct((nseq * n, W), BF16)]
    if want_final:
        out_specs.append(pl.BlockSpec((spb, 2, GLA_HEADS, GLA_DK, GLA_DV), lambda b: (b, 0, 0, 0, 0)))
        out_shape.append(jax.ShapeDtypeStruct((nseq, 2, GLA_HEADS, GLA_DK, GLA_DV), F32))
    scratch = [pltpu.VMEM((spb * n, W), F32), pltpu.VMEM((spb * n, W), F32)]
    scratch += [pltpu.VMEM((GLA_QUAD, GLA_QUAD), F32) for _ in range(spb * 2 * nquad)]
    res = pl.pallas_call(
        functools.partial(_gla_body, n=n, spb=spb, has_init=s0 is not None, has_final=want_final),
        grid=(nseq // spb,),
        in_specs=in_specs,
        out_specs=out_specs,
        out_shape=out_shape,
        scratch_shapes=scratch,
        compiler_params=_cparams(("arbitrary",)),
        name="gla_lat" if s0 is not None else "gla_ctx",
    )(*args)
    return res


def _merge_body(xc_ref, xl_ref, mod_ref, oc_ref, ol_ref, gc_ref, gl_ref, g1_ref, wgate_ref, wom_ref, wog_ref,
                wout_ref, g2_ref, wr_ref, x1_ref, h2_ref, aff_ref, *, D, n_ctx_tiles):
    is_ctx = pl.program_id(0) < n_ctx_tiles
    mod = mod_ref[0]
    x = jnp.where(is_ctx, xc_ref[...], xl_ref[...])
    h = _modulated_norm(x, g1_ref[...], mod[:, D:2 * D], mod[:, 0:D]).astype(BF16)
    om = _dot(jnp.where(is_ctx, oc_ref[...], ol_ref[...]), wom_ref[...])
    merged = _sigmoid(_dot(h, wgate_ref[:, 0:D])) * om
    og = _dot(jnp.where(is_ctx, gc_ref[...], gl_ref[...]), wog_ref[...])
    merged = (merged + _sigmoid(_dot(h, wgate_ref[:, D:2 * D])) * og).astype(BF16)
    mix = _dot(merged, wout_ref[...])
    x1 = x + mod[:, 2 * D:3 * D] * mix
    x1_ref[...] = x1
    h2 = _modulated_norm(x1, g2_ref[...], mod[:, 4 * D:5 * D], mod[:, 3 * D:4 * D])
    h2_ref[...] = h2.astype(BF16)
    a, b, _ = _split3(h2)
    wa, wb, _ = _split3(wr_ref[...])
    logits = _dot(jnp.concatenate([a, a, b], axis=1), jnp.concatenate([wa, wb, wa], axis=0))
    lane = lax.broadcasted_iota(I32, logits.shape, 1)
    logits = jnp.where(lane < N_EXPERTS, logits, -jnp.inf)
    p = jnp.exp(logits - logits.max(axis=-1, keepdims=True))
    aff_ref[...] = p / p.sum(axis=-1, keepdims=True)


def _merge(xc, xl, mod, layer, o_ctx, o_lat, og_ctx, og_lat, g1, wgate, wom, wog, wout, g2, wr, cond_of_tile):
    (Tc, D), Tl = xc.shape, xl.shape[0]
    T2 = Tc + Tl
    n_ctx_tiles, n_lat_tiles = Tc // TOKEN_TILE, Tl // TOKEN_TILE
    nt = n_ctx_tiles + n_lat_tiles
    L, R = mod.shape[:2]
    row = lambda w: pl.BlockSpec((TOKEN_TILE, w), lambda i: (i, 0))
    ctx, lat = _ctx_lat_specs(n_ctx_tiles, n_lat_tiles)
    W = o_ctx.shape[1]
    params = [g1, wgate, wom, wog, wout, g2, wr]
    return pl.pallas_call(
        functools.partial(_merge_body, D=D, n_ctx_tiles=n_ctx_tiles),
        grid=(nt,),
        in_specs=[ctx(D), lat(D), _mod_spec(R, D, layer, cond_of_tile), ctx(W), lat(W), ctx(GLA_W), lat(GLA_W)]
                 + [_layer_spec(a, layer) for a in params],
        out_specs=[row(D), row(D), row(LANES)],
        out_shape=[jax.ShapeDtypeStruct((T2, D), F32), jax.ShapeDtypeStruct((T2, D), BF16),
                   jax.ShapeDtypeStruct((T2, LANES), F32)],
        compiler_params=_cparams(("arbitrary",)),
        name="merge_router",
    )(xc, xl, mod.reshape(L * R, 1, 6 * D), o_ctx, o_lat, og_ctx, og_lat, *params)


def _topk_body(a_ref, slot_ref, cb_ref, *, T, cap):
    a = a_ref[0]
    E = a.shape[0]

    def count_ge(thr):
        return jnp.sum((a >= thr).astype(F32), axis=1, keepdims=True)

    hi = jnp.full((E, 1), 2.0, F32)
    for j in range(TOPK_EXP_BITS - 1, -1, -1):
        cand = hi * (2.0 ** -(2 ** j))
        hi = jnp.where(count_ge(cand) < cap, cand, hi)
    lo = jnp.where(hi > F32_MIN_NORMAL, 0.5 * hi, 0.0)

    def bisect(_, lh):
        lo, hi = lh
        mid = 0.5 * (lo + hi)
        up = count_ge(mid) >= cap
        return jnp.where(up, mid, lo), jnp.where(up, hi, mid)

    lo, hi = lax.fori_loop(0, TOPK_BISECT_STEPS, bisect, (lo, hi))
    gt = a >= hi
    eq = jnp.logical_and(a >= lo, a < hi)
    need = cap - jnp.sum(gt.astype(F32), axis=1, keepdims=True)

    r = lax.broadcasted_iota(I32, (LANES, LANES), 0)
    c = lax.broadcasted_iota(I32, (LANES, LANES), 1)
    triu = (r < c).astype(BF16)
    lane = lax.broadcasted_iota(I32, (E, LANES), 1)

    nb = T // LANES
    per_tile = ROW_TILE // LANES
    carry_eq = jnp.zeros((E, 1), F32)
    carry_sel = jnp.zeros((E, 1), F32)
    cb = jnp.zeros((E, LANES), I32)
    for j in range(nb):
        sl = slice(j * LANES, (j + 1) * LANES)
        eq_j = eq[:, sl].astype(BF16)
        pre = _dot(eq_j, triu) + carry_eq
        carry_eq = carry_eq + jnp.sum(eq_j.astype(F32), axis=1, keepdims=True)
        sel = jnp.logical_or(gt[:, sl], jnp.logical_and(eq[:, sl], pre < need))
        sel_b = sel.astype(BF16)
        if j % per_tile == 0:
            cb = jnp.where(lane == j // per_tile, carry_sel.astype(I32), cb)
        slot = (_dot(sel_b, triu) + carry_sel).astype(I32)
        carry_sel = carry_sel + jnp.sum(sel_b.astype(F32), axis=1, keepdims=True)
        slot_ref[0, :, sl] = jnp.where(sel, slot, -1)
    cb_ref[0] = jnp.where(lane == nb // per_tile, carry_sel.astype(I32), cb)


def _topk(aff_t, cap):
    G, E, T = aff_t.shape
    return pl.pallas_call(
        functools.partial(_topk_body, T=T, cap=cap),
        grid=(G,),
        in_specs=[pl.BlockSpec((1, E, T), lambda g: (g, 0, 0))],
        out_specs=[pl.BlockSpec((1, E, T), lambda g: (g, 0, 0)),
                   pl.BlockSpec((1, E, LANES), lambda g: (g, 0, 0))],
        out_shape=[jax.ShapeDtypeStruct((G, E, T), I32), jax.ShapeDtypeStruct((G, E, LANES), I32)],
        compiler_params=_cparams(("arbitrary",)),
        name="expert_topk",
    )(aff_t)


def _expert_body(cb_ref, slot_ref, aff_ref, h_ref, wg_ref, wu_ref, wd_ref, y_ref, xs_ref, acc_ref, ws_ref, *,
                 cap, win, nt, nfh, E):
    g = pl.program_id(0)
    e = pl.program_id(1)
    fh = pl.program_id(2)
    base = (g * E + e) * LANES

    @pl.when(fh == 0)
    def _gather():
        xs_ref[...] = jnp.zeros_like(xs_ref)
        ws_ref[...] = jnp.zeros_like(ws_ref)
        slab = win

        def slab_start(i):
            return jnp.minimum(jnp.bitwise_and(cb_ref[base + i], -SUBLANES), cap - slab)

        def tile_fits(i, ok):
            return jnp.logical_and(ok, cb_ref[base + i + 1] <= slab_start(i) + slab)

        fits = lax.fori_loop(0, nt, tile_fits, jnp.bool_(True))

        def tile_slab(i, carry):
            r0 = pl.multiple_of(slab_start(i), SUBLANES)
            j = lax.broadcasted_iota(I32, (slab, 1), 0) + r0
            hit = slot_ref[i] == j
            hs = h_ref[pl.ds(pl.multiple_of(i * ROW_TILE, ROW_TILE), ROW_TILE), :]
            xs_ref[pl.ds(r0, slab), :] += _dot(hit.astype(BF16), hs)
            ws_ref[pl.ds(r0, slab), :] += jnp.sum(jnp.where(hit, aff_ref[i], 0.0), axis=1, keepdims=True)
            return carry

        @pl.when(fits)
        def _():
            lax.fori_loop(0, nt, tile_slab, 0, unroll=True)

        def tile(i, carry):
            lo = cb_ref[base + i]
            hi = cb_ref[base + i + 1]
            srow = slot_ref[i]
            arow = aff_ref[i]
            for w in range(cap // win):
                @pl.when(jnp.logical_and(lo < (w + 1) * win, hi > w * win))
                def _():
                    j = lax.broadcasted_iota(I32, (win, 1), 0) + w * win
                    hit = srow == j
                    hs = h_ref[pl.ds(pl.multiple_of(i * ROW_TILE, ROW_TILE), ROW_TILE), :]
                    xs_ref[w * win:(w + 1) * win, :] += _dot(hit.astype(BF16), hs)
                    ws_ref[w * win:(w + 1) * win, :] += jnp.sum(jnp.where(hit, arow, 0.0), axis=1, keepdims=True)
            return carry

        @pl.when(jnp.logical_not(fits))
        def _():
            lax.fori_loop(0, nt, tile, 0)

    xb = xs_ref[...].astype(BF16)
    gate = _dot(xb, wg_ref[...].astype(BF16))
    up = _dot(xb, wu_ref[...].astype(BF16))
    hid = (gate * _sigmoid(gate) * up).astype(BF16)
    part = _dot(hid, wd_ref[...].astype(BF16))

    @pl.when(fh == 0)
    def _():
        acc_ref[...] = part

    @pl.when(fh > 0)
    def _():
        acc_ref[...] += part

    @pl.when(fh == nfh - 1)
    def _():
        y_ref[...] = (acc_ref[...] * ws_ref[...]).astype(BF16)


def _experts(cb_flat, slot5, aff5, h2, w_gate, w_up, w_down, layer, cap, win):
    G, E, nt = slot5.shape[:3]
    T = nt * ROW_TILE
    D = h2.shape[1]
    FF = w_gate.shape[-1]
    nfh = 1
    fb = FF // nfh
    grid_spec = pltpu.PrefetchScalarGridSpec(
        num_scalar_prefetch=1,
        grid=(G, E, nfh),
        in_specs=[pl.BlockSpec((None, None, nt, 1, ROW_TILE), lambda g, e, f, cb: (g, e, 0, 0, 0)),
                  pl.BlockSpec((None, None, nt, 1, ROW_TILE), lambda g, e, f, cb: (g, e, 0, 0, 0)),
                  pl.BlockSpec((T, D), lambda g, e, f, cb: (g, 0), pipeline_mode=pl.Buffered(1)),
                  pl.BlockSpec((None, None, D, fb), lambda g, e, f, cb: (layer, e, 0, f)),
                  pl.BlockSpec((None, None, D, fb), lambda g, e, f, cb: (layer, e, 0, f)),
                  pl.BlockSpec((None, None, fb, D), lambda g, e, f, cb: (layer, e, f, 0))],
        out_specs=pl.BlockSpec((None, None, cap, D), lambda g, e, f, cb: (g, e, 0, 0)),
        scratch_shapes=[pltpu.VMEM((cap, D), F32), pltpu.VMEM((cap, D), F32), pltpu.VMEM((cap, 1), F32)],
    )
    return pl.pallas_call(
        functools.partial(_expert_body, cap=cap, win=win, nt=nt, nfh=nfh, E=E),
        grid_spec=grid_spec,
        out_shape=jax.ShapeDtypeStruct((G, E, cap, D), BF16),
        compiler_params=_cparams(("arbitrary", "arbitrary", "arbitrary")),
        name="expert_ffn",
    )(cb_flat, slot5, aff5, h2, w_gate, w_up, w_down)


def _combine_body(cb_ref, x_ref, mod_ref, slot_ref, y_ref, spread_ref, gf_ref, oc_ref, ol_ref, acc_ref, *,
                  D, cap, win, cw, E, tiles_per_group, final):
    i = pl.program_id(0)
    g = i // tiles_per_group
    ti = i % tiles_per_group
    slots = slot_ref[...]
    los = [cb_ref[(g * E + e) * LANES + ti] for e in range(E)]
    his = [cb_ref[(g * E + e) * LANES + ti + 1] for e in range(E)]
    starts = [jnp.minimum(jnp.bitwise_and(lo, -BF16_ROWS), cap - cw) for lo in los]
    fits = his[0] <= starts[0] + cw
    for e in range(1, E):
        fits = jnp.logical_and(fits, his[e] <= starts[e] + cw)

    @pl.when(fits)
    def _fast():
        sp1 = slots + 1
        digits = jnp.concatenate([jnp.right_shift(sp1, 4), jnp.bitwise_and(sp1, 15)], axis=1)
        spread = _dot(digits.astype(F32).astype(BF16), spread_ref[...])
        lane = lax.broadcasted_iota(I32, (1, cw), 1)
        tgt = jnp.concatenate([lane + (starts[e] + 1) for e in range(E)], axis=1).astype(F32)
        onehot = (spread == tgt).astype(BF16)
        rows = jnp.concatenate([y_ref[e, pl.ds(pl.multiple_of(starts[e], BF16_ROWS), cw), :] for e in range(E)],
                               axis=0)
        acc_ref[...] = _dot(onehot, rows)

    @pl.when(jnp.logical_not(fits))
    def _general():
        acc_ref[...] = jnp.zeros_like(acc_ref)
        for e in range(E):
            col = slots[:, e:e + 1]
            for w in range(cap // win):
                @pl.when(jnp.logical_and(los[e] < (w + 1) * win, his[e] > w * win))
                def _():
                    j = lax.broadcasted_iota(I32, (1, win), 1) + w * win
                    oh = (col == j).astype(BF16)
                    acc_ref[...] += _dot(oh, y_ref[e, w * win:(w + 1) * win, :])

    mod = mod_ref[0]
    x2 = x_ref[...] + mod[:, 5 * D:6 * D] * acc_ref[...]
    if final:
        x2 = _rms(x2, gf_ref[...])

    @pl.when(g == 0)
    def _():
        oc_ref[...] = x2

    @pl.when(g != 0)
    def _():
        ol_ref[...] = x2


def _combine(cb_flat, x1, mod_l, slot_t, y, g_final, cond_of_tile, cap, win, final):
    T2, D = x1.shape
    G, E = y.shape[:2]
    nt = T2 // ROW_TILE
    tpg = nt // G
    R = mod_l.shape[0]
    cw = min(LANES, cap)
    spread = np.zeros((2 * LANES, E * cw), np.float32)
    for e in range(E):
        spread[e, e * cw:(e + 1) * cw] = 16.0
        spread[LANES + e, e * cw:(e + 1) * cw] = 1.0
    spread = jnp.asarray(spread, BF16)
    assert G == 2
    ctx, lat = _ctx_lat_specs(tpg, tpg, ROW_TILE)
    grid_spec = pltpu.PrefetchScalarGridSpec(
        num_scalar_prefetch=1,
        grid=(nt,),
        in_specs=[pl.BlockSpec((ROW_TILE, D), lambda i, cb: (i, 0)),
                  pl.BlockSpec((1, 1, 6 * D), lambda i, cb: (cond_of_tile(i), 0, 0)),
                  pl.BlockSpec((ROW_TILE, LANES), lambda i, cb: (i, 0)),
                  pl.BlockSpec((None, E, cap, D), lambda i, cb: (i // tpg, 0, 0, 0)),
                  pl.BlockSpec(spread.shape, lambda i, cb: (0, 0)),
                  pl.BlockSpec((1, D), lambda i, cb: (0, 0))],
        out_specs=[ctx(D), lat(D)],
        scratch_shapes=[pltpu.VMEM((ROW_TILE, D), F32)],
    )
    return pl.pallas_call(
        functools.partial(_combine_body, D=D, cap=cap, win=win, cw=cw, E=E, tiles_per_group=tpg, final=final),
        grid_spec=grid_spec,
        out_shape=[jax.ShapeDtypeStruct((T2 // G, D), F32), jax.ShapeDtypeStruct((T2 // G, D), F32)],
        compiler_params=_cparams(("arbitrary",)),
        name="moe_combine",
    )(cb_flat, x1, mod_l.reshape(R, 1, 6 * D), slot_t, y, spread, g_final)


def _pack_weights(w_in, w_uq, w_uk, w_uv, w_gla_gate, b_gla_gate):
    L, D, _ = w_in.shape
    sizes = (Q_RANK, KV_RANK, MLA_ROPE, GLA_W, GLA_W, GLA_W, 2 * GLA_GATE_RANK, GLA_W, D, D)
    idx = np.cumsum(sizes)[:-1]
    pq, pkv, kr, gq, gk, gv, glr, gog, ga, gb = jnp.split(w_in, [int(i) for i in idx], axis=-1)
    npair = MLA_ROPE // 4
    swap = np.concatenate([np.arange(npair, 2 * npair), np.arange(0, npair),
                           np.arange(3 * npair, 4 * npair), np.arange(2 * npair, 3 * npair)])

    def slot_rope(w):
        return jnp.pad(w, ((0, 0), (0, 0), (MLA_NOPE, HEAD_PAD - MLA_NOPE - MLA_ROPE)))

    glr_p = jnp.pad(glr, ((0, 0), (0, 0), (0, LANES - 2 * GLA_GATE_RANK)))
    wp = jnp.concatenate([pq, pkv, slot_rope(kr), slot_rope(kr[..., swap]), gq, gk, gv, glr_p, gog],
                         axis=-1).astype(BF16)
    wgate = jnp.concatenate([ga, gb], axis=-1).astype(BF16)

    uq = w_uq.reshape(L, Q_RANK, MLA_HEADS, MLA_NOPE + MLA_ROPE)
    pad_h = HEAD_PAD - MLA_NOPE - MLA_ROPE
    uq_n = jnp.pad(uq, ((0, 0), (0, 0), (0, 0), (0, pad_h))).reshape(L, Q_RANK, MLA_HEADS * HEAD_PAD)
    uq_s = jnp.concatenate([jnp.zeros_like(uq[..., :MLA_NOPE]), uq[..., MLA_NOPE:][..., swap]], axis=-1)
    uq_s = jnp.pad(uq_s, ((0, 0), (0, 0), (0, 0), (0, pad_h))).reshape(L, Q_RANK, MLA_HEADS * HEAD_PAD)
    wuq = jnp.concatenate([uq_n, uq_s], axis=-1).astype(BF16)

    uk = w_uk.reshape(L, KV_RANK, MLA_HEADS, MLA_NOPE)
    wuk = jnp.pad(uk, ((0, 0), (0, 0), (0, 0), (0, HEAD_PAD - MLA_NOPE))).reshape(
        L, KV_RANK, MLA_HEADS * HEAD_PAD).astype(BF16)
    wuv = w_uv.astype(BF16)

    wg = jnp.zeros((L, LANES, 2 * GLA_W), F32)
    wg = wg.at[:, 0:GLA_GATE_RANK, 0:GLA_W].set(w_gla_gate[:, 0])
    wg = wg.at[:, GLA_GATE_RANK:2 * GLA_GATE_RANK, GLA_W:].set(w_gla_gate[:, 1])
    bg = b_gla_gate.reshape(L, 1, 2 * GLA_W)
    return wp, wgate, wuq, wuk, wuv, wg.astype(BF16), bg


def _rope_tables(n_lat):
    npair = MLA_ROPE // 4
    freqs = ROPE_BASE ** (-jnp.arange(npair, dtype=F32) / npair)
    pos = jnp.arange(n_lat)
    ang_r = (pos // GRID_W).astype(F32)[:, None] * freqs
    ang_c = (pos % GRID_W).astype(F32)[:, None] * freqs
    cr, sr, cc, sc = jnp.cos(ang_r), jnp.sin(ang_r), jnp.cos(ang_c), jnp.sin(ang_c)
    cos32 = jnp.concatenate([cr, cr, cc, cc], axis=-1)
    sin32 = jnp.concatenate([-sr, sr, -sc, sc], axis=-1)
    pad_h = HEAD_PAD - MLA_NOPE - MLA_ROPE
    ones = jnp.ones((n_lat, MLA_NOPE), F32)
    cos_l = jnp.concatenate([ones, cos32, jnp.zeros((n_lat, pad_h), F32)], axis=-1)
    sin_l = jnp.pad(sin32, ((0, 0), (MLA_NOPE, pad_h)))
    cos_i = jnp.concatenate([jnp.ones((INPROJ_TILE, MLA_NOPE + MLA_ROPE), F32),
                             jnp.zeros((INPROJ_TILE, pad_h), F32)], -1)
    sin_i = jnp.zeros((INPROJ_TILE, HEAD_PAD), F32)
    return jnp.concatenate([cos_i, cos_l], 0), jnp.concatenate([sin_i, sin_l], 0)


def _gla_consts():
    C = GLA_CHUNK
    r = np.arange(C)
    trif = (r[None, :] <= r[:, None]).astype(np.float32)
    trib = (r[None, :] >= r[:, None]).astype(np.float32)
    h = np.arange(GLA_QUAD) // GLA_DK
    ones_bd = (h[:, None] == h[None, :]).astype(np.float32)
    return jnp.asarray(trif, BF16), jnp.asarray(trib, BF16), jnp.asarray(ones_bd, BF16)


def kernel(x_prompt, x_sample, cache_ckv, cache_krope, state_gla, c, c_ctx, w_mod, b_mod, g_norm1, g_norm2, w_in, g_q, g_kv, w_uq, w_uk, w_uv, w_o_mla, w_gla_gate, b_gla_gate, g_gla, w_o_gla, w_out, w_router, w_e_gate, w_e_up, w_e_down, g_final):
    B, N, D = x_prompt.shape
    DB, DN, _ = x_sample.shape
    L = w_in.shape[0]
    Tc, Tl = B * N, DB * DN
    assert Tc == Tl and Tc % DN == 0 and N % ROW_TILE == 0 and DN % TOKEN_TILE == 0 and Tc % TOKEN_TILE == 0
    assert N % GLA_CHUNK == 0 and DN % GLA_CHUNK == 0 and DN % GRID_W == 0
    assert DN % INPROJ_TILE == 0 and Tc % INPROJ_TILE == 0
    T = Tc
    G = 2
    cap = max(1, CAPACITY_FACTOR * T // N_EXPERTS)
    win = LANES if cap % LANES == 0 else cap
    assert cap % win == 0 and win % SUBLANES == 0 and 1 + DB <= SUBLANES

    def tile_maps(rows):
        nct, per_seq = Tc // rows, DN // rows
        cond = lambda i: jnp.where(i < nct, 0, 1 + (i - nct) // per_seq)
        tab = lambda i: jnp.where(i < nct, 0, 1 + (i - nct) % per_seq)
        return cond, tab

    cond_tok, _ = tile_maps(TOKEN_TILE)
    cond_in, tab_in = tile_maps(INPROJ_TILE)
    cond_row, _ = tile_maps(ROW_TILE)
    n_ctx_tiles = Tc // TOKEN_TILE

    cvec = jnp.concatenate([c_ctx[None, :], c, jnp.zeros((SUBLANES - 1 - DB, D), F32)], axis=0)
    mod = _adaln_all(cvec, w_mod, b_mod)

    wp, wgate, wuq, wuk, wuv, wg, bg = _pack_weights(w_in, w_uq, w_uk, w_uv, w_gla_gate, b_gla_gate)
    cos_t, sin_t = _rope_tables(DN)
    gla_consts = _gla_consts()
    wom = w_o_mla.astype(BF16)
    wog = w_o_gla.astype(BF16)
    wout = w_out.astype(BF16)
    wr = jnp.pad(w_router, ((0, 0), (0, 0), (0, LANES - N_EXPERTS)))
    g1s, g2s, gqs, gkvs = g_norm1[:, None, :], g_norm2[:, None, :], g_q[:, None, :], g_kv[:, None, :]

    ckr_pad = jnp.pad(cache_krope, ((0, 0), (0, 0), (0, 0), (MLA_NOPE, HEAD_PAD - MLA_NOPE - MLA_ROPE)))
    kc_all, vc_all = _cache_kv(cache_ckv, ckr_pad, wuk, wuv)
    st_t = jnp.swapaxes(state_gla, -1, -2)

    xc, xl = x_prompt.reshape(Tc, D), x_sample.reshape(Tl, D)
    ckv_list, kr_list, gla_list = [], [], []
    for l in range(L):
        pre = _inproj(xc, xl, mod, l, g1s, wp, gqs, gkvs, wuq, wuk, wuv, wg, bg, cos_t, sin_t, cond_in, tab_in)
        ckv_list.append(pre["ckv"][:Tc].reshape(B, N, KV_RANK))
        kr_list.append(pre["kr"][:Tc].reshape(B, N, MLA_ROPE))

        o_ctx = _attention_ctx(pre["q"], pre["k"], pre["v"], B, N)
        o_lat = _attention_lat(pre["q"], pre["k"], pre["v"], kc_all, vc_all, l, Tc, DB, DN, min(DN, 2 * TOKEN_TILE))

        gg = g_gla[l][None]
        og_ctx, s_fin = _gla(pre["gq"], pre["gk"], pre["gv"], pre["g"], pre["sg"], gg, gla_consts, 0, B, N,
                             want_final=True)
        (og_lat,) = _gla(pre["gq"], pre["gk"], pre["gv"], pre["g"], pre["sg"], gg, gla_consts, Tc, DB, DN,
                         s0=(st_t, l))
        gla_list.append(s_fin)

        x1, h2, aff = _merge(xc, xl, mod, l, o_ctx, o_lat, og_ctx, og_lat, g1s, wgate, wom, wog, wout, g2s, wr,
                             cond_tok)

        aff_t = jnp.swapaxes(aff[:, :N_EXPERTS].reshape(G, T, N_EXPERTS), 1, 2)
        slot, cb = _topk(aff_t, cap)
        cb_flat = cb.reshape(-1)
        slot5 = slot.reshape(G, N_EXPERTS, T // ROW_TILE, 1, ROW_TILE)
        slot_t = jnp.pad(jnp.swapaxes(slot, 1, 2).reshape(G * T, N_EXPERTS),
                         ((0, 0), (0, LANES - N_EXPERTS)), constant_values=-1)
        aff5 = aff_t.reshape(G, N_EXPERTS, T // ROW_TILE, 1, ROW_TILE)
        y = _experts(cb_flat, slot5, aff5, h2, w_e_gate, w_e_up, w_e_down, l, cap, win)
        xc, xl = _combine(cb_flat, x1, mod[l], slot_t, y, g_final[None], cond_row, cap, win, final=(l == L - 1))

    y_prompt = xc.reshape(B, N, D)
    y_sample = xl.reshape(DB, DN, D)
    new_ckv = jnp.stack(ckv_list, axis=1)
    new_krope = jnp.stack(kr_list, axis=1)
    new_gla = jnp.stack(gla_list, axis=1)
    return (y_prompt, y_sample, new_ckv, new_krope, new_gla)
```

```python
import functools

import jax
import jax.numpy as jnp
import numpy as np
from jax import lax
from jax.experimental import pallas as pl
from jax.experimental.pallas import tpu as pltpu

F32 = jnp.float32
BF16 = jnp.bfloat16
I32 = jnp.int32

GRID_W = 64
EPS = 1e-6
MLA_HEADS = 8
MLA_NOPE = 64
MLA_ROPE = 32
MLA_V = 64
Q_RANK = 384
KV_RANK = 256
ROPE_BASE = 10000.0
GLA_HEADS = 8
GLA_DK = 64
GLA_DV = 64
GLA_W = GLA_HEADS * GLA_DK
GLA_GATE_RANK = 16
GLA_TAU = 16.0
GLA_CHUNK = 64
N_EXPERTS = 16
CAPACITY_FACTOR = 2

LANES = 128
SUBLANES = 8
BF16_ROWS = 16
HEAD_PAD = 128
VMEM_LIMIT = 56 * 1024 * 1024

ROW_TILE = 256
TOKEN_TILE = 256
INPROJ_TILE = 512
GLA_QUAD = 4 * GLA_DK
GLA_DIAG = 8
GLA_ROWS_PER_STEP = 2048
GLA_SEQS_PER_STEP = 2
GLA_MILD_LOG2 = 64.0
F32_MIN_NORMAL = 2.0 ** -126
TOPK_EXP_BITS = 7
TOPK_BISECT_STEPS = 52
GLA_DK_SHIFT = GLA_DK.bit_length() - 1
assert 1 << GLA_DK_SHIFT == GLA_DK
LOG2E = float(np.log2(np.e))
Q_PRESCALE = (MLA_NOPE + MLA_ROPE) ** -0.5 * LOG2E

_SEG = {}
_off = 0
for _name, _width in (("pq", Q_RANK), ("pkv", KV_RANK), ("kr", HEAD_PAD), ("krs", HEAD_PAD),
                      ("gq", GLA_W), ("gk", GLA_W), ("gv", GLA_W), ("glr", LANES), ("gog", GLA_W)):
    _SEG[_name] = (_off, _width)
    _off += _width


def _cparams(sem, vmem=VMEM_LIMIT):
    return pltpu.CompilerParams(dimension_semantics=sem, vmem_limit_bytes=vmem)


def _dot(a, b):
    return jnp.dot(a, b, preferred_element_type=F32)


def _dot_nt(a, b):
    return lax.dot_general(a, b, (((1,), (1,)), ((), ())), preferred_element_type=F32)


def _dot_tn(a, b):
    return lax.dot_general(a, b, (((0,), (0,)), ((), ())), preferred_element_type=F32)


def _rms(x, g):
    return x * lax.rsqrt(jnp.mean(x * x, axis=-1, keepdims=True) + EPS) * g


def _modulated_norm(x, g, scale, shift):
    return _rms(x, g) * (1.0 + scale) + shift


def _sigmoid(x):
    return 0.5 * jnp.tanh(0.5 * x) + 0.5


def _split3(x):
    a = x.astype(BF16)
    r = x - a.astype(F32)
    b = r.astype(BF16)
    c = (r - b.astype(F32)).astype(BF16)
    return a, b, c


def _mod_body(c_ref, w_ref, b_ref, o_ref):
    c = c_ref[...]
    s = (c * _sigmoid(c)).astype(BF16)
    o_ref[0] = _dot(s, w_ref[0].astype(BF16)) + b_ref[0]


def _adaln_all(cvec, w_mod, b_mod):
    L, D, D6 = w_mod.shape
    R = cvec.shape[0]
    tn = 1536
    return pl.pallas_call(
        _mod_body,
        grid=(L, D6 // tn),
        in_specs=[pl.BlockSpec((R, D), lambda l, j: (0, 0)),
                  pl.BlockSpec((1, D, tn), lambda l, j: (l, 0, j)),
                  pl.BlockSpec((1, 1, tn), lambda l, j: (l, 0, j))],
        out_specs=pl.BlockSpec((1, R, tn), lambda l, j: (l, 0, j)),
        out_shape=jax.ShapeDtypeStruct((L, R, D6), F32),
        compiler_params=_cparams(("arbitrary", "arbitrary")),
        name="adaln_mod",
    )(cvec, w_mod, b_mod.reshape(L, 1, D6))


def _inproj_body(xc_ref, xl_ref, mod_ref, g1_ref, w_ref, gq_ref, gkv_ref, wuq_ref, wuk_ref, wuv_ref,
                 wg_ref, bg_ref, ct_ref, st_ref,
                 q_ref, k_ref, v_ref, ckv_ref, kr_ref, gqo_ref, gko_ref, gvo_ref, g_ref,
                 sg_ref, *, D, n_ctx_tiles):
    is_ctx = pl.program_id(0) < n_ctx_tiles
    mod = mod_ref[0]
    x = jnp.where(is_ctx, xc_ref[...], xl_ref[...])
    h = _modulated_norm(x, g1_ref[...], mod[:, D:2 * D], mod[:, 0:D]).astype(BF16)

    split = _SEG["gq"][0]
    mla_part = _dot(h, w_ref[:, 0:split])
    gla_part = _dot(h, w_ref[:, split:])

    def seg(name):
        a, w = _SEG[name]
        return mla_part[:, a:a + w] if a < split else gla_part[:, a - split:a - split + w]

    cos = ct_ref[...]
    sin = st_ref[...]
    nq = MLA_HEADS * HEAD_PAD

    cq = _rms(seg("pq"), gq_ref[...]).astype(BF16)
    qq = _dot(cq, wuq_ref[...])
    for hd in range(MLA_HEADS):
        a = hd * HEAD_PAD
        q_ref[:, a:a + HEAD_PAD] = ((qq[:, a:a + HEAD_PAD] * cos
                                     + qq[:, nq + a:nq + a + HEAD_PAD] * sin) * Q_PRESCALE).astype(BF16)

    ckv = _rms(seg("pkv"), gkv_ref[...])
    ckv_b = ckv.astype(BF16)
    kr = seg("kr")
    ckv_ref[...] = ckv
    kr_ref[...] = kr[:, MLA_NOPE:MLA_NOPE + MLA_ROPE]
    kr_rot = kr * cos + seg("krs") * sin
    kn = _dot(ckv_b, wuk_ref[...])
    for hd in range(MLA_HEADS):
        a = hd * HEAD_PAD
        k_ref[:, a:a + HEAD_PAD] = (kn[:, a:a + HEAD_PAD] + kr_rot).astype(BF16)
    v_ref[...] = _dot(ckv_b, wuv_ref[...]).astype(BF16)

    gqo_ref[...] = (seg("gq") * (GLA_DK ** -0.5)).astype(BF16)
    gko_ref[...] = seg("gk").astype(BF16)
    gvo_ref[...] = seg("gv").astype(BF16)
    logit = _dot(seg("glr").astype(BF16), wg_ref[...]) + bg_ref[...]
    g_ref[...] = (jnp.minimum(logit, 0.0) - jnp.log1p(jnp.exp(-jnp.abs(logit)))) * (1.0 / GLA_TAU)
    gog = seg("gog")
    sg_ref[...] = (gog * _sigmoid(gog)).astype(BF16)


def _layer_spec(a, layer):
    nd = a.ndim - 1
    return pl.BlockSpec((None,) + a.shape[1:], lambda i: (layer,) + (0,) * nd, pipeline_mode=pl.Buffered(1))


def _mod_spec(R, D, layer, cond_of_tile):
    return pl.BlockSpec((1, 1, 6 * D), lambda i: (layer * R + cond_of_tile(i), 0, 0))


def _ctx_lat_specs(n_ctx_tiles, n_lat_tiles, rows=TOKEN_TILE):
    ctx = lambda w: pl.BlockSpec((rows, w), lambda i, *_: (jnp.minimum(i, n_ctx_tiles - 1), 0))
    lat = lambda w: pl.BlockSpec((rows, w), lambda i, *_: (jnp.clip(i - n_ctx_tiles, 0, n_lat_tiles - 1), 0))
    return ctx, lat


def _inproj(xc, xl, mod, layer, g1, wp, gq, gkv, wuq, wuk, wuv, wg, bg, cos_t, sin_t, cond_of_tile, tab_of_tile):
    (Tc, D), Tl = xc.shape, xl.shape[0]
    T2 = Tc + Tl
    nct, nlt = Tc // INPROJ_TILE, Tl // INPROJ_TILE
    L, R = mod.shape[:2]
    row = lambda w: pl.BlockSpec((INPROJ_TILE, w), lambda i: (i, 0))
    ctx, lat = _ctx_lat_specs(nct, nlt, INPROJ_TILE)
    outs = [("q", MLA_HEADS * HEAD_PAD, BF16), ("k", MLA_HEADS * HEAD_PAD, BF16),
            ("v", MLA_HEADS * MLA_V, BF16), ("ckv", KV_RANK, F32), ("kr", MLA_ROPE, F32),
            ("gq", GLA_W, BF16), ("gk", GLA_W, BF16), ("gv", GLA_W, BF16), ("g", 2 * GLA_W, F32),
            ("sg", GLA_W, BF16)]
    ctx_only = ()
    params = [g1, wp, gq, gkv, wuq, wuk, wuv, wg, bg]
    res = pl.pallas_call(
        functools.partial(_inproj_body, D=D, n_ctx_tiles=nct),
        grid=(nct + nlt,),
        in_specs=[ctx(D), lat(D), _mod_spec(R, D, layer, cond_of_tile)]
                 + [_layer_spec(a, layer) for a in params]
                 + [pl.BlockSpec((INPROJ_TILE, HEAD_PAD), lambda i: (tab_of_tile(i), 0)),
                    pl.BlockSpec((INPROJ_TILE, HEAD_PAD), lambda i: (tab_of_tile(i), 0))],
        out_specs=[ctx(w) if nm in ctx_only else row(w) for nm, w, _ in outs],
        out_shape=[jax.ShapeDtypeStruct((Tc if nm in ctx_only else T2, w), dt) for nm, w, dt in outs],
        compiler_params=_cparams(("arbitrary",)),
        name="inproj",
    )(xc, xl, mod.reshape(L * R, 1, 6 * D), *params, cos_t, sin_t)
    return dict(zip([n for n, _, _ in outs], res))


def _cache_kv_body(ckv_ref, krp_ref, wuk_ref, wuv_ref, k_ref, v_ref):
    c = ckv_ref[...].astype(BF16)
    kn = _dot(c, wuk_ref[...])
    krp = krp_ref[...]
    for hd in range(MLA_HEADS):
        a = hd * HEAD_PAD
        k_ref[:, a:a + HEAD_PAD] = (kn[:, a:a + HEAD_PAD] + krp).astype(BF16)
    v_ref[...] = _dot(c, wuv_ref[...]).astype(BF16)


def _cache_kv(cache_ckv, cache_kr_pad, wuk, wuv):
    DB, L, P, R = cache_ckv.shape
    nk = MLA_HEADS * HEAD_PAD
    nv = MLA_HEADS * MLA_V
    return pl.pallas_call(
        _cache_kv_body,
        grid=(L, DB),
        in_specs=[pl.BlockSpec((None, None, P, R), lambda l, b: (b, l, 0, 0)),
                  pl.BlockSpec((None, None, P, HEAD_PAD), lambda l, b: (b, l, 0, 0)),
                  pl.BlockSpec((None, R, nk), lambda l, b: (l, 0, 0)),
                  pl.BlockSpec((None, R, nv), lambda l, b: (l, 0, 0))],
        out_specs=[pl.BlockSpec((None, None, P, nk), lambda l, b: (l, b, 0, 0)),
                   pl.BlockSpec((None, None, P, nv), lambda l, b: (l, b, 0, 0))],
        out_shape=[jax.ShapeDtypeStruct((L, DB, P, nk), BF16),
                   jax.ShapeDtypeStruct((L, DB, P, nv), BF16)],
        compiler_params=_cparams(("arbitrary", "arbitrary")),
        name="cache_kv",
    )(cache_ckv, cache_kr_pad, wuk, wuv)


def _attn_body(*refs, nseg, npairs):
    q_ref = refs[0]
    k_refs = refs[1:1 + nseg]
    v_refs = refs[1 + nseg:1 + 2 * nseg]
    o_ref = refs[1 + 2 * nseg]
    lane = lax.broadcasted_iota(I32, (1, 2 * MLA_V), 1)
    for pr in range(npairs):
        vs = slice(pr * 2 * MLA_V, (pr + 1) * 2 * MLA_V)
        outs = []
        for hh in range(2):
            hs = slice((2 * pr + hh) * HEAD_PAD, (2 * pr + hh + 1) * HEAD_PAD)
            qh = q_ref[:, hs]
            s = [_dot_nt(qh, kr[:, hs]) for kr in k_refs]
            m = s[0].max(axis=-1, keepdims=True)
            for sj in s[1:]:
                m = jnp.maximum(m, sj.max(axis=-1, keepdims=True))
            p = [jnp.exp2(sj - m) for sj in s]
            den = p[0].sum(axis=-1, keepdims=True)
            for pj in p[1:]:
                den = den + pj.sum(axis=-1, keepdims=True)
            o = None
            for pj, vr in zip(p, v_refs):
                t = _dot(pj.astype(BF16), vr[:, vs])
                o = t if o is None else o + t
            outs.append(o * (1.0 / den))
        o_ref[:, vs] = jnp.where(lane < MLA_V, outs[0], outs[1]).astype(BF16)


def _attention_ctx(q, k, v, nseq, n):
    hp = MLA_HEADS // 2
    return pl.pallas_call(
        functools.partial(_attn_body, nseg=1, npairs=hp),
        grid=(nseq,),
        in_specs=[pl.BlockSpec((n, MLA_HEADS * HEAD_PAD), lambda b: (b, 0)),
                  pl.BlockSpec((n, MLA_HEADS * HEAD_PAD), lambda b: (b, 0)),
                  pl.BlockSpec((n, MLA_HEADS * MLA_V), lambda b: (b, 0))],
        out_specs=pl.BlockSpec((n, MLA_HEADS * MLA_V), lambda b: (b, 0)),
        out_shape=jax.ShapeDtypeStruct((nseq * n, MLA_HEADS * MLA_V), BF16),
        compiler_params=_cparams(("arbitrary",)),
        name="attn_ctx",
    )(q, k, v)


def _attention_lat(q, k, v, kc, vc, layer, row0, nseq, n, tq):
    hp = MLA_HEADS // 2
    P = kc.shape[2]
    qt = n // tq
    q0 = row0 // tq
    s0 = row0 // n
    nq, nv = MLA_HEADS * HEAD_PAD, MLA_HEADS * MLA_V
    return pl.pallas_call(
        functools.partial(_attn_body, nseg=2, npairs=hp),
        grid=(nseq, qt),
        in_specs=[pl.BlockSpec((tq, nq), lambda b, t: (q0 + b * qt + t, 0)),
                  pl.BlockSpec((None, None, P, nq), lambda b, t: (layer, b, 0, 0)),
                  pl.BlockSpec((n, nq), lambda b, t: (s0 + b, 0)),
                  pl.BlockSpec((None, None, P, nv), lambda b, t: (layer, b, 0, 0)),
                  pl.BlockSpec((n, nv), lambda b, t: (s0 + b, 0))],
        out_specs=pl.BlockSpec((tq, nv), lambda b, t: (b * qt + t, 0)),
        out_shape=jax.ShapeDtypeStruct((nseq * n, nv), BF16),
        compiler_params=_cparams(("arbitrary", "arbitrary")),
        name="attn_lat",
    )(q, kc, k, vc, v)


def _head_sums(x, ones_quad):
    return jnp.concatenate([_dot(x[:, a:a + GLA_QUAD], ones_quad) for a in range(0, GLA_W, GLA_QUAD)], axis=1)


def _gla_chunk(q, k, v, g, st_refs, fwd, tri, ones_bd, mild):
    C = GLA_CHUNK
    W = GLA_W
    nquad = W // GLA_QUAD
    rows = lax.broadcasted_iota(I32, (C, 1), 0)

    g1, g2, g3 = _split3(g * LOG2E)
    cum = _dot(tri, jnp.concatenate([g1, g2, g3], axis=1))
    cum = cum[:, 0:W] + cum[:, W:2 * W] + cum[:, 2 * W:3 * W]
    edge = C - 1 if fwd else 0
    last = cum[edge:edge + 1]
    q_in = (q * jnp.exp2(cum)).astype(BF16)
    k_st = (k * jnp.exp2(last - cum)).astype(BF16)
    v_b = v.astype(BF16)

    lane_q = lax.broadcasted_iota(I32, (1, GLA_QUAD), 1)
    head_masks = [jnp.right_shift(lane_q, GLA_DK_SHIFT) == h for h in range(GLA_QUAD // GLA_DK)]
    col_s = jnp.bitwise_and(lane_q, C - 1)

    def stack_heads(xq):
        return jnp.concatenate([jnp.where(mh, xq, jnp.zeros_like(xq)) for mh in head_masks], axis=0)

    a_acc = [jnp.zeros((C, GLA_QUAD), F32) for _ in range(nquad)]
    half = C // 2
    while half >= GLA_DIAG:
        blk = 2 * half
        pieces = []
        for p in range(C // blk):
            rr = p * blk + (half - 1 if fwd else half)
            pieces.append(jnp.broadcast_to(cum[rr:rr + 1], (blk, W)))
        ref = jnp.concatenate(pieces, axis=0) if len(pieces) > 1 else pieces[0]
        upper = jnp.bitwise_and(rows, blk - 1) >= half
        qmask = upper if fwd else jnp.logical_not(upper)
        qe = jnp.where(qmask, q * jnp.exp2(cum - ref), 0.0).astype(BF16)
        ke = jnp.where(qmask, 0.0, k * jnp.exp2(ref - cum)).astype(BF16)
        sh = blk.bit_length() - 1
        same = jnp.right_shift(rows, sh) == jnp.right_shift(col_s, sh)
        for qd in range(nquad):
            sl = slice(qd * GLA_QUAD, (qd + 1) * GLA_QUAD)
            a = _dot_nt(qe[:, sl], stack_heads(ke[:, sl]))
            a_acc[qd] = a_acc[qd] + jnp.where(same, a, 0.0)
        half //= 2

    if mild:
        nb = C // GLA_DIAG
        c3 = cum.reshape(nb, GLA_DIAG, W)
        er = 0 if fwd else GLA_DIAG - 1
        ref = jnp.broadcast_to(c3[:, er:er + 1, :], (nb, GLA_DIAG, W)).reshape(C, W)
        qe = (q * jnp.exp2(cum - ref)).astype(BF16)
        ke = (k * jnp.exp2(ref - cum)).astype(BF16)
        sh = GLA_DIAG.bit_length() - 1
        keep = jnp.right_shift(rows, sh) == jnp.right_shift(col_s, sh)
        keep = jnp.logical_and(keep, (rows >= col_s) if fwd else (rows <= col_s))
        for qd in range(nquad):
            sl = slice(qd * GLA_QUAD, (qd + 1) * GLA_QUAD)
            a = _dot_nt(qe[:, sl], stack_heads(ke[:, sl]))
            a_acc[qd] = a_acc[qd] + jnp.where(keep, a, 0.0)

    r2 =jnp.right_shift(lax.broadcasted_iota(I32, (GLA_QUAD, GLA_QUAD), 0), GLA_DK_SHIFT)
    c2 = jnp.right_shift(lax.broadcasted_iota(I32, (GLA_QUAD, GLA_QUAD), 1), GLA_DK_SHIFT)
    o_parts = []
    for qd in range(nquad):
        sl = slice(qd * GLA_QUAD, (qd + 1) * GLA_QUAD)
        st = st_refs[qd][...]
        o = _dot_nt(q_in[:, sl], st.astype(BF16))
        o = o + _dot(a_acc[qd].astype(BF16), stack_heads(v_b[:, sl]))
        o_parts.append(o)
        upd = _dot_tn(v_b[:, sl], k_st[:, sl])
        st_refs[qd][...] = st * jnp.exp2(last[:, sl]) + jnp.where(r2 == c2, upd, 0.0)
    o = jnp.concatenate(o_parts, axis=1)
    if mild:
        return o

    nb = C // GLA_DIAG
    q3 = q.reshape(nb, GLA_DIAG, W)
    k3 = k.reshape(nb, GLA_DIAG, W)
    v3 = v.reshape(nb, GLA_DIAG, W)
    c3 = cum.reshape(nb, GLA_DIAG, W)
    tl = lax.broadcasted_iota(I32, (1, GLA_DIAG, 1), 1)
    zs = []
    for s in range(GLA_DIAG):
        e = jnp.exp2(c3 - c3[:, s:s + 1, :])
        valid = (tl >= s) if fwd else (tl <= s)
        zs.append(jnp.where(valid, q3 * e * k3[:, s:s + 1, :], 0.0).reshape(C, W).astype(BF16))
    w_all = _head_sums(jnp.concatenate(zs, axis=0), ones_bd)
    od = jnp.zeros((nb, GLA_DIAG, W), F32)
    for s in range(GLA_DIAG):
        od = od + w_all[s * C:(s + 1) * C].reshape(nb, GLA_DIAG, W) * v3[:, s:s + 1, :]
    return o + od.reshape(C, W)


def _gla_body(*refs, n, spb, has_init, has_final):
    it = iter(refs)
    q_ref, k_ref, v_ref, g_ref, sg_ref, gg_ref, trif_ref, trib_ref, ones_ref = (next(it) for _ in range(9))
    s0_ref = next(it) if has_init else None
    o_ref = next(it)
    sf_ref = next(it) if has_final else None
    of_ref, ob_ref = next(it), next(it)
    nquad = GLA_W // GLA_QUAD
    st = [[[next(it) for _ in range(nquad)] for _ in range(2)] for _ in range(spb)]

    C = GLA_CHUNK
    nc = n // C
    hq = GLA_QUAD // GLA_DK
    for sq in range(spb):
        for d in range(2):
            for qd in range(nquad):
                st[sq][d][qd][...] = jnp.zeros((GLA_QUAD, GLA_QUAD), F32)
                if has_init:
                    for h in range(hq):
                        hs = slice(h * GLA_DK, (h + 1) * GLA_DK)
                        st[sq][d][qd][hs, hs] = s0_ref[sq, d, qd * hq + h]

    trif = trif_ref[...]
    trib = trib_ref[...]
    ones_bd = ones_ref[...]

    def step(i, carry, mild):
        for sq in range(spb):
            for d, (tri, acc) in enumerate(((trif, of_ref), (trib, ob_ref))):
                c = i if d == 0 else nc - 1 - i
                rs = pl.ds(pl.multiple_of(sq * n + c * C, C), C)
                gd = g_ref[rs, d * GLA_W:(d + 1) * GLA_W]
                qkv = [r[rs, :].astype(F32) for r in (q_ref, k_ref, v_ref)]
                acc[rs, :] = _gla_chunk(*qkv, gd, st[sq][d], d == 0, tri, ones_bd, mild)
        return carry

    steepest = jnp.max(-g_ref[...]) * (LOG2E * (GLA_DIAG - 1))
    is_mild = steepest < GLA_MILD_LOG2

    @pl.when(is_mild)
    def _():
        lax.fori_loop(0, nc, functools.partial(step, mild=True), 0)

    @pl.when(jnp.logical_not(is_mild))
    def _():
        lax.fori_loop(0, nc, functools.partial(step, mild=False), 0)

    gg = gg_ref[...]
    fr = min(n, 256)

    def fin(i, carry):
        rs = pl.ds(pl.multiple_of(i * fr, fr), fr)
        o = of_ref[rs, :] + ob_ref[rs, :]
        ms = _head_sums(jnp.concatenate(_split3(o * o)[:2], axis=0), ones_bd)
        ms = (ms[0:fr] + ms[fr:2 * fr]) * (1.0 / GLA_DV)
        o_ref[rs, :] = (o * lax.rsqrt(ms + EPS) * gg * sg_ref[rs, :].astype(F32)).astype(BF16)
        return carry

    lax.fori_loop(0, spb * n // fr, fin, 0)

    if has_final:
        for sq in range(spb):
            for d in range(2):
                for qd in range(nquad):
                    s = st[sq][d][qd][...].T
                    for h in range(hq):
                        sf_ref[sq, d, qd * hq + h] = s[h * GLA_DK:(h + 1) * GLA_DK, h * GLA_DK:(h + 1) * GLA_DK]


def _gla(gq, gk, gv, g, sg, g_gla, consts, row0, nseq, n, s0=None, want_final=False):
    trif, trib, ones_bd = consts
    spb = 1
    while (2 * spb <= GLA_SEQS_PER_STEP and 2 * spb * n <= GLA_ROWS_PER_STEP and nseq % (2 * spb) == 0
           and (row0 // n) % (2 * spb) == 0):
        spb *= 2
    b0 = row0 // (n * spb)
    W = GLA_W
    nquad = W // GLA_QUAD
    seq = lambda w: pl.BlockSpec((spb * n, w), lambda b: (b0 + b, 0))
    full = lambda a: pl.BlockSpec(a.shape, lambda b: (0,) * a.ndim)
    in_specs = [seq(W), seq(W), seq(W), seq(2 * W), seq(W), full(g_gla), full(trif), full(trib), full(ones_bd)]
    args = [gq, gk, gv, g, sg, g_gla, trif, trib, ones_bd]
    if s0 is not None:
        s0_arr, layer = s0
        in_specs.append(pl.BlockSpec((spb, None, 2, GLA_HEADS, GLA_DV, GLA_DK), lambda b: (b, layer, 0, 0, 0, 0)))
        args.append(s0_arr)
    out_specs = [pl.BlockSpec((spb * n, W), lambda b: (b, 0))]
    out_shape = [jax.ShapeDtypeStruct((nseq * n, W), BF16)]
    if want_final:
        out_specs.append(pl.BlockSpec((spb, 2, GLA_HEADS, GLA_DK, GLA_DV), lambda b: (b, 0, 0, 0, 0)))
        out_shape.append(jax.ShapeDtypeStruct((nseq, 2, GLA_HEADS, GLA_DK, GLA_DV), F32))
    scratch = [pltpu.VMEM((spb * n, W), F32), pltpu.VMEM((spb * n, W), F32)]
    scratch += [pltpu.VMEM((GLA_QUAD, GLA_QUAD), F32) for _ in range(spb * 2 * nquad)]
    res = pl.pallas_call(
        functools.partial(_gla_body, n=n, spb=spb, has_init=s0 is not None, has_final=want_final),
        grid=(nseq // spb,),
        in_specs=in_specs,
        out_specs=out_specs,
        out_shape=out_shape,
        scratch_shapes=scratch,
        compiler_params=_cparams(("arbitrary",)),
        name="gla_lat" if s0 is not None else "gla_ctx",
    )(*args)
    return res


def _merge_body(xc_ref, xl_ref, mod_ref, oc_ref, ol_ref, gc_ref, gl_ref, g1_ref, wgate_ref, wom_ref, wog_ref,
                wout_ref, g2_ref, wr_ref, x1_ref, h2_ref, aff_ref, *, D, n_ctx_tiles):
    is_ctx = pl.program_id(0) < n_ctx_tiles
    mod = mod_ref[0]
    x = jnp.where(is_ctx, xc_ref[...], xl_ref[...])
    h = _modulated_norm(x, g1_ref[...], mod[:, D:2 * D], mod[:, 0:D]).astype(BF16)
    om = _dot(jnp.where(is_ctx, oc_ref[...], ol_ref[...]), wom_ref[...])
    merged = _sigmoid(_dot(h, wgate_ref[:, 0:D])) * om
    og = _dot(jnp.where(is_ctx, gc_ref[...], gl_ref[...]), wog_ref[...])
    merged = (merged + _sigmoid(_dot(h, wgate_ref[:, D:2 * D])) * og).astype(BF16)
    mix = _dot(merged, wout_ref[...])
    x1 = x + mod[:, 2 * D:3 * D] * mix
    x1_ref[...] = x1
    h2 = _modulated_norm(x1, g2_ref[...], mod[:, 4 * D:5 * D], mod[:, 3 * D:4 * D])
    h2_ref[...] = h2.astype(BF16)
    a, b, _ = _split3(h2)
    wa, wb, _ = _split3(wr_ref[...])
    logits = _dot(jnp.concatenate([a, a, b], axis=1), jnp.concatenate([wa, wb, wa], axis=0))
    lane = lax.broadcasted_iota(I32, logits.shape, 1)
    logits = jnp.where(lane < N_EXPERTS, logits, -jnp.inf)
    p = jnp.exp(logits - logits.max(axis=-1, keepdims=True))
    aff_ref[...] = p / p.sum(axis=-1, keepdims=True)


def _merge(xc, xl, mod, layer, o_ctx, o_lat, og_ctx, og_lat, g1, wgate, wom, wog, wout, g2, wr, cond_of_tile):
    (Tc, D), Tl = xc.shape, xl.shape[0]
    T2 = Tc + Tl
    n_ctx_tiles, n_lat_tiles = Tc // TOKEN_TILE, Tl // TOKEN_TILE
    nt = n_ctx_tiles + n_lat_tiles
    L, R = mod.shape[:2]
    row = lambda w: pl.BlockSpec((TOKEN_TILE, w), lambda i: (i, 0))
    ctx, lat = _ctx_lat_specs(n_ctx_tiles, n_lat_tiles)
    W = o_ctx.shape[1]
    params = [g1, wgate, wom, wog, wout, g2, wr]
    return pl.pallas_call(
        functools.partial(_merge_body, D=D, n_ctx_tiles=n_ctx_tiles),
        grid=(nt,),
        in_specs=[ctx(D), lat(D), _mod_spec(R, D, layer, cond_of_tile), ctx(W), lat(W), ctx(GLA_W), lat(GLA_W)]
                 + [_layer_spec(a, layer) for a in params],
        out_specs=[row(D), row(D), row(LANES)],
        out_shape=[jax.ShapeDtypeStruct((T2, D), F32), jax.ShapeDtypeStruct((T2, D), BF16),
                   jax.ShapeDtypeStruct((T2, LANES), F32)],
        compiler_params=_cparams(("arbitrary",)),
        name="merge_router",
    )(xc, xl, mod.reshape(L * R, 1, 6 * D), o_ctx, o_lat, og_ctx, og_lat, *params)


def _topk_body(a_ref, slot_ref, cb_ref, *, T, cap):
    a = a_ref[0]
    E = a.shape[0]

    def count_ge(thr):
        return jnp.sum((a >= thr).astype(F32), axis=1, keepdims=True)

    hi = jnp.full((E, 1), 2.0, F32)
    for j in range(TOPK_EXP_BITS - 1, -1, -1):
        cand = hi * (2.0 ** -(2 ** j))
        hi = jnp.where(count_ge(cand) < cap, cand, hi)
    lo = jnp.where(hi > F32_MIN_NORMAL, 0.5 * hi, 0.0)

    def bisect(_, lh):
        lo, hi = lh
        mid = 0.5 * (lo + hi)
        up = count_ge(mid) >= cap
        return jnp.where(up, mid, lo), jnp.where(up, hi, mid)

    lo, hi = lax.fori_loop(0, TOPK_BISECT_STEPS, bisect, (lo, hi))
    gt = a >= hi
    eq = jnp.logical_and(a >= lo, a < hi)
    need = cap - jnp.sum(gt.astype(F32), axis=1, keepdims=True)

    r = lax.broadcasted_iota(I32, (LANES, LANES), 0)
    c = lax.broadcasted_iota(I32, (LANES, LANES), 1)
    triu = (r < c).astype(BF16)
    lane = lax.broadcasted_iota(I32, (E, LANES), 1)

    nb = T // LANES
    per_tile = ROW_TILE // LANES
    carry_eq = jnp.zeros((E, 1), F32)
    carry_sel = jnp.zeros((E, 1), F32)
    cb = jnp.zeros((E, LANES), I32)
    for j in range(nb):
        sl = slice(j * LANES, (j + 1) * LANES)
        eq_j = eq[:, sl].astype(BF16)
        pre = _dot(eq_j, triu) + carry_eq
        carry_eq = carry_eq + jnp.sum(eq_j.astype(F32), axis=1, keepdims=True)
        sel = jnp.logical_or(gt[:, sl], jnp.logical_and(eq[:, sl], pre < need))
        sel_b = sel.astype(BF16)
        if j % per_tile == 0:
            cb = jnp.where(lane == j // per_tile, carry_sel.astype(I32), cb)
        slot = (_dot(sel_b, triu) + carry_sel).astype(I32)
        carry_sel = carry_sel + jnp.sum(sel_b.astype(F32), axis=1, keepdims=True)
        slot_ref[0, :, sl] = jnp.where(sel, slot, -1)
    cb_ref[0] = jnp.where(lane == nb // per_tile, carry_sel.astype(I32), cb)


def _topk(aff_t, cap):
    G, E, T = aff_t.shape
    return pl.pallas_call(
        functools.partial(_topk_body, T=T, cap=cap),
        grid=(G,),
        in_specs=[pl.BlockSpec((1, E, T), lambda g: (g, 0, 0))],
        out_specs=[pl.BlockSpec((1, E, T), lambda g: (g, 0, 0)),
                   pl.BlockSpec((1, E, LANES), lambda g: (g, 0, 0))],
        out_shape=[jax.ShapeDtypeStruct((G, E, T), I32), jax.ShapeDtypeStruct((G, E, LANES), I32)],
        compiler_params=_cparams(("arbitrary",)),
        name="expert_topk",
    )(aff_t)


def _expert_body(cb_ref, slot_ref, aff_ref, h_ref, wg_ref, wu_ref, wd_ref, y_ref, xs_ref, acc_ref, ws_ref, *,
                 cap, win, nt, nfh, E):
    e = pl.program_id(0)
    g = pl.program_id(1)
    fh = pl.program_id(2)
    base = (g * E + e) * LANES
    tok0 = g * (nt * ROW_TILE)

    @pl.when(fh == 0)
    def _gather():
        xs_ref[...] = jnp.zeros_like(xs_ref)
        ws_ref[...] = jnp.zeros_like(ws_ref)
        slab = win

        def slab_start(i):
            return jnp.minimum(jnp.bitwise_and(cb_ref[base + i], -SUBLANES), cap - slab)

        def tile_fits(i, ok):
            return jnp.logical_and(ok, cb_ref[base + i + 1] <= slab_start(i) + slab)

        fits = lax.fori_loop(0, nt, tile_fits, jnp.bool_(True))

        def tile_slab(i, carry):
            r0 = pl.multiple_of(slab_start(i), SUBLANES)
            j = lax.broadcasted_iota(I32, (slab, 1), 0) + r0
            hit = slot_ref[i] == j
            hs = h_ref[pl.ds(pl.multiple_of(tok0 + i * ROW_TILE, ROW_TILE), ROW_TILE), :]
            xs_ref[pl.ds(r0, slab), :] += _dot(hit.astype(BF16), hs)
            ws_ref[pl.ds(r0, slab), :] += jnp.sum(jnp.where(hit, aff_ref[i], 0.0), axis=1, keepdims=True)
            return carry

        @pl.when(fits)
        def _():
            lax.fori_loop(0, nt, tile_slab, 0, unroll=True)

        def tile(i, carry):
            lo = cb_ref[base + i]
            hi = cb_ref[base + i + 1]
            srow = slot_ref[i]
            arow = aff_ref[i]
            for w in range(cap // win):
                @pl.when(jnp.logical_and(lo < (w + 1) * win, hi > w * win))
                def _():
                    j = lax.broadcasted_iota(I32, (win, 1), 0) + w * win
                    hit = srow == j
                    hs = h_ref[pl.ds(pl.multiple_of(tok0 + i * ROW_TILE, ROW_TILE), ROW_TILE), :]
                    xs_ref[w * win:(w + 1) * win, :] += _dot(hit.astype(BF16), hs)
                    ws_ref[w * win:(w + 1) * win, :] += jnp.sum(jnp.where(hit, arow, 0.0), axis=1, keepdims=True)
            return carry

        @pl.when(jnp.logical_not(fits))
        def _():
            lax.fori_loop(0, nt, tile, 0)

    xb = xs_ref[...].astype(BF16)
    gate = _dot(xb, wg_ref[...].astype(BF16))
    up = _dot(xb, wu_ref[...].astype(BF16))
    hid = (gate * _sigmoid(gate) * up).astype(BF16)
    part = _dot(hid, wd_ref[...].astype(BF16))

    @pl.when(fh == 0)
    def _():
        acc_ref[...] = part

    @pl.when(fh > 0)
    def _():
        acc_ref[...] += part

    @pl.when(fh == nfh - 1)
    def _():
        y_ref[...] = (acc_ref[...] * ws_ref[...]).astype(BF16)


def _experts(cb_flat, slot5, aff5, h2, w_gate, w_up, w_down, layer, cap, win):
    G, E, nt = slot5.shape[:3]
    T = nt * ROW_TILE
    D = h2.shape[1]
    FF = w_gate.shape[-1]
    nfh = 1
    fb = FF // nfh
    grid_spec = pltpu.PrefetchScalarGridSpec(
        num_scalar_prefetch=1,
        grid=(E, G, nfh),
        in_specs=[pl.BlockSpec((None, None, nt, 1, ROW_TILE), lambda e, g, f, cb: (g, e, 0, 0, 0)),
                  pl.BlockSpec((None, None, nt, 1, ROW_TILE), lambda e, g, f, cb: (g, e, 0, 0, 0)),
                  pl.BlockSpec((G * T, D), lambda e, g, f, cb: (0, 0), pipeline_mode=pl.Buffered(1)),
                  pl.BlockSpec((None, None, D, fb), lambda e, g, f, cb: (layer, e, 0, f)),
                  pl.BlockSpec((None, None, D, fb), lambda e, g, f, cb: (layer, e, 0, f)),
                  pl.BlockSpec((None, None, fb, D), lambda e, g, f, cb: (layer, e, f, 0))],
        out_specs=pl.BlockSpec((None, None, cap, D), lambda e, g, f, cb: (g, e, 0, 0)),
        scratch_shapes=[pltpu.VMEM((cap, D), F32), pltpu.VMEM((cap, D), F32), pltpu.VMEM((cap, 1), F32)],
    )
    return pl.pallas_call(
        functools.partial(_expert_body, cap=cap, win=win, nt=nt, nfh=nfh, E=E),
        grid_spec=grid_spec,
        out_shape=jax.ShapeDtypeStruct((G, E, cap, D), BF16),
        compiler_params=_cparams(("arbitrary", "arbitrary", "arbitrary")),
        name="expert_ffn",
    )(cb_flat, slot5, aff5, h2, w_gate, w_up, w_down)


def _combine_body(cb_ref, x_ref, mod_ref, slot_ref, y_ref, spread_ref, gf_ref, oc_ref, ol_ref, acc_ref, *,
                  D, cap, win, cw, E, tiles_per_group, final):
    i = pl.program_id(0)
    g = i // tiles_per_group
    ti = i % tiles_per_group
    slots = slot_ref[...]
    los = [cb_ref[(g * E + e) * LANES + ti] for e in range(E)]
    his = [cb_ref[(g * E + e) * LANES + ti + 1] for e in range(E)]
    starts = [jnp.minimum(jnp.bitwise_and(lo, -BF16_ROWS), cap - cw) for lo in los]
    fits = his[0] <= starts[0] + cw
    for e in range(1, E):
        fits = jnp.logical_and(fits, his[e] <= starts[e] + cw)

    @pl.when(fits)
    def _fast():
        sp1 = slots + 1
        digits = jnp.concatenate([jnp.right_shift(sp1, 4), jnp.bitwise_and(sp1, 15)], axis=1)
        spread = _dot(digits.astype(F32).astype(BF16), spread_ref[...])
        lane = lax.broadcasted_iota(I32, (1, cw), 1)
        tgt = jnp.concatenate([lane + (starts[e] + 1) for e in range(E)], axis=1).astype(F32)
        onehot = (spread == tgt).astype(BF16)
        rows = jnp.concatenate([y_ref[e, pl.ds(pl.multiple_of(starts[e], BF16_ROWS), cw), :] for e in range(E)],
                               axis=0)
        acc_ref[...] = _dot(onehot, rows)

    @pl.when(jnp.logical_not(fits))
    def _general():
        acc_ref[...] = jnp.zeros_like(acc_ref)
        for e in range(E):
            col = slots[:, e:e + 1]
            for w in range(cap // win):
                @pl.when(jnp.logical_and(los[e] < (w + 1) * win, his[e] > w * win))
                def _():
                    j = lax.broadcasted_iota(I32, (1, win), 1) + w * win
                    oh = (col == j).astype(BF16)
                    acc_ref[...] += _dot(oh, y_ref[e, w * win:(w + 1) * win, :])

    mod = mod_ref[0]
    x2 = x_ref[...] + mod[:, 5 * D:6 * D] * acc_ref[...]
    if final:
        x2 = _rms(x2, gf_ref[...])

    @pl.when(g == 0)
    def _():
        oc_ref[...] = x2

    @pl.when(g != 0)
    def _():
        ol_ref[...] = x2


def _combine(cb_flat, x1, mod_l, slot_t, y, g_final, cond_of_tile, cap, win, final):
    T2, D = x1.shape
    G, E = y.shape[:2]
    nt = T2 // ROW_TILE
    tpg = nt // G
    R = mod_l.shape[0]
    cw = min(LANES, cap)
    spread = np.zeros((2 * LANES, E * cw), np.float32)
    for e in range(E):
        spread[e, e * cw:(e + 1) * cw] = 16.0
        spread[LANES + e, e * cw:(e + 1) * cw] = 1.0
    spread = jnp.asarray(spread, BF16)
    assert G == 2
    ctx, lat = _ctx_lat_specs(tpg, tpg, ROW_TILE)
    grid_spec = pltpu.PrefetchScalarGridSpec(
        num_scalar_prefetch=1,
        grid=(nt,),
        in_specs=[pl.BlockSpec((ROW_TILE, D), lambda i, cb: (i, 0)),
                  pl.BlockSpec((1, 1, 6 * D), lambda i, cb: (cond_of_tile(i), 0, 0)),
                  pl.BlockSpec((ROW_TILE, LANES), lambda i, cb: (i, 0)),
                  pl.BlockSpec((None, E, cap, D), lambda i, cb: (i // tpg, 0, 0, 0)),
                  pl.BlockSpec(spread.shape, lambda i, cb: (0, 0)),
                  pl.BlockSpec((1, D), lambda i, cb: (0, 0))],
        out_specs=[ctx(D), lat(D)],
        scratch_shapes=[pltpu.VMEM((ROW_TILE, D), F32)],
    )
    return pl.pallas_call(
        functools.partial(_combine_body, D=D, cap=cap, win=win, cw=cw, E=E, tiles_per_group=tpg, final=final),
        grid_spec=grid_spec,
        out_shape=[jax.ShapeDtypeStruct((T2 // G, D), F32), jax.ShapeDtypeStruct((T2 // G, D), F32)],
        compiler_params=_cparams(("arbitrary",)),
        name="moe_combine",
    )(cb_flat, x1, mod_l.reshape(R, 1, 6 * D), slot_t, y, spread, g_final)


def _pack_weights(w_in, w_uq, w_uk, w_uv, w_gla_gate, b_gla_gate):
    L, D, _ = w_in.shape
    sizes = (Q_RANK, KV_RANK, MLA_ROPE, GLA_W, GLA_W, GLA_W, 2 * GLA_GATE_RANK, GLA_W, D, D)
    idx = np.cumsum(sizes)[:-1]
    pq, pkv, kr, gq, gk, gv, glr, gog, ga, gb = jnp.split(w_in, [int(i) for i in idx], axis=-1)
    npair = MLA_ROPE // 4
    swap = np.concatenate([np.arange(npair, 2 * npair), np.arange(0, npair),
                           np.arange(3 * npair, 4 * npair), np.arange(2 * npair, 3 * npair)])

    def slot_rope(w):
        return jnp.pad(w, ((0, 0), (0, 0), (MLA_NOPE, HEAD_PAD - MLA_NOPE - MLA_ROPE)))

    glr_p = jnp.pad(glr, ((0, 0), (0, 0), (0, LANES - 2 * GLA_GATE_RANK)))
    wp = jnp.concatenate([pq, pkv, slot_rope(kr), slot_rope(kr[..., swap]), gq, gk, gv, glr_p, gog],
                         axis=-1).astype(BF16)
    wgate = jnp.concatenate([ga, gb], axis=-1).astype(BF16)

    uq = w_uq.reshape(L, Q_RANK, MLA_HEADS, MLA_NOPE + MLA_ROPE)
    pad_h = HEAD_PAD - MLA_NOPE - MLA_ROPE
    uq_n = jnp.pad(uq, ((0, 0), (0, 0), (0, 0), (0, pad_h))).reshape(L, Q_RANK, MLA_HEADS * HEAD_PAD)
    uq_s = jnp.concatenate([jnp.zeros_like(uq[..., :MLA_NOPE]), uq[..., MLA_NOPE:][..., swap]], axis=-1)
    uq_s = jnp.pad(uq_s, ((0, 0), (0, 0), (0, 0), (0, pad_h))).reshape(L, Q_RANK, MLA_HEADS * HEAD_PAD)
    wuq = jnp.concatenate([uq_n, uq_s], axis=-1).astype(BF16)

    uk = w_uk.reshape(L, KV_RANK, MLA_HEADS, MLA_NOPE)
    wuk = jnp.pad(uk, ((0, 0), (0, 0), (0, 0), (0, HEAD_PAD - MLA_NOPE))).reshape(
        L, KV_RANK, MLA_HEADS * HEAD_PAD).astype(BF16)
    wuv = w_uv.astype(BF16)

    wg = jnp.zeros((L, LANES, 2 * GLA_W), F32)
    wg = wg.at[:, 0:GLA_GATE_RANK, 0:GLA_W].set(w_gla_gate[:, 0])
    wg = wg.at[:, GLA_GATE_RANK:2 * GLA_GATE_RANK, GLA_W:].set(w_gla_gate[:, 1])
    bg = b_gla_gate.reshape(L, 1, 2 * GLA_W)
    return wp, wgate, wuq, wuk, wuv, wg.astype(BF16), bg


def _rope_tables(n_lat):
    npair = MLA_ROPE // 4
    freqs = ROPE_BASE ** (-jnp.arange(npair, dtype=F32) / npair)
    pos = jnp.arange(n_lat)
    ang_r = (pos // GRID_W).astype(F32)[:, None] * freqs
    ang_c = (pos % GRID_W).astype(F32)[:, None] * freqs
    cr, sr, cc, sc = jnp.cos(ang_r), jnp.sin(ang_r), jnp.cos(ang_c), jnp.sin(ang_c)
    cos32 = jnp.concatenate([cr, cr, cc, cc], axis=-1)
    sin32 = jnp.concatenate([-sr, sr, -sc, sc], axis=-1)
    pad_h = HEAD_PAD - MLA_NOPE - MLA_ROPE
    ones = jnp.ones((n_lat, MLA_NOPE), F32)
    cos_l = jnp.concatenate([ones, cos32, jnp.zeros((n_lat, pad_h), F32)], axis=-1)
    sin_l = jnp.pad(sin32, ((0, 0), (MLA_NOPE, pad_h)))
    cos_i = jnp.concatenate([jnp.ones((INPROJ_TILE, MLA_NOPE + MLA_ROPE), F32),
                             jnp.zeros((INPROJ_TILE, pad_h), F32)], -1)
    sin_i = jnp.zeros((INPROJ_TILE, HEAD_PAD), F32)
    return jnp.concatenate([cos_i, cos_l], 0), jnp.concatenate([sin_i, sin_l], 0)


def _gla_consts():
    C = GLA_CHUNK
    r = np.arange(C)
    trif = (r[None, :] <= r[:, None]).astype(np.float32)
    trib = (r[None, :] >= r[:, None]).astype(np.float32)
    h = np.arange(GLA_QUAD) // GLA_DK
    ones_bd = (h[:, None] == h[None, :]).astype(np.float32)
    return jnp.asarray(trif, BF16), jnp.asarray(trib, BF16), jnp.asarray(ones_bd, BF16)


def kernel(x_prompt, x_sample, cache_ckv, cache_krope, state_gla, c, c_ctx, w_mod, b_mod, g_norm1, g_norm2, w_in, g_q, g_kv, w_uq, w_uk, w_uv, w_o_mla, w_gla_gate, b_gla_gate, g_gla, w_o_gla, w_out, w_router, w_e_gate, w_e_up, w_e_down, g_final):
    B, N, D = x_prompt.shape
    DB, DN, _ = x_sample.shape
    L = w_in.shape[0]
    Tc, Tl = B * N, DB * DN
    assert Tc == Tl and Tc % DN == 0 and N % ROW_TILE == 0 and DN % TOKEN_TILE == 0 and Tc % TOKEN_TILE == 0
    assert N % GLA_CHUNK == 0 and DN % GLA_CHUNK == 0 and DN % GRID_W == 0
    assert DN % INPROJ_TILE == 0 and Tc % INPROJ_TILE == 0
    T = Tc
    G = 2
    cap = max(1, CAPACITY_FACTOR * T // N_EXPERTS)
    win = LANES if cap % LANES == 0 else cap
    assert cap % win == 0 and win % SUBLANES == 0 and 1 + DB <= SUBLANES

    def tile_maps(rows):
        nct, per_seq = Tc // rows, DN // rows
        cond = lambda i: jnp.where(i < nct, 0, 1 + (i - nct) // per_seq)
        tab = lambda i: jnp.where(i < nct, 0, 1 + (i - nct) % per_seq)
        return cond, tab

    cond_tok, _ = tile_maps(TOKEN_TILE)
    cond_in, tab_in = tile_maps(INPROJ_TILE)
    cond_row, _ = tile_maps(ROW_TILE)
    n_ctx_tiles = Tc // TOKEN_TILE

    cvec = jnp.concatenate([c_ctx[None, :], c, jnp.zeros((SUBLANES - 1 - DB, D), F32)], axis=0)
    mod = _adaln_all(cvec, w_mod, b_mod)

    wp, wgate, wuq, wuk, wuv, wg, bg = _pack_weights(w_in, w_uq, w_uk, w_uv, w_gla_gate, b_gla_gate)
    cos_t, sin_t = _rope_tables(DN)
    gla_consts = _gla_consts()
    wom = w_o_mla.astype(BF16)
    wog = w_o_gla.astype(BF16)
    wout = w_out.astype(BF16)
    wr = jnp.pad(w_router, ((0, 0), (0, 0), (0, LANES - N_EXPERTS)))
    g1s, g2s, gqs, gkvs = g_norm1[:, None, :], g_norm2[:, None, :], g_q[:, None, :], g_kv[:, None, :]

    ckr_pad = jnp.pad(cache_krope, ((0, 0), (0, 0), (0, 0), (MLA_NOPE, HEAD_PAD - MLA_NOPE - MLA_ROPE)))
    kc_all, vc_all = _cache_kv(cache_ckv, ckr_pad, wuk, wuv)
    st_t = jnp.swapaxes(state_gla, -1, -2)

    xc, xl = x_prompt.reshape(Tc, D), x_sample.reshape(Tl, D)
    ckv_list, kr_list, gla_list = [], [], []
    for l in range(L):
        pre = _inproj(xc, xl, mod, l, g1s, wp, gqs, gkvs, wuq, wuk, wuv, wg, bg, cos_t, sin_t, cond_in, tab_in)
        ckv_list.append(pre["ckv"][:Tc].reshape(B, N, KV_RANK))
        kr_list.append(pre["kr"][:Tc].reshape(B, N, MLA_ROPE))

        o_ctx = _attention_ctx(pre["q"], pre["k"], pre["v"], B, N)
        o_lat = _attention_lat(pre["q"], pre["k"], pre["v"], kc_all, vc_all, l, Tc, DB, DN, min(DN, 2 * TOKEN_TILE))

        gg = g_gla[l][None]
        og_ctx, s_fin = _gla(pre["gq"], pre["gk"], pre["gv"], pre["g"], pre["sg"], gg, gla_consts, 0, B, N,
                             want_final=True)
        (og_lat,) = _gla(pre["gq"], pre["gk"], pre["gv"], pre["g"], pre["sg"], gg, gla_consts, Tc, DB, DN,
                         s0=(st_t, l))
        gla_list.append(s_fin)

        x1, h2, aff = _merge(xc, xl, mod, l, o_ctx, o_lat, og_ctx, og_lat, g1s, wgate, wom, wog, wout, g2s, wr,
                             cond_tok)

        aff_t = jnp.swapaxes(aff[:, :N_EXPERTS].reshape(G, T, N_EXPERTS), 1, 2)
        slot, cb = _topk(aff_t, cap)
        cb_flat = cb.reshape(-1)
        slot5 = slot.reshape(G, N_EXPERTS, T // ROW_TILE, 1, ROW_TILE)
        slot_t = jnp.pad(jnp.swapaxes(slot, 1, 2).reshape(G * T, N_EXPERTS),
                         ((0, 0), (0, LANES - N_EXPERTS)), constant_values=-1)
        aff5 = aff_t.reshape(G, N_EXPERTS, T // ROW_TILE, 1, ROW_TILE)
        y = _experts(cb_flat, slot5, aff5, h2, w_e_gate, w_e_up, w_e_down, l, cap, win)
        xc, xl = _combine(cb_flat, x1, mod[l], slot_t, y, g_final[None], cond_row, cap, win, final=(l == L - 1))

    y_prompt = xc.reshape(B, N, D)
    y_sample = xl.reshape(DB, DN, D)
    new_ckv = jnp.stack(ckv_list, axis=1)
    new_krope = jnp.stack(kr_list, axis=1)
    new_gla = jnp.stack(gla_list, axis=1)
    return (y_prompt, y_sample, new_ckv, new_krope, new_gla)
```
